```python
import math
import jax, jax.numpy as jnp
from jax import lax
import numpy as np

D_MODEL = 1024
BATCH = 8
SEQ = 4096
DEPTH = 1

GRID_W = 64
MIX_WIDTH = 1024
ATTN_WIDTH = 512
HEAD_DIM = 64
N_Q_HEADS = ATTN_WIDTH // HEAD_DIM
N_KV_HEADS = 2
KV_WIDTH = N_KV_HEADS * HEAD_DIM
AXIS_DIM = HEAD_DIM // 2
ROPE_THETA = 10000.0
Q_BLOCK = 128
HG_WIDTH = 512
HG_EXPAND = 128
HG_HEADS = HG_WIDTH // HG_EXPAND
HG_CHUNK = 64
RMS_EPS = 1e-6
LN_EPS = 1e-5
DEEPNORM_ALPHA = (2 * DEPTH) ** 0.25
DEEPNORM_BETA = (8 * DEPTH) ** -0.25
IN_SIZES = (ATTN_WIDTH, KV_WIDTH, KV_WIDTH, ATTN_WIDTH,
            HG_WIDTH, HG_WIDTH, HG_WIDTH, HG_WIDTH, HG_WIDTH)
IN_WIDTH = sum(IN_SIZES)

kernel_name = "hymba_attn_hgrn2_deepnorm_encoder"


def rms_norm(x, w):
    xf = x.astype(jnp.float32)
    y = xf * lax.rsqrt(jnp.mean(xf * xf, axis=-1, keepdims=True) + RMS_EPS)
    return y * w.astype(jnp.float32)


def layer_norm(x, w, b):
    xf = x.astype(jnp.float32)
    mu = jnp.mean(xf, axis=-1, keepdims=True)
    xc = xf - mu
    var = jnp.mean(xc * xc, axis=-1, keepdims=True)
    return xc * lax.rsqrt(var + LN_EPS) * w.astype(jnp.float32) + b.astype(jnp.float32)


def axial_rope_tables(seq_len):
    rows = seq_len // GRID_W
    row = jnp.repeat(jnp.arange(rows, dtype=jnp.int32), GRID_W).astype(jnp.float32)
    col = jnp.tile(jnp.arange(GRID_W, dtype=jnp.int32), rows).astype(jnp.float32)
    inv = ROPE_THETA ** (-jnp.arange(0, AXIS_DIM, 2, dtype=jnp.float32) / AXIS_DIM)
    ang_r = row[:, None] * inv[None, :]
    ang_c = col[:, None] * inv[None, :]
    return jnp.cos(ang_r), jnp.sin(ang_r), jnp.cos(ang_c), jnp.sin(ang_c)


def _rotate(x, cos, sin):
    x1, x2 = jnp.split(x, 2, axis=-1)
    c = cos[:, None, :]
    s = sin[:, None, :]
    return jnp.concatenate([x1 * c - x2 * s, x2 * c + x1 * s], axis=-1)


def axial_rope(x, tabs):
    cos_r, sin_r, cos_c, sin_c = tabs
    x_row, x_col = jnp.split(x, 2, axis=-1)
    return jnp.concatenate([_rotate(x_row, cos_r, sin_r), _rotate(x_col, cos_c, sin_c)], axis=-1)


def block_attention(q, k, v):
    B, S = q.shape[0], q.shape[1]
    nb = S // Q_BLOCK
    G = N_Q_HEADS // N_KV_HEADS
    qb = q.reshape(B, nb, Q_BLOCK, N_KV_HEADS, G, HEAD_DIM).transpose(1, 0, 2, 3, 4, 5)
    scale = HEAD_DIM ** -0.5

    def one_block(qblk):
        s = jnp.einsum('bqhgd,bkhd->bhgqk', qblk, k).astype(jnp.float32) * scale
        p = jax.nn.softmax(s, axis=-1).astype(v.dtype)
        return jnp.einsum('bhgqk,bkhd->bqhgd', p, v)

    o = lax.map(one_block, qb)
    return o.transpose(1, 0, 2, 3, 4, 5).reshape(B, S, N_Q_HEADS * HEAD_DIM)


def hgrn2_chunk_scan(q, logf, i):
    B, T, H, K = q.shape
    V = i.shape[-1]
    N = T // HG_CHUNK
    q = q.reshape(B, N, HG_CHUNK, H, K)
    logf = logf.reshape(B, N, HG_CHUNK, H, K)
    i = i.reshape(B, N, HG_CHUNK, H, V)
    k = -jnp.expm1(logf)
    b = jnp.cumsum(logf, axis=2)
    b_last = b[:, :, -1:]
    b_mid = b[:, :, HG_CHUNK // 2:HG_CHUNK // 2 + 1]
    qm = q * jnp.exp(b - b_mid)
    km = k * jnp.exp(b_mid - b)
    A = jnp.einsum('bnthk,bnshk->bnhts', qm, km)
    tril = jnp.tril(jnp.ones((HG_CHUNK, HG_CHUNK), dtype=bool))
    A = jnp.where(tril, A, 0.0)
    o_intra = jnp.einsum('bnhts,bnshv->bnthv', A, i)
    U = jnp.einsum('bnshk,bnshv->bnhkv', k * jnp.exp(b_last - b), i)
    decay = jnp.exp(b_last[:, :, 0])

    def step(S, inp):
        d, u = inp
        return d[..., None] * S + u, S

    S0 = jnp.zeros((B, H, K, V), jnp.float32)
    _, S_prev = lax.scan(step, S0, (decay.swapaxes(0, 1), U.swapaxes(0, 1)))
    o_inter = jnp.einsum('bnthk,nbhkv->bnthv', q * jnp.exp(b), S_prev)
    return (o_intra + o_inter).reshape(B, T, H, V)


def hgrn2_bidirectional(q, i, z_fwd, z_bwd, lb):
    B, S = q.shape[0], q.shape[1]
    heads = lambda a: a.astype(jnp.float32).reshape(B, S, HG_HEADS, HG_EXPAND)
    qh = heads(jax.nn.silu(q.astype(jnp.float32))) * (HG_EXPAND ** -0.5)
    ih = heads(i)
    lbh = lb.reshape(2, HG_HEADS, HG_EXPAND)
    logf_f = jnp.log(lbh[0] + (1.0 - lbh[0]) * jax.nn.sigmoid(heads(z_fwd)))
    logf_b = jnp.log(lbh[1] + (1.0 - lbh[1]) * jax.nn.sigmoid(heads(z_bwd)))
    o_f = hgrn2_chunk_scan(qh, logf_f, ih)
    o_b = hgrn2_chunk_scan(qh[:, ::-1], logf_b[:, ::-1], ih[:, ::-1])[:, ::-1]
    return o_f + o_b


def setup_inputs(seed: int = 0) -> dict:
    key = jax.random.key(seed)
    ks = jax.random.split(key, 12)
    f32 = jnp.float32
    x = jax.random.normal(ks[0], (BATCH, SEQ, D_MODEL), f32)
    col_scale = jnp.concatenate([
        jnp.full((s,), DEEPNORM_BETA if idx in (2, 5) else 1.0, f32)
        for idx, s in enumerate(IN_SIZES)])
    w_in = jax.random.normal(ks[1], (DEPTH, D_MODEL, IN_WIDTH), f32) * (D_MODEL ** -0.5) * col_scale
    q_norm_w = 1.0 + 0.02 * jax.random.normal(ks[2], (DEPTH, HEAD_DIM), f32)
    k_norm_w = 1.0 + 0.02 * jax.random.normal(ks[3], (DEPTH, HEAD_DIM), f32)
    attn_norm_w = 1.0 + 0.02 * jax.random.normal(ks[4], (DEPTH, ATTN_WIDTH), f32)
    hg_lb_logits = 0.1 * jax.random.normal(ks[5], (2, DEPTH + 1, HG_WIDTH), f32)
    hg_norm_w = 1.0 + 0.02 * jax.random.normal(ks[6], (DEPTH, HG_EXPAND), f32)
    w_out = jax.random.normal(ks[7], (DEPTH, MIX_WIDTH, D_MODEL), f32) * (MIX_WIDTH ** -0.5) * DEEPNORM_BETA
    ln_w = 1.0 + 0.02 * jax.random.normal(ks[8], (DEPTH, D_MODEL), f32)
    ln_b = 0.02 * jax.random.normal(ks[9], (DEPTH, D_MODEL), f32)
    return {"x": x, "w_in": w_in, "q_norm_w": q_norm_w, "k_norm_w": k_norm_w,
            "attn_norm_w": attn_norm_w, "hg_lb_logits": hg_lb_logits, "hg_norm_w": hg_norm_w,
            "w_out": w_out, "ln_w": ln_w, "ln_b": ln_b}


def reference(x, w_in, q_norm_w, k_norm_w, attn_norm_w, hg_lb_logits, hg_norm_w, w_out, ln_w, ln_b):
    B, S = x.shape[0], x.shape[1]
    dt = x.dtype
    tabs = axial_rope_tables(S)
    lb_all = jnp.cumsum(jax.nn.softmax(hg_lb_logits.astype(jnp.float32), axis=1), axis=1)
    splits = [int(v) for v in np.cumsum(IN_SIZES)[:-1]]
    for l in range(DEPTH):
        proj = jnp.einsum('bsd,de->bse', x, w_in[l])
        aq, ak, av, ag, hq, hi, hff, hfb, hg = jnp.split(proj, splits, axis=-1)
        q = rms_norm(aq.reshape(B, S, N_Q_HEADS, HEAD_DIM), q_norm_w[l])
        k = rms_norm(ak.reshape(B, S, N_KV_HEADS, HEAD_DIM), k_norm_w[l])
        q = axial_rope(q, tabs).astype(dt)
        k = axial_rope(k, tabs).astype(dt)
        v = av.reshape(B, S, N_KV_HEADS, HEAD_DIM)
        attn = block_attention(q, k, v)
        attn_branch = rms_norm(attn, attn_norm_w[l]) * jax.nn.silu(ag.astype(jnp.float32))
        o = hgrn2_bidirectional(hq, hi, hff, hfb, lb_all[:, l, :])
        o = rms_norm(o, hg_norm_w[l]).reshape(B, S, HG_WIDTH)
        hg_branch = o * jax.nn.silu(hg.astype(jnp.float32))
        mix = jnp.concatenate([attn_branch, hg_branch], axis=-1).astype(dt)
        y = jnp.einsum('bse,ed->bsd', mix, w_out[l])
        x = layer_norm(DEEPNORM_ALPHA * x + y, ln_w[l], ln_b[l]).astype(dt)
    return x
```

```python
import functools

import jax
import jax.numpy as jnp
from jax import lax
from jax.experimental import pallas as pl
from jax.experimental.pallas import tpu as pltpu

F32 = jnp.float32
BF16 = jnp.bfloat16

GRID_W = 64
ATTN_WIDTH = 512
HEAD_DIM = 64
N_Q_HEADS = 8
N_KV_HEADS = 2
KV_WIDTH = 128
AXIS_DIM = 32
ROPE_THETA = 10000.0
HG_WIDTH = 512
HG_EXPAND = 128
HG_HEADS = 4
HG_CHUNK = 64
RMS_EPS = 1e-6
LN_EPS = 1e-5
DEPTH = 1
DEEPNORM_ALPHA = (2 * DEPTH) ** 0.25
IN_WIDTH = 3840

LANES = 128
VMEM_LIMIT = 56 * 1024 * 1024

PROJ_TM = 512
ATTN_TQ = 128
ATTN_TK = 512
HG_UNROLL = 8
OUT_TM = 512


def _sigmoid(x):
    return 1.0 / (1.0 + jnp.exp(-x))


def _proj_kernel(x_ref, w_ref, cos_ref, sin_ref, qw_ref, kw_ref, lbl_ref, seg_ref,
                 q_ref, kt_ref, v_ref, ag_ref, hq_ref, hi_ref, lf_ref, hg_ref):
    xb = x_ref[...].astype(BF16)
    tm = xb.shape[0]

    def proj(a, b):
        return jnp.dot(xb, w_ref[:, a:b], preferred_element_type=F32)

    cosv = cos_ref[...]
    sinv = sin_ref[...]
    lane = lax.broadcasted_iota(jnp.int32, (tm, LANES), 1)
    first_half = (lane % AXIS_DIM) < (AXIS_DIM // 2)

    def norm_rope(a, w_row, seg, scale):
        ms = jnp.dot((a * a).astype(BF16), seg, preferred_element_type=F32)
        y = a * lax.rsqrt(ms + RMS_EPS) * w_row
        outs = []
        for c in range(a.shape[1] // LANES):
            yc = y[:, c * LANES:(c + 1) * LANES]
            partner = jnp.where(first_half,
                                pltpu.roll(yc, LANES - AXIS_DIM // 2, 1),
                                pltpu.roll(yc, AXIS_DIM // 2, 1))
            outs.append((yc * cosv + partner * sinv) * scale)
        return outs

    aq = proj(0, ATTN_WIDTH)
    qs = norm_rope(aq, qw_ref[...], seg_ref[...], HEAD_DIM ** -0.5)
    for c, qc in enumerate(qs):
        q_ref[:, c * LANES:(c + 1) * LANES] = qc.astype(BF16)
    akv = proj(ATTN_WIDTH, ATTN_WIDTH + 2 * KV_WIDTH)
    kk = norm_rope(akv[:, :KV_WIDTH], kw_ref[...], seg_ref[:KV_WIDTH, :KV_WIDTH], 1.0)[0]
    kt_ref[...] = kk.T.astype(BF16)
    v_ref[...] = akv[:, KV_WIDTH:].astype(BF16)

    base = ATTN_WIDTH + 2 * KV_WIDTH
    ag = proj(base, base + ATTN_WIDTH)
    ag_ref[...] = (ag * _sigmoid(ag)).astype(BF16)
    base += ATTN_WIDTH
    hq = proj(base, base + HG_WIDTH)
    hq_ref[...] = (hq * _sigmoid(hq) * (HG_EXPAND ** -0.5)).astype(BF16)
    base += HG_WIDTH
    hi_ref[...] = proj(base, base + HG_WIDTH).astype(BF16)
    base += HG_WIDTH
    lbl = lbl_ref[...]
    for d in range(2):
        l0 = lbl[2 * d:2 * d + 1, :]
        l1 = lbl[2 * d + 1:2 * d + 2, :]
        mx = jnp.maximum(l0, l1)
        e0 = jnp.exp(l0 - mx)
        e1 = jnp.exp(l1 - mx)
        lb = e0 / (e0 + e1)
        z = proj(base, base + HG_WIDTH)
        lf_ref[:, d * HG_WIDTH:(d + 1) * HG_WIDTH] = jnp.log(lb + (1.0 - lb) * _sigmoid(z))
        base += HG_WIDTH
    hg = proj(base, base + HG_WIDTH)
    hg_ref[...] = (hg * _sigmoid(hg)).astype(BF16)


def _proj_call(x, w_in_b, cos_t, sin_t, qw, kw, lbl, seg):
    B, S, D = x.shape
    tm = PROJ_TM
    nt = S // tm
    row = lambda b, i: (b, i, 0)
    const2 = lambda b, i: (0, 0)
    out_shape = (
        jax.ShapeDtypeStruct((B, S, ATTN_WIDTH), BF16),
        jax.ShapeDtypeStruct((B, KV_WIDTH, S), BF16),
        jax.ShapeDtypeStruct((B, S, KV_WIDTH), BF16),
        jax.ShapeDtypeStruct((B, S, ATTN_WIDTH), BF16),
        jax.ShapeDtypeStruct((B, S, HG_WIDTH), BF16),
        jax.ShapeDtypeStruct((B, S, HG_WIDTH), BF16),
        jax.ShapeDtypeStruct((B, S, 2 * HG_WIDTH), F32),
        jax.ShapeDtypeStruct((B, S, HG_WIDTH), BF16),
    )
    in_specs = [
        pl.BlockSpec((None, tm, D), row),
        pl.BlockSpec((D, IN_WIDTH), const2),
        pl.BlockSpec((tm, LANES), lambda b, i: (i, 0)),
        pl.BlockSpec((tm, LANES), lambda b, i: (i, 0)),
        pl.BlockSpec((1, ATTN_WIDTH), const2),
        pl.BlockSpec((1, KV_WIDTH), const2),
        pl.BlockSpec((4, HG_WIDTH), const2),
        pl.BlockSpec((ATTN_WIDTH, ATTN_WIDTH), const2),
    ]
    out_specs = (
        pl.BlockSpec((None, tm, ATTN_WIDTH), row),
        pl.BlockSpec((None, KV_WIDTH, tm), lambda b, i: (b, 0, i)),
        pl.BlockSpec((None, tm, KV_WIDTH), row),
        pl.BlockSpec((None, tm, ATTN_WIDTH), row),
        pl.BlockSpec((None, tm, HG_WIDTH), row),
        pl.BlockSpec((None, tm, HG_WIDTH), row),
        pl.BlockSpec((None, tm, 2 * HG_WIDTH), row),
        pl.BlockSpec((None, tm, HG_WIDTH), row),
    )
    return pl.pallas_call(
        _proj_kernel,
        grid=(B, nt),
        in_specs=in_specs,
        out_specs=out_specs,
        out_shape=out_shape,
        compiler_params=pltpu.CompilerParams(
            dimension_semantics=("arbitrary", "arbitrary"),
            vmem_limit_bytes=VMEM_LIMIT),
        name="in_proj",
    )(x, w_in_b, cos_t, sin_t, qw, kw, lbl, seg)


def _attn_kernel(q_ref, kt_ref, v_ref, o_ref, m_ref, l_ref, acc_ref):
    tq = q_ref.shape[0]
    S = v_ref.shape[0]
    G = N_Q_HEADS // N_KV_HEADS
    q = q_ref[...]
    outs = []
    for g in range(N_KV_HEADS):
        qs = jnp.concatenate(
            [q[:, (G * g + h) * HEAD_DIM:(G * g + h + 1) * HEAD_DIM] for h in range(G)],
            axis=0)
        m_ref[...] = jnp.full(m_ref.shape, -jnp.inf, F32)
        l_ref[...] = jnp.zeros(l_ref.shape, F32)
        acc_ref[...] = jnp.zeros(acc_ref.shape, F32)

        def body(j, _):
            k0 = pl.multiple_of(j * ATTN_TK, ATTN_TK)
            kt = kt_ref[g * HEAD_DIM:(g + 1) * HEAD_DIM, pl.ds(k0, ATTN_TK)]
            s = jnp.dot(qs, kt, preferred_element_type=F32)
            m_old = m_ref[...]
            m_new = jnp.maximum(m_old, jnp.max(s, axis=1, keepdims=True))
            alpha = jnp.exp(m_old - m_new)
            p = jnp.exp(s - m_new)
            l_ref[...] = alpha * l_ref[...] + jnp.sum(p, axis=1, keepdims=True)
            m_ref[...] = m_new
            vv = v_ref[pl.ds(k0, ATTN_TK), :]
            acc_ref[...] = alpha * acc_ref[...] + jnp.dot(
                p.astype(BF16), vv, preferred_element_type=F32)
            return 0

        lax.fori_loop(0, S // ATTN_TK, body, 0)
        o = acc_ref[:, g * HEAD_DIM:(g + 1) * HEAD_DIM] / l_ref[...]
        outs += [o[h * tq:(h + 1) * tq] for h in range(G)]
    o_ref[...] = jnp.concatenate(outs, axis=1).astype(o_ref.dtype)


def _attn_call(q, kt, v):
    B, S, _ = q.shape
    tq = ATTN_TQ
    G = N_Q_HEADS // N_KV_HEADS
    return pl.pallas_call(
        _attn_kernel,
        grid=(B, S // tq),
        in_specs=[
            pl.BlockSpec((None, tq, ATTN_WIDTH), lambda b, i: (b, i, 0)),
            pl.BlockSpec((None, KV_WIDTH, S), lambda b, i: (b, 0, 0)),
            pl.BlockSpec((None, S, KV_WIDTH), lambda b, i: (b, 0, 0)),
        ],
        out_specs=pl.BlockSpec((None, tq, ATTN_WIDTH), lambda b, i: (b, i, 0)),
        out_shape=jax.ShapeDtypeStruct((B, S, ATTN_WIDTH), BF16),
        scratch_shapes=[
            pltpu.VMEM((G * tq, 1), F32),
            pltpu.VMEM((G * tq, 1), F32),
            pltpu.VMEM((G * tq, KV_WIDTH), F32),
        ],
        compiler_params=pltpu.CompilerParams(
            dimension_semantics=("arbitrary", "arbitrary"),
            vmem_limit_bytes=VMEM_LIMIT),
        name="attention",
    )(q, kt, v)


def _split_dot(tri, x):
    hi = x.astype(BF16)
    lo = (x - hi.astype(F32)).astype(BF16)
    return (jnp.dot(tri, hi, preferred_element_type=F32)
            + jnp.dot(tri, lo, preferred_element_type=F32))


def _hgrn_kernel(q_ref, i_ref, lff_ref, lfb_ref, g_ref, nw_ref, o_ref, of_ref, st_ref):
    S = q_ref.shape[0]
    C = HG_CHUNK
    n_chunks = S // C
    r = lax.broadcasted_iota(jnp.int32, (C, C), 0)
    c = lax.broadcasted_iota(jnp.int32, (C, C), 1)
    lower = r >= c
    tri_f = jnp.where(lower, 1.0, 0.0).astype(BF16)
    tri_b = jnp.where(r <= c, 1.0, 0.0).astype(BF16)

    def chunk(n, lf_ref, backward):
        rows = pl.ds(pl.multiple_of(n * C, C), C)
        lf = lf_ref[rows, :]
        q = q_ref[rows, :].astype(F32)
        iv = i_ref[rows, :]
        k = 1.0 - jnp.exp(lf)
        if backward:
            b = _split_dot(tri_b, lf)
            b_last = b[0:1, :]
            b_mid = b[C - 1 - C // 2:C - C // 2, :]
            mask = r <= c
        else:
            b = _split_dot(tri_f, lf)
            b_last = b[C - 1:C, :]
            b_mid = b[C // 2:C // 2 + 1, :]
            mask = lower
        qm = (q * jnp.exp(b - b_mid)).astype(BF16)
        km = (k * jnp.exp(b_mid - b)).astype(BF16)
        a = lax.dot_general(qm, km, (((1,), (1,)), ((), ())), preferred_element_type=F32)
        a = jnp.where(mask, a, 0.0).astype(BF16)
        o_intra = jnp.dot(a, iv, preferred_element_type=F32)
        st = st_ref[...]
        qb = (q * jnp.exp(b)).astype(BF16)
        o_inter = lax.dot_general(qb, st.astype(BF16), (((1,), (1,)), ((), ())),
                                  preferred_element_type=F32)
        kd = (k * jnp.exp(b_last - b)).astype(BF16)
        ivt = iv.astype(F32).T.astype(BF16)
        u_t = jnp.dot(ivt, kd, preferred_element_type=F32)
        st_ref[...] = jnp.exp(b_last) * st + u_t
        return rows, o_intra + o_inter

    st_ref[...] = jnp.zeros(st_ref.shape, F32)

    def fwd_body(t, _):
        for u in range(HG_UNROLL):
            rows, o = chunk(t * HG_UNROLL + u, lff_ref, False)
            of_ref[rows, :] = o
        return 0

    lax.fori_loop(0, n_chunks // HG_UNROLL, fwd_body, 0)

    st_ref[...] = jnp.zeros(st_ref.shape, F32)
    nw = nw_ref[...]

    def bwd_body(t, _):
        for u in range(HG_UNROLL):
            n = n_chunks - 1 - (t * HG_UNROLL + u)
            rows, o = chunk(n, lfb_ref, True)
            o = o + of_ref[rows, :]
            ms = jnp.mean(o * o, axis=-1, keepdims=True)
            y = o * lax.rsqrt(ms + RMS_EPS) * nw
            o_ref[rows, :] = (y * g_ref[rows, :].astype(F32)).astype(o_ref.dtype)
        return 0

    lax.fori_loop(0, n_chunks // HG_UNROLL, bwd_body, 0)


def _hgrn_call(hq, hi, lf, hg, nw):
    B, S, _ = hq.shape
    blk = lambda b, h: (b, 0, h)
    return pl.pallas_call(
        _hgrn_kernel,
        grid=(B, HG_HEADS),
        in_specs=[
            pl.BlockSpec((None, S, HG_EXPAND), blk),
            pl.BlockSpec((None, S, HG_EXPAND), blk),
            pl.BlockSpec((None, S, HG_EXPAND), blk),
            pl.BlockSpec((None, S, HG_EXPAND), lambda b, h: (b, 0, HG_HEADS + h)),
            pl.BlockSpec((None, S, HG_EXPAND), blk),
            pl.BlockSpec((1, HG_EXPAND), lambda b, h: (0, 0)),
        ],
        out_specs=pl.BlockSpec((None, S, HG_EXPAND), blk),
        out_shape=jax.ShapeDtypeStruct((B, S, HG_WIDTH), BF16),
        scratch_shapes=[
            pltpu.VMEM((S, HG_EXPAND), F32),
            pltpu.VMEM((HG_EXPAND, HG_EXPAND), F32),
        ],
        compiler_params=pltpu.CompilerParams(
            dimension_semantics=("arbitrary", "arbitrary"),
            vmem_limit_bytes=VMEM_LIMIT),
        name="hgrn2",
    )(hq, hi, lf, lf, hg, nw)


def _out_kernel(x_ref, attn_ref, ag_ref, hgb_ref, anw_ref, w_ref, lnw_ref, lnb_ref, o_ref):
    a = attn_ref[...].astype(F32)
    ms = jnp.mean(a * a, axis=-1, keepdims=True)
    ab = a * lax.rsqrt(ms + RMS_EPS) * anw_ref[...] * ag_ref[...].astype(F32)
    y = (jnp.dot(ab.astype(BF16), w_ref[:ATTN_WIDTH, :], preferred_element_type=F32)
         + jnp.dot(hgb_ref[...], w_ref[ATTN_WIDTH:, :], preferred_element_type=F32))
    z = DEEPNORM_ALPHA * x_ref[...] + y
    mu = jnp.mean(z, axis=-1, keepdims=True)
    zc = z - mu
    var = jnp.mean(zc * zc, axis=-1, keepdims=True)
    o_ref[...] = zc * lax.rsqrt(var + LN_EPS) * lnw_ref[...] + lnb_ref[...]


def _out_call(x, attn, ag, hgb, anw, w_out_b, lnw, lnb):
    B, S, D = x.shape
    tm = OUT_TM
    row = lambda b, i: (b, i, 0)
    const2 = lambda b, i: (0, 0)
    return pl.pallas_call(
        _out_kernel,
        grid=(B, S // tm),
        in_specs=[
            pl.BlockSpec((None, tm, D), row),
            pl.BlockSpec((None, tm, ATTN_WIDTH), row),
            pl.BlockSpec((None, tm, ATTN_WIDTH), row),
            pl.BlockSpec((None, tm, HG_WIDTH), row),
            pl.BlockSpec((1, ATTN_WIDTH), const2),
            pl.BlockSpec((ATTN_WIDTH + HG_WIDTH, D), const2),
            pl.BlockSpec((1, D), const2),
            pl.BlockSpec((1, D), const2),
        ],
        out_specs=pl.BlockSpec((None, tm, D), row),
        out_shape=jax.ShapeDtypeStruct((B, S, D), x.dtype),
        compiler_params=pltpu.CompilerParams(
            dimension_semantics=("arbitrary", "arbitrary"),
            vmem_limit_bytes=VMEM_LIMIT),
        name="out_proj_ln",
    )(x, attn, ag, hgb, anw, w_out_b, lnw, lnb)


def _rope_tables(seq_len):
    t = jnp.arange(seq_len, dtype=jnp.int32)
    row = (t // GRID_W).astype(F32)
    col = (t % GRID_W).astype(F32)
    inv = ROPE_THETA ** (-jnp.arange(0, AXIS_DIM, 2, dtype=F32) / AXIS_DIM)
    lane = jnp.arange(LANES, dtype=jnp.int32)
    d = lane % HEAD_DIM
    use_col = (d // AXIS_DIM) == 1
    dd = d % AXIS_DIM
    first = dd < (AXIS_DIM // 2)
    freq = inv[dd % (AXIS_DIM // 2)]
    pos = jnp.where(use_col[None, :], col[:, None], row[:, None])
    ang = pos * freq[None, :]
    cos_t = jnp.cos(ang)
    sin_t = jnp.where(first[None, :], -jnp.sin(ang), jnp.sin(ang))
    return cos_t, sin_t


def kernel(x, w_in, q_norm_w, k_norm_w, attn_norm_w, hg_lb_logits, hg_norm_w, w_out, ln_w, ln_b):
    B, S, D = x.shape
    assert w_in.shape == (DEPTH, D, IN_WIDTH) and DEPTH == 1
    cos_t, sin_t = _rope_tables(S)
    qw = jnp.tile(q_norm_w[0].astype(F32), N_Q_HEADS).reshape(1, ATTN_WIDTH)
    kw = jnp.tile(k_norm_w[0].astype(F32), N_KV_HEADS).reshape(1, KV_WIDTH)
    lbl = hg_lb_logits.astype(F32).reshape(2 * (DEPTH + 1), HG_WIDTH)
    hid = jnp.arange(ATTN_WIDTH, dtype=jnp.int32) // HEAD_DIM
    seg = jnp.where(hid[:, None] == hid[None, :], 1.0 / HEAD_DIM, 0.0).astype(BF16)

    q, kt, v, ag, hq, hi, lf, hg = _proj_call(
        x, w_in[0].astype(BF16), cos_t, sin_t, qw, kw, lbl, seg)
    attn = _attn_call(q, kt, v)
    hgb = _hgrn_call(hq, hi, lf, hg, hg_norm_w[0].astype(F32).reshape(1, HG_EXPAND))
    return _out_call(
        x, attn, ag, hgb,
        attn_norm_w[0].astype(F32).reshape(1, ATTN_WIDTH),
        w_out[0].astype(BF16),
        ln_w[0].astype(F32).reshape(1, D),
        ln_b[0].astype(F32).reshape(1, D))
```

```python
import functools

import jax
import jax.numpy as jnp
from jax import lax
from jax.experimental import pallas as pl
from jax.experimental.pallas import tpu as pltpu

F32 = jnp.float32
BF16 = jnp.bfloat16

GRID_W = 64
ATTN_WIDTH = 512
HEAD_DIM = 64
N_Q_HEADS = 8
N_KV_HEADS = 2
KV_WIDTH = 128
AXIS_DIM = 32
ROPE_THETA = 10000.0
HG_WIDTH = 512
HG_EXPAND = 128
HG_HEADS = 4
HG_CHUNK = 64
RMS_EPS = 1e-6
LN_EPS = 1e-5
DEPTH = 1
DEEPNORM_ALPHA = (2 * DEPTH) ** 0.25
IN_WIDTH = 3840
Q_SCALE = HEAD_DIM ** -0.5 * 1.4426950408889634

LANES = 128
VMEM_LIMIT = 56 * 1024 * 1024

PROJ_TM = 512
ATTN_TQ = 128
ATTN_TK = 512
HG_UNROLL = 8
OUT_TM = 512


def _sigmoid(x):
    return 1.0 / (1.0 + jnp.exp(-x))


def _proj_kernel(x_ref, w_ref, cos_ref, sin_ref, qw_ref, kw_ref, lbl_ref, seg_ref,
                 qt_ref, k_ref, vt_ref, ag_ref, hq_ref, hi_ref, lf_ref, hg_ref):
    xb = x_ref[...].astype(BF16)
    tm = xb.shape[0]

    def proj(a, b):
        return jnp.dot(xb, w_ref[:, a:b], preferred_element_type=F32)

    cosv = cos_ref[...]
    sinv = sin_ref[...]
    lane = lax.broadcasted_iota(jnp.int32, (tm, LANES), 1)
    first_half = (lane % AXIS_DIM) < (AXIS_DIM // 2)

    def norm_rope(a, w_row, seg, scale):
        ms = jnp.dot((a * a).astype(BF16), seg, preferred_element_type=F32)
        y = a * lax.rsqrt(ms + RMS_EPS) * w_row
        outs = []
        for c in range(a.shape[1] // LANES):
            yc = y[:, c * LANES:(c + 1) * LANES]
            partner = jnp.where(first_half,
                                pltpu.roll(yc, LANES - AXIS_DIM // 2, 1),
                                pltpu.roll(yc, AXIS_DIM // 2, 1))
            outs.append((yc * cosv + partner * sinv) * scale)
        return outs

    aq = proj(0, ATTN_WIDTH)
    qs = norm_rope(aq, qw_ref[...], seg_ref[...], Q_SCALE)
    for c, qc in enumerate(qs):
        qt_ref[c * LANES:(c + 1) * LANES, :] = qc.T.astype(BF16)
    akv = proj(ATTN_WIDTH, ATTN_WIDTH + 2 * KV_WIDTH)
    kk = norm_rope(akv[:, :KV_WIDTH], kw_ref[...], seg_ref[:KV_WIDTH, :KV_WIDTH], 1.0)[0]
    k_ref[...] = kk.astype(BF16)
    vt_ref[...] = akv[:, KV_WIDTH:].T.astype(BF16)

    base = ATTN_WIDTH + 2 * KV_WIDTH
    ag = proj(base, base + ATTN_WIDTH)
    ag_ref[...] = (ag * _sigmoid(ag)).astype(BF16)
    base += ATTN_WIDTH
    hq = proj(base, base + HG_WIDTH)
    hq_ref[...] = (hq * _sigmoid(hq) * (HG_EXPAND ** -0.5)).astype(BF16)
    base += HG_WIDTH
    hi_ref[...] = proj(base, base + HG_WIDTH).astype(BF16)
    base += HG_WIDTH
    lbl = lbl_ref[...]
    for d in range(2):
        l0 = lbl[2 * d:2 * d + 1, :]
        l1 = lbl[2 * d + 1:2 * d + 2, :]
        mx = jnp.maximum(l0, l1)
        e0 = jnp.exp(l0 - mx)
        e1 = jnp.exp(l1 - mx)
        lb = e0 / (e0 + e1)
        z = proj(base, base + HG_WIDTH)
        lf_ref[:, d * HG_WIDTH:(d + 1) * HG_WIDTH] = jnp.log(lb + (1.0 - lb) * _sigmoid(z))
        base += HG_WIDTH
    hg = proj(base, base + HG_WIDTH)
    hg_ref[...] = (hg * _sigmoid(hg)).astype(BF16)


def _proj_call(x, w_in_b, cos_t, sin_t, qw, kw, lbl, seg):
    B, S, D = x.shape
    tm = PROJ_TM
    nt = S // tm
    row = lambda b, i: (b, i, 0)
    const2 = lambda b, i: (0, 0)
    out_shape = (
        jax.ShapeDtypeStruct((B, ATTN_WIDTH, S), BF16),
        jax.ShapeDtypeStruct((B, S, KV_WIDTH), BF16),
        jax.ShapeDtypeStruct((B, KV_WIDTH, S), BF16),
        jax.ShapeDtypeStruct((B, S, ATTN_WIDTH), BF16),
        jax.ShapeDtypeStruct((B, S, HG_WIDTH), BF16),
        jax.ShapeDtypeStruct((B, S, HG_WIDTH), BF16),
        jax.ShapeDtypeStruct((B, S, 2 * HG_WIDTH), F32),
        jax.ShapeDtypeStruct((B, S, HG_WIDTH), BF16),
    )
    in_specs = [
        pl.BlockSpec((None, tm, D), row),
        pl.BlockSpec((D, IN_WIDTH), const2),
        pl.BlockSpec((tm, LANES), lambda b, i: (i, 0)),
        pl.BlockSpec((tm, LANES), lambda b, i: (i, 0)),
        pl.BlockSpec((1, ATTN_WIDTH), const2),
        pl.BlockSpec((1, KV_WIDTH), const2),
        pl.BlockSpec((4, HG_WIDTH), const2),
        pl.BlockSpec((ATTN_WIDTH, ATTN_WIDTH), const2),
    ]
    out_specs = (
        pl.BlockSpec((None, ATTN_WIDTH, tm), lambda b, i: (b, 0, i)),
        pl.BlockSpec((None, tm, KV_WIDTH), row),
        pl.BlockSpec((None, KV_WIDTH, tm), lambda b, i: (b, 0, i)),
        pl.BlockSpec((None, tm, ATTN_WIDTH), row),
        pl.BlockSpec((None, tm, HG_WIDTH), row),
        pl.BlockSpec((None, tm, HG_WIDTH), row),
        pl.BlockSpec((None, tm, 2 * HG_WIDTH), row),
        pl.BlockSpec((None, tm, HG_WIDTH), row),
    )
    return pl.pallas_call(
        _proj_kernel,
        grid=(B, nt),
        in_specs=in_specs,
        out_specs=out_specs,
        out_shape=out_shape,
        compiler_params=pltpu.CompilerParams(
            dimension_semantics=("arbitrary", "arbitrary"),
            vmem_limit_bytes=VMEM_LIMIT),
        name="in_proj",
    )(x, w_in_b, cos_t, sin_t, qw, kw, lbl, seg)


def _attn_kernel(qt_ref, k_ref, vt_ref, o_ref, s0_ref, s1_ref, vx_ref, acc_ref):
    S = k_ref.shape[0]
    G = N_Q_HEADS // N_KV_HEADS
    tq, tk = ATTN_TQ, ATTN_TK
    cols = G * tq
    n_q = S // tq
    n_c = S // tk
    vrows = vx_ref.shape[1]

    for g in range(N_KV_HEADS):
        vx_ref[g, 0:HEAD_DIM, :] = vt_ref[g * HEAD_DIM:(g + 1) * HEAD_DIM, :]
        vx_ref[g, HEAD_DIM:vrows, :] = jnp.ones((vrows - HEAD_DIM, S), BF16)

    def q_weights(i, g):
        c0 = pl.multiple_of(i * tq, tq)
        qs = jnp.concatenate(
            [qt_ref[(G * g + h) * HEAD_DIM:(G * g + h + 1) * HEAD_DIM, pl.ds(c0, tq)]
             for h in range(G)], axis=1)
        z = jnp.zeros_like(qs)
        return jnp.concatenate([qs, z] if g == 0 else [z, qs], axis=0)

    def scores(c, w, s_ref, mx):
        rows = pl.ds(pl.multiple_of(c * tk, tk), tk)
        s = jnp.dot(k_ref[rows, :], w, preferred_element_type=F32)
        s_ref[rows, :] = s
        return jnp.maximum(mx, jnp.max(s.reshape(tk // 8, 8, cols), axis=0))

    def probs_pv(c, s_ref, m, g):
        rows = pl.ds(pl.multiple_of(c * tk, tk), tk)
        p = jnp.exp2(s_ref[rows, :] - m).astype(BF16)
        acc_ref[...] += jnp.dot(vx_ref[g, :, rows], p, preferred_element_type=F32)

    def finish(i, g):
        acc = acc_ref[...]
        o_t = acc[0:HEAD_DIM, :] / acc[HEAD_DIM:HEAD_DIM + 1, :]
        o = o_t.T
        blk = jnp.concatenate([o[h * tq:(h + 1) * tq, :] for h in range(G)], axis=1)
        r0 = pl.multiple_of(i * tq, tq)
        o_ref[pl.ds(r0, tq), g * G * HEAD_DIM:(g + 1) * G * HEAD_DIM] = blk.astype(o_ref.dtype)

    mx0 = jnp.full((8, cols), -jnp.inf, F32)
    col_max = lambda mx: jnp.max(mx, axis=0, keepdims=True)

    w00 = q_weights(0, 0)
    m_first = col_max(lax.fori_loop(
        0, n_c, lambda c, mx: scores(c, w00, s0_ref, mx), mx0))

    def outer(i, m_g0):
        w1 = q_weights(i, 1)
        acc_ref[...] = jnp.zeros(acc_ref.shape, F32)

        def body_b(c, mx):
            mx = scores(c, w1, s1_ref, mx)
            probs_pv(c, s0_ref, m_g0, 0)
            return mx

        m_g1 = col_max(lax.fori_loop(0, n_c, body_b, mx0))
        finish(i, 0)
        w0 = q_weights(jnp.minimum(i + 1, n_q - 1), 0)
        acc_ref[...] = jnp.zeros(acc_ref.shape, F32)

        def body_a(c, mx):
            mx = scores(c, w0, s0_ref, mx)
            probs_pv(c, s1_ref, m_g1, 1)
            return mx

        m_next = col_max(lax.fori_loop(0, n_c, body_a, mx0))
        finish(i, 1)
        return m_next

    lax.fori_loop(0, n_q, outer, m_first)


def _attn_call(qt, k, vt):
    B, _, S = qt.shape
    G = N_Q_HEADS // N_KV_HEADS
    cols = G * ATTN_TQ
    whole = lambda b: (b, 0, 0)
    return pl.pallas_call(
        _attn_kernel,
        grid=(B,),
        in_specs=[
            pl.BlockSpec((None, ATTN_WIDTH, S), whole),
            pl.BlockSpec((None, S, KV_WIDTH), whole),
            pl.BlockSpec((None, KV_WIDTH, S), whole),
        ],
        out_specs=pl.BlockSpec((None, S, ATTN_WIDTH), whole),
        out_shape=jax.ShapeDtypeStruct((B, S, ATTN_WIDTH), BF16),
        scratch_shapes=[
            pltpu.VMEM((S, cols), F32),
            pltpu.VMEM((S, cols), F32),
            pltpu.VMEM((N_KV_HEADS, HEAD_DIM + 16, S), BF16),
            pltpu.VMEM((HEAD_DIM + 16, cols), F32),
        ],
        compiler_params=pltpu.CompilerParams(
            dimension_semantics=("arbitrary",),
            vmem_limit_bytes=VMEM_LIMIT),
        name="attention",
    )(qt, k, vt)


def _split_dot(tri, x):
    hi = x.astype(BF16)
    lo = (x - hi.astype(F32)).astype(BF16)
    return (jnp.dot(tri, hi, preferred_element_type=F32)
            + jnp.dot(tri, lo, preferred_element_type=F32))


def _hgrn_kernel(q_ref, i_ref, lff_ref, lfb_ref, g_ref, nw_ref, o_ref, of_ref, st_ref):
    S = q_ref.shape[0]
    C = HG_CHUNK
    n_chunks = S // C
    r = lax.broadcasted_iota(jnp.int32, (C, C), 0)
    c = lax.broadcasted_iota(jnp.int32, (C, C), 1)
    lower = r >= c
    tri_f = jnp.where(lower, 1.0, 0.0).astype(BF16)
    tri_b = jnp.where(r <= c, 1.0, 0.0).astype(BF16)

    def chunk(n, lf_ref, backward):
        rows = pl.ds(pl.multiple_of(n * C, C), C)
        lf = lf_ref[rows, :]
        q = q_ref[rows, :].astype(F32)
        iv = i_ref[rows, :]
        k = 1.0 - jnp.exp(lf)
        if backward:
            b = _split_dot(tri_b, lf)
            b_last = b[0:1, :]
            b_mid = b[C - 1 - C // 2:C - C // 2, :]
            mask = r <= c
        else:
            b = _split_dot(tri_f, lf)
            b_last = b[C - 1:C, :]
            b_mid = b[C // 2:C // 2 + 1, :]
            mask = lower
        qm = (q * jnp.exp(b - b_mid)).astype(BF16)
        km = (k * jnp.exp(b_mid - b)).astype(BF16)
        a = lax.dot_general(qm, km, (((1,), (1,)), ((), ())), preferred_element_type=F32)
        a = jnp.where(mask, a, 0.0).astype(BF16)
        o_intra = jnp.dot(a, iv, preferred_element_type=F32)
        st = st_ref[...]
        qb = (q * jnp.exp(b)).astype(BF16)
        o_inter = lax.dot_general(qb, st.astype(BF16), (((1,), (1,)), ((), ())),
                                  preferred_element_type=F32)
        kd = (k * jnp.exp(b_last - b)).astype(BF16)
        ivt = iv.astype(F32).T.astype(BF16)
        u_t = jnp.dot(ivt, kd, preferred_element_type=F32)
        st_ref[...] = jnp.exp(b_last) * st + u_t
        return rows, o_intra + o_inter

    st_ref[...] = jnp.zeros(st_ref.shape, F32)

    def fwd_body(t, _):
        for u in range(HG_UNROLL):
            rows, o = chunk(t * HG_UNROLL + u, lff_ref, False)
            of_ref[rows, :] = o
        return 0

    lax.fori_loop(0, n_chunks // HG_UNROLL, fwd_body, 0)

    st_ref[...] = jnp.zeros(st_ref.shape, F32)
    nw = nw_ref[...]

    def bwd_body(t, _):
        for u in range(HG_UNROLL):
            n = n_chunks - 1 - (t * HG_UNROLL + u)
            rows, o = chunk(n, lfb_ref, True)
            o = o + of_ref[rows, :]
            ms = jnp.mean(o * o, axis=-1, keepdims=True)
            y = o * lax.rsqrt(ms + RMS_EPS) * nw
            o_ref[rows, :] = (y * g_ref[rows, :].astype(F32)).astype(o_ref.dtype)
        return 0

    lax.fori_loop(0, n_chunks // HG_UNROLL, bwd_body, 0)


def _hgrn_call(hq, hi, lf, hg, nw):
    B, S, _ = hq.shape
    blk = lambda b, h: (b, 0, h)
    return pl.pallas_call(
        _hgrn_kernel,
        grid=(B, HG_HEADS),
        in_specs=[
            pl.BlockSpec((None, S, HG_EXPAND), blk),
            pl.BlockSpec((None, S, HG_EXPAND), blk),
            pl.BlockSpec((None, S, HG_EXPAND), blk),
            pl.BlockSpec((None, S, HG_EXPAND), lambda b, h: (b, 0, HG_HEADS + h)),
            pl.BlockSpec((None, S, HG_EXPAND), blk),
            pl.BlockSpec((1, HG_EXPAND), lambda b, h: (0, 0)),
        ],
        out_specs=pl.BlockSpec((None, S, HG_EXPAND), blk),
        out_shape=jax.ShapeDtypeStruct((B, S, HG_WIDTH), BF16),
        scratch_shapes=[
            pltpu.VMEM((S, HG_EXPAND), F32),
            pltpu.VMEM((HG_EXPAND, HG_EXPAND), F32),
        ],
        compiler_params=pltpu.CompilerParams(
            dimension_semantics=("arbitrary", "arbitrary"),
            vmem_limit_bytes=VMEM_LIMIT),
        name="hgrn2",
    )(hq, hi, lf, lf, hg, nw)


def _out_kernel(x_ref, attn_ref, ag_ref, hgb_ref, anw_ref, w_ref, lnw_ref, lnb_ref, o_ref):
    a = attn_ref[...].astype(F32)
    ms = jnp.mean(a * a, axis=-1, keepdims=True)
    ab = a * lax.rsqrt(ms + RMS_EPS) * anw_ref[...] * ag_ref[...].astype(F32)
    y = (jnp.dot(ab.astype(BF16), w_ref[:ATTN_WIDTH, :], preferred_element_type=F32)
         + jnp.dot(hgb_ref[...], w_ref[ATTN_WIDTH:, :], preferred_element_type=F32))
    z = DEEPNORM_ALPHA * x_ref[...] + y
    mu = jnp.mean(z, axis=-1, keepdims=True)
    zc = z - mu
    var = jnp.mean(zc * zc, axis=-1, keepdims=True)
    o_ref[...] = zc * lax.rsqrt(var + LN_EPS) * lnw_ref[...] + lnb_ref[...]


def _out_call(x, attn, ag, hgb, anw, w_out_b, lnw, lnb):
    B, S, D = x.shape
    tm = OUT_TM
    row = lambda b, i: (b, i, 0)
    const2 = lambda b, i: (0, 0)
    return pl.pallas_call(
        _out_kernel,
        grid=(B, S // tm),
        in_specs=[
            pl.BlockSpec((None, tm, D), row),
            pl.BlockSpec((None, tm, ATTN_WIDTH), row),
            pl.BlockSpec((None, tm, ATTN_WIDTH), row),
            pl.BlockSpec((None, tm, HG_WIDTH), row),
            pl.BlockSpec((1, ATTN_WIDTH), const2),
            pl.BlockSpec((ATTN_WIDTH + HG_WIDTH, D), const2),
            pl.BlockSpec((1, D), const2),
            pl.BlockSpec((1, D), const2),
        ],
        out_specs=pl.BlockSpec((None, tm, D), row),
        out_shape=jax.ShapeDtypeStruct((B, S, D), x.dtype),
        compiler_params=pltpu.CompilerParams(
            dimension_semantics=("arbitrary", "arbitrary"),
            vmem_limit_bytes=VMEM_LIMIT),
        name="out_proj_ln",
    )(x, attn, ag, hgb, anw, w_out_b, lnw, lnb)


def _rope_tables(seq_len):
    t = jnp.arange(seq_len, dtype=jnp.int32)
    row = (t // GRID_W).astype(F32)
    col = (t % GRID_W).astype(F32)
    inv = ROPE_THETA ** (-jnp.arange(0, AXIS_DIM, 2, dtype=F32) / AXIS_DIM)
    lane = jnp.arange(LANES, dtype=jnp.int32)
    d = lane % HEAD_DIM
    use_col = (d // AXIS_DIM) == 1
    dd = d % AXIS_DIM
    first = dd < (AXIS_DIM // 2)
    freq = inv[dd % (AXIS_DIM // 2)]
    pos = jnp.where(use_col[None, :], col[:, None], row[:, None])
    ang = pos * freq[None, :]
    cos_t = jnp.cos(ang)
    sin_t = jnp.where(first[None, :], -jnp.sin(ang), jnp.sin(ang))
    return cos_t, sin_t


def kernel(x, w_in, q_norm_w, k_norm_w, attn_norm_w, hg_lb_logits, hg_norm_w, w_out, ln_w, ln_b):
    B, S, D = x.shape
    assert w_in.shape == (DEPTH, D, IN_WIDTH) and DEPTH == 1
    cos_t, sin_t = _rope_tables(S)
    qw = jnp.tile(q_norm_w[0].astype(F32), N_Q_HEADS).reshape(1, ATTN_WIDTH)
    kw = jnp.tile(k_norm_w[0].astype(F32), N_KV_HEADS).reshape(1, KV_WIDTH)
    lbl = hg_lb_logits.astype(F32).reshape(2 * (DEPTH + 1), HG_WIDTH)
    hid = jnp.arange(ATTN_WIDTH, dtype=jnp.int32) // HEAD_DIM
    seg = jnp.where(hid[:, None] == hid[None, :], 1.0 / HEAD_DIM, 0.0).astype(BF16)

    qt, k, vt, ag, hq, hi, lf, hg = _proj_call(
        x, w_in[0].astype(BF16), cos_t, sin_t, qw, kw, lbl, seg)
    attn = _attn_call(qt, k, vt)
    hgb = _hgrn_call(hq, hi, lf, hg, hg_norm_w[0].astype(F32).reshape(1, HG_EXPAND))
    return _out_call(
        x, attn, ag, hgb,
        attn_norm_w[0].astype(F32).reshape(1, ATTN_WIDTH),
        w_out[0].astype(BF16),
        ln_w[0].astype(F32).reshape(1, D),
        ln_b[0].astype(F32).reshape(1, D))
```

```python
import functools

import jax
import jax.numpy as jnp
from jax import lax
from jax.experimental import pallas as pl
from jax.experimental.pallas import tpu as pltpu

F32 = jnp.float32
BF16 = jnp.bfloat16

GRID_W = 64
ATTN_WIDTH = 512
HEAD_DIM = 64
N_Q_HEADS = 8
N_KV_HEADS = 2
KV_WIDTH = 128
AXIS_DIM = 32
ROPE_THETA = 10000.0
HG_WIDTH = 512
HG_EXPAND = 128
HG_HEADS = 4
HG_CHUNK = 64
RMS_EPS = 1e-6
LN_EPS = 1e-5
DEPTH = 1
DEEPNORM_ALPHA = (2 * DEPTH) ** 0.25
IN_WIDTH = 3840
LOG2E = 1.4426950408889634
Q_SCALE = HEAD_DIM ** -0.5 * LOG2E

LANES = 128
VMEM_LIMIT = 56 * 1024 * 1024

PROJ_TM = 512
ATTN_TQ = 128
ATTN_TK = 512
HG_BLOCK = 8
HG_BLOCK_OUT = 16
OUT_TM = 512


def _sigmoid(x):
    return 1.0 / (1.0 + jnp.exp(-x))


def _proj_kernel(x_ref, w_ref, cos_ref, sin_ref, qw_ref, kw_ref, lbl_ref, seg_ref,
                 qt_ref, k_ref, vt_ref, ag_ref, hq_ref, hi_ref, lf_ref, hg_ref):
    xb = x_ref[...].astype(BF16)
    tm = xb.shape[0]

    def proj(a, b):
        return jnp.dot(xb, w_ref[:, a:b], preferred_element_type=F32)

    cosv = cos_ref[...]
    sinv = sin_ref[...]
    lane = lax.broadcasted_iota(jnp.int32, (tm, LANES), 1)
    first_half = (lane % AXIS_DIM) < (AXIS_DIM // 2)

    def norm_rope(a, w_row, seg, scale):
        ms = jnp.dot((a * a).astype(BF16), seg, preferred_element_type=F32)
        y = a * lax.rsqrt(ms + RMS_EPS) * w_row
        outs = []
        for c in range(a.shape[1] // LANES):
            yc = y[:, c * LANES:(c + 1) * LANES]
            partner = jnp.where(first_half,
                                pltpu.roll(yc, LANES - AXIS_DIM // 2, 1),
                                pltpu.roll(yc, AXIS_DIM // 2, 1))
            outs.append((yc * cosv + partner * sinv) * scale)
        return outs

    aq = proj(0, ATTN_WIDTH)
    qs = norm_rope(aq, qw_ref[...], seg_ref[...], Q_SCALE)
    for c, qc in enumerate(qs):
        qt_ref[c * LANES:(c + 1) * LANES, :] = qc.T.astype(BF16)
    akv = proj(ATTN_WIDTH, ATTN_WIDTH + 2 * KV_WIDTH)
    kk = norm_rope(akv[:, :KV_WIDTH], kw_ref[...], seg_ref[:KV_WIDTH, :KV_WIDTH], 1.0)[0]
    k_ref[...] = kk.astype(BF16)
    vt_ref[...] = akv[:, KV_WIDTH:].T.astype(BF16)

    base = ATTN_WIDTH + 2 * KV_WIDTH
    ag = proj(base, base + ATTN_WIDTH)
    ag_ref[...] = (ag * _sigmoid(ag)).astype(BF16)
    base += ATTN_WIDTH
    hq = proj(base, base + HG_WIDTH)
    hq_ref[...] = (hq * _sigmoid(hq) * (HG_EXPAND ** -0.5)).astype(BF16)
    base += HG_WIDTH
    hi_ref[...] = proj(base, base + HG_WIDTH).astype(BF16)
    base += HG_WIDTH
    lbl = lbl_ref[...]
    for d in range(2):
        l0 = lbl[2 * d:2 * d + 1, :]
        l1 = lbl[2 * d + 1:2 * d + 2, :]
        mx = jnp.maximum(l0, l1)
        e0 = jnp.exp(l0 - mx)
        e1 = jnp.exp(l1 - mx)
        lb = e0 / (e0 + e1)
        z = proj(base, base + HG_WIDTH)
        lf_ref[:, d * HG_WIDTH:(d + 1) * HG_WIDTH] = jnp.log(lb + (1.0 - lb) * _sigmoid(z)) * LOG2E
        base += HG_WIDTH
    hg = proj(base, base + HG_WIDTH)
    hg_ref[...] = (hg * _sigmoid(hg)).astype(BF16)


def _proj_call(x, w_in_b, cos_t, sin_t, qw, kw, lbl, seg):
    B, S, D = x.shape
    tm = PROJ_TM
    nt = S // tm
    row = lambda b, i: (b, i, 0)
    const2 = lambda b, i: (0, 0)
    out_shape = (
        jax.ShapeDtypeStruct((B, ATTN_WIDTH, S), BF16),
        jax.ShapeDtypeStruct((B, S, KV_WIDTH), BF16),
        jax.ShapeDtypeStruct((B, KV_WIDTH, S), BF16),
        jax.ShapeDtypeStruct((B, S, ATTN_WIDTH), BF16),
        jax.ShapeDtypeStruct((B, S, HG_WIDTH), BF16),
        jax.ShapeDtypeStruct((B, S, HG_WIDTH), BF16),
        jax.ShapeDtypeStruct((B, S, 2 * HG_WIDTH), F32),
        jax.ShapeDtypeStruct((B, S, HG_WIDTH), BF16),
    )
    in_specs = [
        pl.BlockSpec((None, tm, D), row),
        pl.BlockSpec((D, IN_WIDTH), const2),
        pl.BlockSpec((tm, LANES), lambda b, i: (i, 0)),
        pl.BlockSpec((tm, LANES), lambda b, i: (i, 0)),
        pl.BlockSpec((1, ATTN_WIDTH), const2),
        pl.BlockSpec((1, KV_WIDTH), const2),
        pl.BlockSpec((4, HG_WIDTH), const2),
        pl.BlockSpec((ATTN_WIDTH, ATTN_WIDTH), const2),
    ]
    out_specs = (
        pl.BlockSpec((None, ATTN_WIDTH, tm), lambda b, i: (b, 0, i)),
        pl.BlockSpec((None, tm, KV_WIDTH), row),
        pl.BlockSpec((None, KV_WIDTH, tm), lambda b, i: (b, 0, i)),
        pl.BlockSpec((None, tm, ATTN_WIDTH), row),
        pl.BlockSpec((None, tm, HG_WIDTH), row),
        pl.BlockSpec((None, tm, HG_WIDTH), row),
        pl.BlockSpec((None, tm, 2 * HG_WIDTH), row),
        pl.BlockSpec((None, tm, HG_WIDTH), row),
    )
    return pl.pallas_call(
        _proj_kernel,
        grid=(B, nt),
        in_specs=in_specs,
        out_specs=out_specs,
        out_shape=out_shape,
        compiler_params=pltpu.CompilerParams(
            dimension_semantics=("arbitrary", "arbitrary"),
            vmem_limit_bytes=VMEM_LIMIT),
        name="in_proj",
    )(x, w_in_b, cos_t, sin_t, qw, kw, lbl, seg)


def _attn_kernel(qt_ref, k_ref, vt_ref, o_ref, s0_ref, s1_ref, vx_ref, acc_ref):
    S = k_ref.shape[0]
    G = N_Q_HEADS // N_KV_HEADS
    tq, tk = ATTN_TQ, ATTN_TK
    cols = G * tq
    n_q = S // tq
    n_c = S // tk
    vrows = vx_ref.shape[1]

    for g in range(N_KV_HEADS):
        vx_ref[g, 0:HEAD_DIM, :] = vt_ref[g * HEAD_DIM:(g + 1) * HEAD_DIM, :]
        vx_ref[g, HEAD_DIM:vrows, :] = jnp.ones((vrows - HEAD_DIM, S), BF16)

    def q_weights(i, g):
        c0 = pl.multiple_of(i * tq, tq)
        qs = jnp.concatenate(
            [qt_ref[(G * g + h) * HEAD_DIM:(G * g + h + 1) * HEAD_DIM, pl.ds(c0, tq)]
             for h in range(G)], axis=1)
        z = jnp.zeros_like(qs)
        return jnp.concatenate([qs, z] if g == 0 else [z, qs], axis=0)

    def scores(c, w, s_ref, mx):
        rows = pl.ds(pl.multiple_of(c * tk, tk), tk)
        s = jnp.dot(k_ref[rows, :], w, preferred_element_type=F32)
        s_ref[rows, :] = s
        return jnp.maximum(mx, jnp.max(s.reshape(tk // 8, 8, cols), axis=0))

    def probs_pv(c, s_ref, m, g):
        rows = pl.ds(pl.multiple_of(c * tk, tk), tk)
        p = jnp.exp2(s_ref[rows, :] - m).astype(BF16)
        acc_ref[...] += jnp.dot(vx_ref[g, :, rows], p, preferred_element_type=F32)

    def finish(i, g):
        acc = acc_ref[...]
        o_t = acc[0:HEAD_DIM, :] / acc[HEAD_DIM:HEAD_DIM + 1, :]
        o = o_t.T
        blk = jnp.concatenate([o[h * tq:(h + 1) * tq, :] for h in range(G)], axis=1)
        r0 = pl.multiple_of(i * tq, tq)
        o_ref[pl.ds(r0, tq), g * G * HEAD_DIM:(g + 1) * G * HEAD_DIM] = blk.astype(o_ref.dtype)

    mx0 = jnp.full((8, cols), -jnp.inf, F32)
    col_max = lambda mx: jnp.max(mx, axis=0, keepdims=True)

    w00 = q_weights(0, 0)
    m_first = col_max(lax.fori_loop(
        0, n_c, lambda c, mx: scores(c, w00, s0_ref, mx), mx0))

    def outer(i, m_g0):
        w1 = q_weights(i, 1)
        acc_ref[...] = jnp.zeros(acc_ref.shape, F32)

        def body_b(c, mx):
            mx = scores(c, w1, s1_ref, mx)
            probs_pv(c, s0_ref, m_g0, 0)
            return mx

        m_g1 = col_max(lax.fori_loop(0, n_c, body_b, mx0))
        finish(i, 0)
        w0 = q_weights(jnp.minimum(i + 1, n_q - 1), 0)
        acc_ref[...] = jnp.zeros(acc_ref.shape, F32)

        def body_a(c, mx):
            mx = scores(c, w0, s0_ref, mx)
            probs_pv(c, s1_ref, m_g1, 1)
            return mx

        m_next = col_max(lax.fori_loop(0, n_c, body_a, mx0))
        finish(i, 1)
        return m_next

    lax.fori_loop(0, n_q, outer, m_first)


def _attn_call(qt, k, vt):
    B, _, S = qt.shape
    G = N_Q_HEADS // N_KV_HEADS
    cols = G * ATTN_TQ
    whole = lambda b: (b, 0, 0)
    return pl.pallas_call(
        _attn_kernel,
        grid=(B,),
        in_specs=[
            pl.BlockSpec((None, ATTN_WIDTH, S), whole),
            pl.BlockSpec((None, S, KV_WIDTH), whole),
            pl.BlockSpec((None, KV_WIDTH, S), whole),
        ],
        out_specs=pl.BlockSpec((None, S, ATTN_WIDTH), whole),
        out_shape=jax.ShapeDtypeStruct((B, S, ATTN_WIDTH), BF16),
        scratch_shapes=[
            pltpu.VMEM((S, cols), F32),
            pltpu.VMEM((S, cols), F32),
            pltpu.VMEM((N_KV_HEADS, HEAD_DIM + 16, S), BF16),
            pltpu.VMEM((HEAD_DIM + 16, cols), F32),
        ],
        compiler_params=pltpu.CompilerParams(
            dimension_semantics=("arbitrary",),
            vmem_limit_bytes=VMEM_LIMIT),
        name="attention",
    )(qt, k, vt)


def _hgrn_kernel(q_ref, i_ref, lff_ref, lfb_ref, g_ref, nw_ref, o_ref,
                 qm_ref, km_ref, qb_ref, kd_ref, dec_ref, ut_ref, st_ref, cur_ref):
    S = q_ref.shape[0]
    C = HG_CHUNK
    K = HG_EXPAND
    n_chunks = S // C
    nb = HG_BLOCK
    rows_b = nb * C
    r = lax.broadcasted_iota(jnp.int32, (C, C), 0)
    c = lax.broadcasted_iota(jnp.int32, (C, C), 1)
    lower = r >= c
    upper = r <= c
    tri2 = [jnp.concatenate([t, t], axis=1) for t in
            (jnp.where(lower, 1.0, 0.0).astype(BF16), jnp.where(upper, 1.0, 0.0).astype(BF16))]
    mid = (C // 2, C - 1 - C // 2)
    last = (C - 1, 0)

    def prep(t, _):
        r0 = pl.multiple_of(t * rows_b, rows_b)
        rows = pl.ds(r0, rows_b)
        q3 = q_ref[rows, :].astype(F32).reshape(nb, C, K)
        lfs = (lff_ref[rows, :], lfb_ref[rows, :])
        b3s = []
        for d, lf in enumerate(lfs):
            hi = lf.astype(BF16)
            lo = (lf - hi.astype(F32)).astype(BF16)
            b3s.append(jnp.stack([
                jnp.dot(tri2[d],
                        jnp.concatenate([hi[j * C:(j + 1) * C], lo[j * C:(j + 1) * C]], axis=0),
                        preferred_element_type=F32)
                for j in range(nb)]))
        for d, (lf, b3) in enumerate(zip(lfs, b3s)):
            k3 = (1.0 - jnp.exp2(lf)).reshape(nb, C, K)
            bm = b3[:, mid[d]:mid[d] + 1, :]
            bl = b3[:, last[d]:last[d] + 1, :]
            e = jnp.exp2(b3 - bm)
            einv = jnp.exp2(bm - b3)
            qm = q3 * e
            km = k3 * einv
            qm_ref[d, rows, :] = qm.reshape(rows_b, K).astype(BF16)
            km_ref[d, rows, :] = km.reshape(rows_b, K).astype(BF16)
            qb_ref[rows, d * K:(d + 1) * K] = (qm * jnp.exp2(bm)).reshape(rows_b, K).astype(BF16)
            kd_ref[rows, d * K:(d + 1) * K] = (km * jnp.exp2(bl - bm)).reshape(rows_b, K).astype(BF16)
            dec_ref[d, pl.ds(pl.multiple_of(t * nb, nb), nb), :] = jnp.exp2(bl).reshape(nb, K)
        for j in range(nb):
            cr = pl.ds(r0 + j * C, C)
            ivt = i_ref[cr, :].astype(F32).T.astype(BF16)
            ut_ref[t * nb + j] = jnp.dot(ivt, kd_ref[cr, :], preferred_element_type=F32)
        return 0

    lax.fori_loop(0, n_chunks // nb, prep, 0)

    cur_ref[...] = jnp.zeros(cur_ref.shape, F32)

    def chain(n, _):
        m = n_chunks - 1 - n
        sf = cur_ref[0]
        st_ref[n, :, 0:K] = sf.astype(BF16)
        cur_ref[0] = dec_ref[0, pl.ds(n, 1), :] * sf + ut_ref[n, :, 0:K]
        sb = cur_ref[1]
        st_ref[m, :, K:2 * K] = sb.astype(BF16)
        cur_ref[1] = dec_ref[1, pl.ds(m, 1), :] * sb + ut_ref[m, :, K:2 * K]
        return 0

    lax.fori_loop(0, n_chunks, chain, 0)

    nw = nw_ref[...]
    nt_dims = (((1,), (1,)), ((), ()))

    no = HG_BLOCK_OUT

    def outp(t, _):
        r0 = pl.multiple_of(t * (no * C), no * C)
        crs = [pl.ds(r0 + j * C, C) for j in range(no)]
        a_f = [lax.dot_general(qm_ref[0, cr, :], km_ref[0, cr, :], nt_dims,
                               preferred_element_type=F32) for cr in crs]
        a_b = [lax.dot_general(qm_ref[1, cr, :], km_ref[1, cr, :], nt_dims,
                               preferred_element_type=F32) for cr in crs]
        a = [(jnp.where(lower, f, 0.0) + jnp.where(upper, b, 0.0)).astype(BF16)
             for f, b in zip(a_f, a_b)]
        outs = [jnp.dot(a[j], i_ref[cr, :], preferred_element_type=F32)
                + lax.dot_general(qb_ref[cr, :], st_ref[t * no + j], nt_dims,
                                  preferred_element_type=F32)
                for j, cr in enumerate(crs)]
        for cr, o in zip(crs, outs):
            ms = jnp.mean(o * o, axis=-1, keepdims=True)
            y = o * lax.rsqrt(ms + RMS_EPS) * nw
            o_ref[cr, :] = (y * g_ref[cr, :].astype(F32)).astype(o_ref.dtype)
        return 0

    lax.fori_loop(0, n_chunks // no, outp, 0)


def _hgrn_call(hq, hi, lf, hg, nw):
    B, S, _ = hq.shape
    n_chunks = S // HG_CHUNK
    K = HG_EXPAND
    blk = lambda b, h: (b, 0, h)
    return pl.pallas_call(
        _hgrn_kernel,
        grid=(B, HG_HEADS),
        in_specs=[
            pl.BlockSpec((None, S, K), blk),
            pl.BlockSpec((None, S, K), blk),
            pl.BlockSpec((None, S, K), blk),
            pl.BlockSpec((None, S, K), lambda b, h: (b, 0, HG_HEADS + h)),
            pl.BlockSpec((None, S, K), blk),
            pl.BlockSpec((1, K), lambda b, h: (0, 0)),
        ],
        out_specs=pl.BlockSpec((None, S, K), blk),
        out_shape=jax.ShapeDtypeStruct((B, S, HG_WIDTH), BF16),
        scratch_shapes=[
            pltpu.VMEM((2, S, K), BF16),
            pltpu.VMEM((2, S, K), BF16),
            pltpu.VMEM((S, 2 * K), BF16),
            pltpu.VMEM((S, 2 * K), BF16),
            pltpu.VMEM((2, n_chunks, K), F32),
            pltpu.VMEM((n_chunks, K, 2 * K), F32),
            pltpu.VMEM((n_chunks, K, 2 * K), BF16),
            pltpu.VMEM((2, K, K), F32),
        ],
        compiler_params=pltpu.CompilerParams(
            dimension_semantics=("arbitrary", "arbitrary"),
            vmem_limit_bytes=VMEM_LIMIT),
        name="hgrn2",
    )(hq, hi, lf, lf, hg, nw)


def _out_kernel(x_ref, attn_ref, ag_ref, hgb_ref, anw_ref, w_ref, lnw_ref, lnb_ref, o_ref):
    a = attn_ref[...].astype(F32)
    ms = jnp.mean(a * a, axis=-1, keepdims=True)
    ab = a * lax.rsqrt(ms + RMS_EPS) * anw_ref[...] * ag_ref[...].astype(F32)
    y = (jnp.dot(ab.astype(BF16), w_ref[:ATTN_WIDTH, :], preferred_element_type=F32)
         + jnp.dot(hgb_ref[...], w_ref[ATTN_WIDTH:, :], preferred_element_type=F32))
    z = DEEPNORM_ALPHA * x_ref[...] + y
    mu = jnp.mean(z, axis=-1, keepdims=True)
    zc = z - mu
    var = jnp.mean(zc * zc, axis=-1, keepdims=True)
    o_ref[...] = zc * lax.rsqrt(var + LN_EPS) * lnw_ref[...] + lnb_ref[...]


def _out_call(x, attn, ag, hgb, anw, w_out_b, lnw, lnb):
    B, S, D = x.shape
    tm = OUT_TM
    row = lambda b, i: (b, i, 0)
    const2 = lambda b, i: (0, 0)
    return pl.pallas_call(
        _out_kernel,
        grid=(B, S // tm),
        in_specs=[
            pl.BlockSpec((None, tm, D), row),
            pl.BlockSpec((None, tm, ATTN_WIDTH), row),
            pl.BlockSpec((None, tm, ATTN_WIDTH), row),
            pl.BlockSpec((None, tm, HG_WIDTH), row),
            pl.BlockSpec((1, ATTN_WIDTH), const2),
            pl.BlockSpec((ATTN_WIDTH + HG_WIDTH, D), const2),
            pl.BlockSpec((1, D), const2),
            pl.BlockSpec((1, D), const2),
        ],
        out_specs=pl.BlockSpec((None, tm, D), row),
        out_shape=jax.ShapeDtypeStruct((B, S, D), x.dtype),
        compiler_params=pltpu.CompilerParams(
            dimension_semantics=("arbitrary", "arbitrary"),
            vmem_limit_bytes=VMEM_LIMIT),
        name="out_proj_ln",
    )(x, attn, ag, hgb, anw, w_out_b, lnw, lnb)


def _rope_tables(seq_len):
    t = jnp.arange(seq_len, dtype=jnp.int32)
    row = (t // GRID_W).astype(F32)
    col = (t % GRID_W).astype(F32)
    inv = ROPE_THETA ** (-jnp.arange(0, AXIS_DIM, 2, dtype=F32) / AXIS_DIM)
    lane = jnp.arange(LANES, dtype=jnp.int32)
    d = lane % HEAD_DIM
    use_col = (d // AXIS_DIM) == 1
    dd = d % AXIS_DIM
    first = dd < (AXIS_DIM // 2)
    freq = inv[dd % (AXIS_DIM // 2)]
    pos = jnp.where(use_col[None, :], col[:, None], row[:, None])
    ang = pos * freq[None, :]
    cos_t = jnp.cos(ang)
    sin_t = jnp.where(first[None, :], -jnp.sin(ang), jnp.sin(ang))
    return cos_t, sin_t


def kernel(x, w_in, q_norm_w, k_norm_w, attn_norm_w, hg_lb_logits, hg_norm_w, w_out, ln_w, ln_b):
    B, S, D = x.shape
    assert w_in.shape == (DEPTH, D, IN_WIDTH) and DEPTH == 1
    cos_t, sin_t = _rope_tables(S)
    qw = jnp.tile(q_norm_w[0].astype(F32), N_Q_HEADS).reshape(1, ATTN_WIDTH)
    kw = jnp.tile(k_norm_w[0].astype(F32), N_KV_HEADS).reshape(1, KV_WIDTH)
    lbl = hg_lb_logits.astype(F32).reshape(2 * (DEPTH + 1), HG_WIDTH)
    hid = jnp.arange(ATTN_WIDTH, dtype=jnp.int32) // HEAD_DIM
    seg = jnp.where(hid[:, None] == hid[None, :], 1.0 / HEAD_DIM, 0.0).astype(BF16)

    qt, k, vt, ag, hq, hi, lf, hg = _proj_call(
        x, w_in[0].astype(BF16), cos_t, sin_t, qw, kw, lbl, seg)
    attn = _attn_call(qt, k, vt)
    hgb = _hgrn_call(hq, hi, lf, hg, hg_norm_w[0].astype(F32).reshape(1, HG_EXPAND))
    return _out_call(
        x, attn, ag, hgb,
        attn_norm_w[0].astype(F32).reshape(1, ATTN_WIDTH),
        w_out[0].astype(BF16),
        ln_w[0].astype(F32).reshape(1, D),
        ln_b[0].astype(F32).reshape(1, D))
```

```python
import functools

import jax
import jax.numpy as jnp
from jax import lax
from jax.experimental import pallas as pl
from jax.experimental.pallas import tpu as pltpu

F32 = jnp.float32
BF16 = jnp.bfloat16

GRID_W = 64
ATTN_WIDTH = 512
HEAD_DIM = 64
N_Q_HEADS = 8
N_KV_HEADS = 2
KV_WIDTH = 128
AXIS_DIM = 32
ROPE_THETA = 10000.0
HG_WIDTH = 512
HG_EXPAND = 128
HG_HEADS = 4
HG_CHUNK = 64
RMS_EPS = 1e-6
LN_EPS = 1e-5
DEPTH = 1
DEEPNORM_ALPHA = (2 * DEPTH) ** 0.25
IN_WIDTH = 3840
LOG2E = 1.4426950408889634
Q_SCALE = HEAD_DIM ** -0.5 * LOG2E

LANES = 128
VMEM_LIMIT = 56 * 1024 * 1024

PROJ_TM = 512
ATTN_TQ = 128
ATTN_TK = 512
HG_BLOCK = 8
HG_BLOCK_OUT = 16
OUT_TM = 512


def _exact_zero(x):
    bits = lax.bitcast_convert_type(x, jnp.uint32)
    return ((bits >> 16) >> 16).astype(F32)


def _sigmoid(x):
    return 1.0 / (1.0 + jnp.exp(-x))


def _proj_kernel(x_ref, w_ref, cos_ref, sin_ref, qw_ref, kw_ref, lbl_ref, seg_ref,
                 qt_ref, k_ref, vt_ref, ag_ref, hq_ref, hi_ref, lf_ref, hg_ref):
    xb = x_ref[...].astype(BF16)
    tm = xb.shape[0]

    def proj(a, b):
        return jnp.dot(xb, w_ref[:, a:b], preferred_element_type=F32)

    cosv = cos_ref[...]
    sinv = sin_ref[...]
    lane = lax.broadcasted_iota(jnp.int32, (tm, LANES), 1)
    first_half = (lane % AXIS_DIM) < (AXIS_DIM // 2)

    def norm_rope(a, w_row, seg, scale):
        ms = jnp.dot((a * a).astype(BF16), seg, preferred_element_type=F32)
        y = a * lax.rsqrt(ms + RMS_EPS) * w_row
        outs = []
        for c in range(a.shape[1] // LANES):
            yc = y[:, c * LANES:(c + 1) * LANES]
            partner = jnp.where(first_half,
                                pltpu.roll(yc, LANES - AXIS_DIM // 2, 1),
                                pltpu.roll(yc, AXIS_DIM // 2, 1))
            outs.append((yc * cosv + partner * sinv) * scale)
        return outs

    aq = proj(0, ATTN_WIDTH)
    qs = norm_rope(aq, qw_ref[...], seg_ref[...], Q_SCALE)
    for c, qc in enumerate(qs):
        qt_ref[c * LANES:(c + 1) * LANES, :] = qc.T.astype(BF16)
    akv = proj(ATTN_WIDTH, ATTN_WIDTH + 2 * KV_WIDTH)
    kk = norm_rope(akv[:, :KV_WIDTH], kw_ref[...], seg_ref[:KV_WIDTH, :KV_WIDTH], 1.0)[0]
    k_ref[...] = kk.astype(BF16)
    vt_ref[...] = akv[:, KV_WIDTH:].T.astype(BF16)

    base = ATTN_WIDTH + 2 * KV_WIDTH
    ag = proj(base, base + ATTN_WIDTH)
    ag_ref[...] = (ag * _sigmoid(ag)).astype(BF16)
    base += ATTN_WIDTH
    hq = proj(base, base + HG_WIDTH)
    hq_ref[...] = (hq * _sigmoid(hq) * (HG_EXPAND ** -0.5)).astype(BF16)
    base += HG_WIDTH
    hi_ref[...] = proj(base, base + HG_WIDTH).astype(BF16)
    base += HG_WIDTH
    lbl = lbl_ref[...]
    for d in range(2):
        l0 = lbl[2 * d:2 * d + 1, :]
        l1 = lbl[2 * d + 1:2 * d + 2, :]
        mx = jnp.maximum(l0, l1)
        e0 = jnp.exp(l0 - mx)
        e1 = jnp.exp(l1 - mx)
        lb = e0 / (e0 + e1)
        z = proj(base, base + HG_WIDTH)
        lf_ref[:, d * HG_WIDTH:(d + 1) * HG_WIDTH] = jnp.log(lb + (1.0 - lb) * _sigmoid(z)) * LOG2E
        base += HG_WIDTH
    hg = proj(base, base + HG_WIDTH)
    hg_ref[...] = (hg * _sigmoid(hg)).astype(BF16)


def _proj_call(x, w_in_b, cos_t, sin_t, qw, kw, lbl, seg):
    B, S, D = x.shape
    tm = PROJ_TM
    nt = S // tm
    row = lambda b, i: (b, i, 0)
    const2 = lambda b, i: (0, 0)
    out_shape = (
        jax.ShapeDtypeStruct((B, ATTN_WIDTH, S), BF16),
        jax.ShapeDtypeStruct((B, S, KV_WIDTH), BF16),
        jax.ShapeDtypeStruct((B, KV_WIDTH, S), BF16),
        jax.ShapeDtypeStruct((B, S, ATTN_WIDTH), BF16),
        jax.ShapeDtypeStruct((B, S, HG_WIDTH), BF16),
        jax.ShapeDtypeStruct((B, S, HG_WIDTH), BF16),
        jax.ShapeDtypeStruct((B, S, 2 * HG_WIDTH), F32),
        jax.ShapeDtypeStruct((B, S, HG_WIDTH), BF16),
    )
    in_specs = [
        pl.BlockSpec((None, tm, D), row),
        pl.BlockSpec((D, IN_WIDTH), const2),
        pl.BlockSpec((tm, LANES), lambda b, i: (i, 0)),
        pl.BlockSpec((tm, LANES), lambda b, i: (i, 0)),
        pl.BlockSpec((1, ATTN_WIDTH), const2),
        pl.BlockSpec((1, KV_WIDTH), const2),
        pl.BlockSpec((4, HG_WIDTH), const2),
        pl.BlockSpec((ATTN_WIDTH, ATTN_WIDTH), const2),
    ]
    out_specs = (
        pl.BlockSpec((None, ATTN_WIDTH, tm), lambda b, i: (b, 0, i)),
        pl.BlockSpec((None, tm, KV_WIDTH), row),
        pl.BlockSpec((None, KV_WIDTH, tm), lambda b, i: (b, 0, i)),
        pl.BlockSpec((None, tm, ATTN_WIDTH), row),
        pl.BlockSpec((None, tm, HG_WIDTH), row),
        pl.BlockSpec((None, tm, HG_WIDTH), row),
        pl.BlockSpec((None, tm, 2 * HG_WIDTH), row),
        pl.BlockSpec((None, tm, HG_WIDTH), row),
    )
    return pl.pallas_call(
        _proj_kernel,
        grid=(B, nt),
        in_specs=in_specs,
        out_specs=out_specs,
        out_shape=out_shape,
        compiler_params=pltpu.CompilerParams(
            dimension_semantics=("arbitrary", "arbitrary"),
            vmem_limit_bytes=VMEM_LIMIT),
        name="in_proj",
    )(x, w_in_b, cos_t, sin_t, qw, kw, lbl, seg)


def _attn_kernel(qt_ref, k_ref, vt_ref, o_ref, s_ref, p_ref, vx_ref):
    S = k_ref.shape[0]
    G = N_Q_HEADS // N_KV_HEADS
    tq, tk = ATTN_TQ, ATTN_TK
    cols = G * tq
    n_units = N_KV_HEADS * (S // tq)
    n_c = S // tk
    vrows = vx_ref.shape[1]

    for g in range(N_KV_HEADS):
        vx_ref[g, 0:HEAD_DIM, :] = vt_ref[g * HEAD_DIM:(g + 1) * HEAD_DIM, :]
        vx_ref[g, HEAD_DIM:vrows, :] = jnp.ones((vrows - HEAD_DIM, S), BF16)

    row_group = lax.broadcasted_iota(jnp.int32, (N_KV_HEADS * HEAD_DIM, cols), 0) // HEAD_DIM

    def q_weights(u):
        i, g = u >> 1, u & 1
        c0 = pl.multiple_of(i * tq, tq)
        qs = jnp.concatenate(
            [qt_ref[pl.ds(pl.multiple_of((G * g + h) * HEAD_DIM, HEAD_DIM), HEAD_DIM),
                    pl.ds(c0, tq)] for h in range(G)], axis=1)
        return jnp.where(row_group == g, jnp.concatenate([qs, qs], axis=0), 0.0).astype(BF16)

    def stage(u, m_prev, do_scores=True, do_probs=True, do_pv=True):
        if do_scores:
            w = q_weights(u)
            mx = jnp.full((8, cols), -jnp.inf, F32)
        if do_pv:
            g_pv = (u - 2) & 1
            acc = jnp.zeros((vrows, cols), F32)
        for c in range(n_c):
            rows = pl.ds(c * tk, tk)
            m_c = m_prev
            if do_pv:
                acc = acc + jnp.dot(vx_ref[g_pv, :, rows], p_ref[rows, :],
                                    preferred_element_type=F32)
                if do_probs:
                    guard = sum(_exact_zero(acc[0:8, t * 2 * LANES:(t * 2 + 1) * LANES])
                                for t in range(cols // (2 * LANES)))
                    m_c = m_prev + jnp.tile(guard[0:1, :], (1, cols // LANES))
            if do_probs:
                p_ref[rows, :] = jnp.exp2(s_ref[rows, :] - m_c).astype(BF16)
            if do_scores:
                s = jnp.dot(k_ref[rows, :], w, preferred_element_type=F32)
                s_ref[rows, :] = s
                mx = jnp.maximum(mx, jnp.max(s.reshape(tk // 8, 8, cols), axis=0))
        if do_pv:
            o_t = acc[0:HEAD_DIM, :] / acc[HEAD_DIM:HEAD_DIM + 1, :]
            o = o_t.T
            blk = jnp.concatenate([o[h * tq:(h + 1) * tq, :] for h in range(G)], axis=1)
            r0 = pl.multiple_of(((u - 2) >> 1) * tq, tq)
            o_ref[g_pv, pl.ds(r0, tq), :] = blk.astype(o_ref.dtype)
        if do_scores:
            return jnp.max(mx, axis=0, keepdims=True)
        return m_prev

    zero = jnp.int32(0)
    m = stage(zero, None, do_probs=False, do_pv=False)
    m = stage(zero + 1, m, do_pv=False)
    m = lax.fori_loop(2, n_units, stage, m)
    m = stage(zero + n_units, m, do_scores=False)
    stage(zero + n_units + 1, m, do_scores=False, do_probs=False)


def _attn_call(qt, k, vt):
    B, _, S = qt.shape
    G = N_Q_HEADS // N_KV_HEADS
    cols = G * ATTN_TQ
    gw = G * HEAD_DIM
    whole = lambda b: (b, 0, 0)
    return pl.pallas_call(
        _attn_kernel,
        grid=(B,),
        in_specs=[
            pl.BlockSpec((None, ATTN_WIDTH, S), whole),
            pl.BlockSpec((None, S, KV_WIDTH), whole),
            pl.BlockSpec((None, KV_WIDTH, S), whole),
        ],
        out_specs=pl.BlockSpec((None, N_KV_HEADS, S, gw), lambda b: (b, 0, 0, 0)),
        out_shape=jax.ShapeDtypeStruct((B, N_KV_HEADS, S, gw), BF16),
        scratch_shapes=[
            pltpu.VMEM((S, cols), F32),
            pltpu.VMEM((S, cols), BF16),
            pltpu.VMEM((N_KV_HEADS, HEAD_DIM + 16, S), BF16),
        ],
        compiler_params=pltpu.CompilerParams(
            dimension_semantics=("arbitrary",),
            vmem_limit_bytes=VMEM_LIMIT),
        name="attention",
    )(qt, k, vt)


def _hgrn_kernel(q_ref, i_ref, lff_ref, lfb_ref, g_ref, nw_ref, o_ref,
                 qm_ref, km_ref, qb_ref, kd_ref, dec_ref, ut_ref, st_ref, cur_ref):
    S = q_ref.shape[0]
    C = HG_CHUNK
    K = HG_EXPAND
    n_chunks = S // C
    nb = HG_BLOCK
    rows_b = nb * C
    r = lax.broadcasted_iota(jnp.int32, (C, C), 0)
    c = lax.broadcasted_iota(jnp.int32, (C, C), 1)
    lower = r >= c
    upper = r <= c
    tri2 = [jnp.concatenate([t, t], axis=1) for t in
            (jnp.where(lower, 1.0, 0.0).astype(BF16), jnp.where(upper, 1.0, 0.0).astype(BF16))]
    mid = (C // 2, C - 1 - C // 2)
    last = (C - 1, 0)

    def prep(t, _):
        r0 = pl.multiple_of(t * rows_b, rows_b)
        rows = pl.ds(r0, rows_b)
        q3 = q_ref[rows, :].astype(F32).reshape(nb, C, K)
        lfs = (lff_ref[rows, :], lfb_ref[rows, :])
        b3s = []
        for d, lf in enumerate(lfs):
            hi = lf.astype(BF16)
            lo = (lf - hi.astype(F32)).astype(BF16)
            b3s.append(jnp.stack([
                jnp.dot(tri2[d],
                        jnp.concatenate([hi[j * C:(j + 1) * C], lo[j * C:(j + 1) * C]], axis=0),
                        preferred_element_type=F32)
                for j in range(nb)]))
        for d, (lf, b3) in enumerate(zip(lfs, b3s)):
            k3 = (1.0 - jnp.exp2(lf)).reshape(nb, C, K)
            bm = b3[:, mid[d]:mid[d] + 1, :]
            bl = b3[:, last[d]:last[d] + 1, :]
            e = jnp.exp2(b3 - bm)
            einv = jnp.exp2(bm - b3)
            qm = q3 * e
            km = k3 * einv
            qm_ref[d, rows, :] = qm.reshape(rows_b, K).astype(BF16)
            km_ref[d, rows, :] = km.reshape(rows_b, K).astype(BF16)
            qb_ref[rows, d * K:(d + 1) * K] = (qm * jnp.exp2(bm)).reshape(rows_b, K).astype(BF16)
            kd_ref[rows, d * K:(d + 1) * K] = (km * jnp.exp2(bl - bm)).reshape(rows_b, K).astype(BF16)
            dec_ref[d, pl.ds(pl.multiple_of(t * nb, nb), nb), :] = jnp.exp2(bl).reshape(nb, K)
        for j in range(nb):
            cr = pl.ds(r0 + j * C, C)
            ivt = i_ref[cr, :].astype(F32).T.astype(BF16)
            ut_ref[t * nb + j] = jnp.dot(ivt, kd_ref[cr, :], preferred_element_type=F32)
        return 0

    lax.fori_loop(0, n_chunks // nb, prep, 0)

    cur_ref[...] = jnp.zeros(cur_ref.shape, F32)

    def chain(n, _):
        m = n_chunks - 1 - n
        sf = cur_ref[0]
        st_ref[n, :, 0:K] = sf.astype(BF16)
        cur_ref[0] = dec_ref[0, pl.ds(n, 1), :] * sf + ut_ref[n, :, 0:K]
        sb = cur_ref[1]
        st_ref[m, :, K:2 * K] = sb.astype(BF16)
        cur_ref[1] = dec_ref[1, pl.ds(m, 1), :] * sb + ut_ref[m, :, K:2 * K]
        return 0

    lax.fori_loop(0, n_chunks, chain, 0)

    nw = nw_ref[...]
    nt_dims = (((1,), (1,)), ((), ()))

    no = HG_BLOCK_OUT

    def outp(t, _):
        r0 = pl.multiple_of(t * (no * C), no * C)
        crs = [pl.ds(r0 + j * C, C) for j in range(no)]
        a_f = [lax.dot_general(qm_ref[0, cr, :], km_ref[0, cr, :], nt_dims,
                               preferred_element_type=F32) for cr in crs]
        a_b = [lax.dot_general(qm_ref[1, cr, :], km_ref[1, cr, :], nt_dims,
                               preferred_element_type=F32) for cr in crs]
        a = [(jnp.where(lower, f, 0.0) + jnp.where(upper, b, 0.0)).astype(BF16)
             for f, b in zip(a_f, a_b)]
        outs = [jnp.dot(a[j], i_ref[cr, :], preferred_element_type=F32)
                + lax.dot_general(qb_ref[cr, :], st_ref[t * no + j], nt_dims,
                                  preferred_element_type=F32)
                for j, cr in enumerate(crs)]
        for cr, o in zip(crs, outs):
            ms = jnp.mean(o * o, axis=-1, keepdims=True)
            y = o * lax.rsqrt(ms + RMS_EPS) * nw
            o_ref[cr, :] = (y * g_ref[cr, :].astype(F32)).astype(o_ref.dtype)
        return 0

    lax.fori_loop(0, n_chunks // no, outp, 0)


def _hgrn_call(hq, hi, lf, hg, nw):
    B, S, _ = hq.shape
    n_chunks = S // HG_CHUNK
    K = HG_EXPAND
    blk = lambda b, h: (b, 0, h)
    return pl.pallas_call(
        _hgrn_kernel,
        grid=(B, HG_HEADS),
        in_specs=[
            pl.BlockSpec((None, S, K), blk),
            pl.BlockSpec((None, S, K), blk),
            pl.BlockSpec((None, S, K), blk),
            pl.BlockSpec((None, S, K), lambda b, h: (b, 0, HG_HEADS + h)),
            pl.BlockSpec((None, S, K), blk),
            pl.BlockSpec((1, K), lambda b, h: (0, 0)),
        ],
        out_specs=pl.BlockSpec((None, S, K), blk),
        out_shape=jax.ShapeDtypeStruct((B, S, HG_WIDTH), BF16),
        scratch_shapes=[
            pltpu.VMEM((2, S, K), BF16),
            pltpu.VMEM((2, S, K), BF16),
            pltpu.VMEM((S, 2 * K), BF16),
            pltpu.VMEM((S, 2 * K), BF16),
            pltpu.VMEM((2, n_chunks, K), F32),
            pltpu.VMEM((n_chunks, K, 2 * K), F32),
            pltpu.VMEM((n_chunks, K, 2 * K), BF16),
            pltpu.VMEM((2, K, K), F32),
        ],
        compiler_params=pltpu.CompilerParams(
            dimension_semantics=("arbitrary", "arbitrary"),
            vmem_limit_bytes=VMEM_LIMIT),
        name="hgrn2",
    )(hq, hi, lf, lf, hg, nw)


def _out_kernel(x_ref, attn_ref, ag_ref, hgb_ref, anw_ref, w_ref, lnw_ref, lnb_ref, o_ref):
    a = jnp.concatenate([attn_ref[g] for g in range(N_KV_HEADS)], axis=1).astype(F32)
    ms = jnp.mean(a * a, axis=-1, keepdims=True)
    ab = a * lax.rsqrt(ms + RMS_EPS) * anw_ref[...] * ag_ref[...].astype(F32)
    y = (jnp.dot(ab.astype(BF16), w_ref[:ATTN_WIDTH, :], preferred_element_type=F32)
         + jnp.dot(hgb_ref[...], w_ref[ATTN_WIDTH:, :], preferred_element_type=F32))
    z = DEEPNORM_ALPHA * x_ref[...] + y
    mu = jnp.mean(z, axis=-1, keepdims=True)
    zc = z - mu
    var = jnp.mean(zc * zc, axis=-1, keepdims=True)
    o_ref[...] = zc * lax.rsqrt(var + LN_EPS) * lnw_ref[...] + lnb_ref[...]


def _out_call(x, attn, ag, hgb, anw, w_out_b, lnw, lnb):
    B, S, D = x.shape
    tm = OUT_TM
    row = lambda b, i: (b, i, 0)
    const2 = lambda b, i: (0, 0)
    return pl.pallas_call(
        _out_kernel,
        grid=(B, S // tm),
        in_specs=[
            pl.BlockSpec((None, tm, D), row),
            pl.BlockSpec((None, N_KV_HEADS, tm, ATTN_WIDTH // N_KV_HEADS), lambda b, i: (b, 0, i, 0)),
            pl.BlockSpec((None, tm, ATTN_WIDTH), row),
            pl.BlockSpec((None, tm, HG_WIDTH), row),
            pl.BlockSpec((1, ATTN_WIDTH), const2),
            pl.BlockSpec((ATTN_WIDTH + HG_WIDTH, D), const2),
            pl.BlockSpec((1, D), const2),
            pl.BlockSpec((1, D), const2),
        ],
        out_specs=pl.BlockSpec((None, tm, D), row),
        out_shape=jax.ShapeDtypeStruct((B, S, D), x.dtype),
        compiler_params=pltpu.CompilerParams(
            dimension_semantics=("arbitrary", "arbitrary"),
            vmem_limit_bytes=VMEM_LIMIT),
        name="out_proj_ln",
    )(x, attn, ag, hgb, anw, w_out_b, lnw, lnb)


def _rope_tables(seq_len):
    t = jnp.arange(seq_len, dtype=jnp.int32)
    row = (t // GRID_W).astype(F32)
    col = (t % GRID_W).astype(F32)
    inv = ROPE_THETA ** (-jnp.arange(0, AXIS_DIM, 2, dtype=F32) / AXIS_DIM)
    lane = jnp.arange(LANES, dtype=jnp.int32)
    d = lane % HEAD_DIM
    use_col = (d // AXIS_DIM) == 1
    dd = d % AXIS_DIM
    first = dd < (AXIS_DIM // 2)
    freq = inv[dd % (AXIS_DIM // 2)]
    pos = jnp.where(use_col[None, :], col[:, None], row[:, None])
    ang = pos * freq[None, :]
    cos_t = jnp.cos(ang)
    sin_t = jnp.where(first[None, :], -jnp.sin(ang), jnp.sin(ang))
    return cos_t, sin_t


def kernel(x, w_in, q_norm_w, k_norm_w, attn_norm_w, hg_lb_logits, hg_norm_w, w_out, ln_w, ln_b):
    B, S, D = x.shape
    assert w_in.shape == (DEPTH, D, IN_WIDTH) and DEPTH == 1
    cos_t, sin_t = _rope_tables(S)
    qw = jnp.tile(q_norm_w[0].astype(F32), N_Q_HEADS).reshape(1, ATTN_WIDTH)
    kw = jnp.tile(k_norm_w[0].astype(F32), N_KV_HEADS).reshape(1, KV_WIDTH)
    lbl = hg_lb_logits.astype(F32).reshape(2 * (DEPTH + 1), HG_WIDTH)
    hid = jnp.arange(ATTN_WIDTH, dtype=jnp.int32) // HEAD_DIM
    seg = jnp.where(hid[:, None] == hid[None, :], 1.0 / HEAD_DIM, 0.0).astype(BF16)

    qt, k, vt, ag, hq, hi, lf, hg = _proj_call(
        x, w_in[0].astype(BF16), cos_t, sin_t, qw, kw, lbl, seg)
    attn = _attn_call(qt, k, vt)
    hgb = _hgrn_call(hq, hi, lf, hg, hg_norm_w[0].astype(F32).reshape(1, HG_EXPAND))
    return _out_call(
        x, attn, ag, hgb,
        attn_norm_w[0].astype(F32).reshape(1, ATTN_WIDTH),
        w_out[0].astype(BF16),
        ln_w[0].astype(F32).reshape(1, D),
        ln_b[0].astype(F32).reshape(1, D))
```

```python
import functools

import jax
import jax.numpy as jnp
from jax import lax
from jax.experimental import pallas as pl
from jax.experimental.pallas import tpu as pltpu

F32 = jnp.float32
BF16 = jnp.bfloat16

GRID_W = 64
ATTN_WIDTH = 512
HEAD_DIM = 64
N_Q_HEADS = 8
N_KV_HEADS = 2
KV_WIDTH = 128
AXIS_DIM = 32
ROPE_THETA = 10000.0
HG_WIDTH = 512
HG_EXPAND = 128
HG_HEADS = 4
HG_CHUNK = 64
RMS_EPS = 1e-6
LN_EPS = 1e-5
DEPTH = 1
DEEPNORM_ALPHA = (2 * DEPTH) ** 0.25
IN_WIDTH = 3840
LOG2E = 1.4426950408889634
Q_SCALE = HEAD_DIM ** -0.5 * LOG2E

LANES = 128
VMEM_LIMIT = 56 * 1024 * 1024

PROJ_TM = 512
ATTN_TQ = 256
ATTN_TK = 512
HG_BLOCK = 8
HG_BLOCK_OUT = 16
OUT_TM = 512
OUT_SUB = 256


def _exact_zero(x):
    bits = lax.bitcast_convert_type(x, jnp.uint32)
    return ((bits >> 16) >> 16).astype(F32)


def _sigmoid(x):
    return 1.0 / (1.0 + jnp.exp(-x))


def _proj_kernel(x_ref, w_ref, cos_ref, sin_ref, qw_ref, kw_ref, lbl_ref, seg_ref,
                 qt_ref, k_ref, vt_ref, ag_ref, hq_ref, hi_ref, lf_ref, hg_ref):
    xb = x_ref[...].astype(BF16)
    tm = xb.shape[0]

    def proj(a, b):
        return jnp.dot(xb, w_ref[:, a:b], preferred_element_type=F32)

    cosv = cos_ref[...]
    sinv = sin_ref[...]
    lane = lax.broadcasted_iota(jnp.int32, (tm, LANES), 1)
    first_half = (lane % AXIS_DIM) < (AXIS_DIM // 2)

    def head_mean_sq(a, seg):
        return jnp.dot((a * a).astype(BF16), seg, preferred_element_type=F32)

    def norm_rope(a, ms, w_row, scale):
        y = a * lax.rsqrt(ms + RMS_EPS) * w_row
        outs = []
        for c in range(a.shape[1] // LANES):
            yc = y[:, c * LANES:(c + 1) * LANES]
            partner = jnp.where(first_half,
                                pltpu.roll(yc, LANES - AXIS_DIM // 2, 1),
                                pltpu.roll(yc, AXIS_DIM // 2, 1))
            outs.append((yc * cosv + partner * sinv) * scale)
        return outs

    aq = proj(0, ATTN_WIDTH)
    akv = proj(ATTN_WIDTH, ATTN_WIDTH + 2 * KV_WIDTH)
    ms_q = head_mean_sq(aq, seg_ref[...])
    base = ATTN_WIDTH + 2 * KV_WIDTH
    ag = proj(base, base + ATTN_WIDTH)
    ak = akv[:, :KV_WIDTH]
    ms_k = head_mean_sq(ak, seg_ref[:KV_WIDTH, :KV_WIDTH])
    for c, qc in enumerate(norm_rope(aq, ms_q, qw_ref[...], Q_SCALE)):
        qt_ref[c * LANES:(c + 1) * LANES, :] = qc.T.astype(BF16)
    k_ref[...] = norm_rope(ak, ms_k, kw_ref[...], 1.0)[0].astype(BF16)
    vt_ref[...] = akv[:, KV_WIDTH:].T.astype(BF16)
    ag_ref[...] = (ag * _sigmoid(ag)).astype(BF16)
    base += ATTN_WIDTH
    hq = proj(base, base + HG_WIDTH)
    hq_ref[...] = (hq * _sigmoid(hq) * (HG_EXPAND ** -0.5)).astype(BF16)
    base += HG_WIDTH
    hi_base = base
    base += HG_WIDTH
    lbl = lbl_ref[...]
    for d in range(2):
        l0 = lbl[2 * d:2 * d + 1, :]
        l1 = lbl[2 * d + 1:2 * d + 2, :]
        mx = jnp.maximum(l0, l1)
        e0 = jnp.exp(l0 - mx)
        e1 = jnp.exp(l1 - mx)
        lb = e0 / (e0 + e1)
        z = proj(base, base + HG_WIDTH)
        lf_ref[:, d * HG_WIDTH:(d + 1) * HG_WIDTH] = jnp.log(lb + (1.0 - lb) * _sigmoid(z)) * LOG2E
        base += HG_WIDTH
    hg = proj(base, base + HG_WIDTH)
    hg_ref[...] = (hg * _sigmoid(hg)).astype(BF16)
    hi_ref[...] = proj(hi_base, hi_base + HG_WIDTH).astype(BF16)


def _proj_call(x, w_in_b, cos_t, sin_t, qw, kw, lbl, seg):
    B, S, D = x.shape
    tm = PROJ_TM
    nt = S // tm
    row = lambda b, i: (b, i, 0)
    const2 = lambda b, i: (0, 0)
    out_shape = (
        jax.ShapeDtypeStruct((B, ATTN_WIDTH, S), BF16),
        jax.ShapeDtypeStruct((B, S, KV_WIDTH), BF16),
        jax.ShapeDtypeStruct((B, KV_WIDTH, S), BF16),
        jax.ShapeDtypeStruct((B, S, ATTN_WIDTH), BF16),
        jax.ShapeDtypeStruct((B, S, HG_WIDTH), BF16),
        jax.ShapeDtypeStruct((B, S, HG_WIDTH), BF16),
        jax.ShapeDtypeStruct((B, S, 2 * HG_WIDTH), F32),
        jax.ShapeDtypeStruct((B, S, HG_WIDTH), BF16),
    )
    in_specs = [
        pl.BlockSpec((None, tm, D), row),
        pl.BlockSpec((D, IN_WIDTH), const2),
        pl.BlockSpec((tm, LANES), lambda b, i: (i, 0)),
        pl.BlockSpec((tm, LANES), lambda b, i: (i, 0)),
        pl.BlockSpec((1, ATTN_WIDTH), const2),
        pl.BlockSpec((1, KV_WIDTH), const2),
        pl.BlockSpec((4, HG_WIDTH), const2),
        pl.BlockSpec((ATTN_WIDTH, ATTN_WIDTH), const2),
    ]
    out_specs = (
        pl.BlockSpec((None, ATTN_WIDTH, tm), lambda b, i: (b, 0, i)),
        pl.BlockSpec((None, tm, KV_WIDTH), row),
        pl.BlockSpec((None, KV_WIDTH, tm), lambda b, i: (b, 0, i)),
        pl.BlockSpec((None, tm, ATTN_WIDTH), row),
        pl.BlockSpec((None, tm, HG_WIDTH), row),
        pl.BlockSpec((None, tm, HG_WIDTH), row),
        pl.BlockSpec((None, tm, 2 * HG_WIDTH), row),
        pl.BlockSpec((None, tm, HG_WIDTH), row),
    )
    return pl.pallas_call(
        _proj_kernel,
        grid=(B, nt),
        in_specs=in_specs,
        out_specs=out_specs,
        out_shape=out_shape,
        compiler_params=pltpu.CompilerParams(
            dimension_semantics=("arbitrary", "arbitrary"),
            vmem_limit_bytes=VMEM_LIMIT),
        name="in_proj",
    )(x, w_in_b, cos_t, sin_t, qw, kw, lbl, seg)


def _attn_kernel(qt_ref, k_ref, vt_ref, o_ref, s_ref, p_ref, vx_ref):
    S = k_ref.shape[0]
    G = N_Q_HEADS // N_KV_HEADS
    tq, tk = ATTN_TQ, ATTN_TK
    cols = G * tq
    n_units = N_KV_HEADS * (S // tq)
    n_c = S // tk
    vrows = vx_ref.shape[1]

    for g in range(N_KV_HEADS):
        vx_ref[g, 0:HEAD_DIM, :] = vt_ref[g * HEAD_DIM:(g + 1) * HEAD_DIM, :]
        vx_ref[g, HEAD_DIM:vrows, :] = jnp.ones((vrows - HEAD_DIM, S), BF16)

    row_group = lax.broadcasted_iota(jnp.int32, (N_KV_HEADS * HEAD_DIM, cols), 0) // HEAD_DIM

    def q_weights(u):
        i, g = u >> 1, u & 1
        c0 = pl.multiple_of(i * tq, tq)
        qs = jnp.concatenate(
            [qt_ref[pl.ds(pl.multiple_of((G * g + h) * HEAD_DIM, HEAD_DIM), HEAD_DIM),
                    pl.ds(c0, tq)] for h in range(G)], axis=1)
        return jnp.where(row_group == g, jnp.concatenate([qs, qs], axis=0), 0.0).astype(BF16)

    def stage(u, m_prev, do_scores=True, do_probs=True, do_pv=True):
        if do_scores:
            w = q_weights(u)
            mx = jnp.full((8, cols), -jnp.inf, F32)
        if do_pv:
            g_pv = (u - 2) & 1
            acc = jnp.zeros((vrows, cols), F32)
        for c in range(n_c):
            rows = pl.ds(c * tk, tk)
            m_c = m_prev
            if do_pv:
                acc = acc + jnp.dot(vx_ref[g_pv, :, rows], p_ref[rows, :],
                                    preferred_element_type=F32)
                if do_probs:
                    guard = sum(_exact_zero(acc[0:8, t * 2 * LANES:(t * 2 + 1) * LANES])
                                for t in range(cols // (2 * LANES)))
                    m_c = m_prev + jnp.tile(guard[0:1, :], (1, cols // LANES))
            if do_probs:
                p_ref[rows, :] = jnp.exp2(s_ref[rows, :] - m_c).astype(BF16)
            if do_scores:
                s = jnp.dot(k_ref[rows, :], w, preferred_element_type=F32)
                s_ref[rows, :] = s
                mx = jnp.maximum(mx, jnp.max(s.reshape(tk // 8, 8, cols), axis=0))
        if do_pv:
            o_t = acc[0:HEAD_DIM, :] / acc[HEAD_DIM:HEAD_DIM + 1, :]
            o = o_t.T
            blk = jnp.concatenate([o[h * tq:(h + 1) * tq, :] for h in range(G)], axis=1)
            r0 = pl.multiple_of(((u - 2) >> 1) * tq, tq)
            o_ref[g_pv, pl.ds(r0, tq), :] = blk.astype(o_ref.dtype)
        if do_scores:
            return jnp.max(mx, axis=0, keepdims=True)
        return m_prev

    zero = jnp.int32(0)
    m = stage(zero, None, do_probs=False, do_pv=False)
    m = stage(zero + 1, m, do_pv=False)
    m = lax.fori_loop(2, n_units, stage, m)
    m = stage(zero + n_units, m, do_scores=False)
    stage(zero + n_units + 1, m, do_scores=False, do_probs=False)


def _attn_call(qt, k, vt):
    B, _, S = qt.shape
    G = N_Q_HEADS // N_KV_HEADS
    cols = G * ATTN_TQ
    gw = G * HEAD_DIM
    whole = lambda b: (b, 0, 0)
    return pl.pallas_call(
        _attn_kernel,
        grid=(B,),
        in_specs=[
            pl.BlockSpec((None, ATTN_WIDTH, S), whole),
            pl.BlockSpec((None, S, KV_WIDTH), whole),
            pl.BlockSpec((None, KV_WIDTH, S), whole),
        ],
        out_specs=pl.BlockSpec((None, N_KV_HEADS, S, gw), lambda b: (b, 0, 0, 0)),
        out_shape=jax.ShapeDtypeStruct((B, N_KV_HEADS, S, gw), BF16),
        scratch_shapes=[
            pltpu.VMEM((S, cols), F32),
            pltpu.VMEM((S, cols), BF16),
            pltpu.VMEM((N_KV_HEADS, HEAD_DIM + 16, S), BF16),
        ],
        compiler_params=pltpu.CompilerParams(
            dimension_semantics=("arbitrary",),
            vmem_limit_bytes=VMEM_LIMIT),
        name="attention",
    )(qt, k, vt)


def _hgrn_kernel(q_ref, i_ref, lff_ref, lfb_ref, g_ref, nw_ref, o_ref,
                 qm_ref, km_ref, qb_ref, kd_ref, dec_ref, ut_ref, st_ref, cur_ref):
    S = q_ref.shape[0]
    C = HG_CHUNK
    K = HG_EXPAND
    n_chunks = S // C
    nb = HG_BLOCK
    rows_b = nb * C
    r = lax.broadcasted_iota(jnp.int32, (C, C), 0)
    c = lax.broadcasted_iota(jnp.int32, (C, C), 1)
    lower = r >= c
    upper = r <= c
    tri2 = [jnp.concatenate([t, t], axis=1) for t in
            (jnp.where(lower, 1.0, 0.0).astype(BF16), jnp.where(upper, 1.0, 0.0).astype(BF16))]
    mid = (C // 2, C - 1 - C // 2)
    last = (C - 1, 0)

    def block_rows(t):
        return t * rows_b if isinstance(t, int) else pl.multiple_of(t * rows_b, rows_b)

    def prefix_sums(t):
        rows = pl.ds(block_rows(t), rows_b)
        lfs = (lff_ref[rows, :], lfb_ref[rows, :])
        b3s = []
        for d, lf in enumerate(lfs):
            hi = lf.astype(BF16)
            lo = (lf - hi.astype(F32)).astype(BF16)
            b3s.append(jnp.stack([
                jnp.dot(tri2[d],
                        jnp.concatenate([hi[j * C:(j + 1) * C], lo[j * C:(j + 1) * C]], axis=0),
                        preferred_element_type=F32)
                for j in range(nb)]))
        return lfs, b3s

    def state_updates(t):
        r0 = block_rows(t)
        for j in range(nb):
            cr = pl.ds(r0 + j * C, C)
            ivt = i_ref[cr, :].astype(F32).T.astype(BF16)
            ut_ref[t * nb + j] = jnp.dot(ivt, kd_ref[cr, :], preferred_element_type=F32)

    def scaled_operands(t, lfs, b3s):
        rows = pl.ds(block_rows(t), rows_b)
        q3 = q_ref[rows, :].astype(F32).reshape(nb, C, K)
        for d, (lf, b3) in enumerate(zip(lfs, b3s)):
            k3 = (1.0 - jnp.exp2(lf)).reshape(nb, C, K)
            bm = b3[:, mid[d]:mid[d] + 1, :]
            bl = b3[:, last[d]:last[d] + 1, :]
            e = jnp.exp2(b3 - bm)
            einv = jnp.exp2(bm - b3)
            qm = q3 * e
            km = k3 * einv
            qm_ref[d, rows, :] = qm.reshape(rows_b, K).astype(BF16)
            km_ref[d, rows, :] = km.reshape(rows_b, K).astype(BF16)
            qb_ref[rows, d * K:(d + 1) * K] = (qm * jnp.exp2(bm)).reshape(rows_b, K).astype(BF16)
            kd_ref[rows, d * K:(d + 1) * K] = (km * jnp.exp2(bl - bm)).reshape(rows_b, K).astype(BF16)
            c0 = t * nb if isinstance(t, int) else pl.multiple_of(t * nb, nb)
            dec_ref[d, pl.ds(c0, nb), :] = jnp.exp2(bl).reshape(nb, K)

    n_blocks = n_chunks // nb
    scaled_operands(0, *prefix_sums(0))

    def prep(t, _):
        sums = prefix_sums(t)
        state_updates(t - 1)
        scaled_operands(t, *sums)
        return 0

    lax.fori_loop(1, n_blocks, prep, 0)
    state_updates(n_blocks - 1)

    cur_ref[...] = jnp.zeros(cur_ref.shape, F32)

    def chain(n, _):
        m = n_chunks - 1 - n
        sf = cur_ref[0]
        st_ref[n, :, 0:K] = sf.astype(BF16)
        cur_ref[0] = dec_ref[0, pl.ds(n, 1), :] * sf + ut_ref[n, :, 0:K]
        sb = cur_ref[1]
        st_ref[m, :, K:2 * K] = sb.astype(BF16)
        cur_ref[1] = dec_ref[1, pl.ds(m, 1), :] * sb + ut_ref[m, :, K:2 * K]
        return 0

    lax.fori_loop(0, n_chunks, chain, 0)

    nw = nw_ref[...]
    nt_dims = (((1,), (1,)), ((), ()))

    no = HG_BLOCK_OUT

    def outp(t, _):
        r0 = pl.multiple_of(t * (no * C), no * C)
        crs = [pl.ds(r0 + j * C, C) for j in range(no)]
        a_f = [lax.dot_general(qm_ref[0, cr, :], km_ref[0, cr, :], nt_dims,
                               preferred_element_type=F32) for cr in crs]
        a_b = [lax.dot_general(qm_ref[1, cr, :], km_ref[1, cr, :], nt_dims,
                               preferred_element_type=F32) for cr in crs]
        a = [(jnp.where(lower, f, 0.0) + jnp.where(upper, b, 0.0)).astype(BF16)
             for f, b in zip(a_f, a_b)]
        outs = [jnp.dot(a[j], i_ref[cr, :], preferred_element_type=F32)
                + lax.dot_general(qb_ref[cr, :], st_ref[t * no + j], nt_dims,
                                  preferred_element_type=F32)
                for j, cr in enumerate(crs)]
        for cr, o in zip(crs, outs):
            ms = jnp.mean(o * o, axis=-1, keepdims=True)
            y = o * lax.rsqrt(ms + RMS_EPS) * nw
            o_ref[cr, :] = (y * g_ref[cr, :].astype(F32)).astype(o_ref.dtype)
        return 0

    lax.fori_loop(0, n_chunks // no, outp, 0)


def _hgrn_call(hq, hi, lf, hg, nw):
    B, S, _ = hq.shape
    n_chunks = S // HG_CHUNK
    K = HG_EXPAND
    blk = lambda b, h: (b, 0, h)
    return pl.pallas_call(
        _hgrn_kernel,
        grid=(B, HG_HEADS),
        in_specs=[
            pl.BlockSpec((None, S, K), blk),
            pl.BlockSpec((None, S, K), blk),
            pl.BlockSpec((None, S, K), blk),
            pl.BlockSpec((None, S, K), lambda b, h: (b, 0, HG_HEADS + h)),
            pl.BlockSpec((None, S, K), blk),
            pl.BlockSpec((1, K), lambda b, h: (0, 0)),
        ],
        out_specs=pl.BlockSpec((None, S, K), blk),
        out_shape=jax.ShapeDtypeStruct((B, S, HG_WIDTH), BF16),
        scratch_shapes=[
            pltpu.VMEM((2, S, K), BF16),
            pltpu.VMEM((2, S, K), BF16),
            pltpu.VMEM((S, 2 * K), BF16),
            pltpu.VMEM((S, 2 * K), BF16),
            pltpu.VMEM((2, n_chunks, K), F32),
            pltpu.VMEM((n_chunks, K, 2 * K), F32),
            pltpu.VMEM((n_chunks, K, 2 * K), BF16),
            pltpu.VMEM((2, K, K), F32),
        ],
        compiler_params=pltpu.CompilerParams(
            dimension_semantics=("arbitrary", "arbitrary"),
            vmem_limit_bytes=VMEM_LIMIT),
        name="hgrn2",
    )(hq, hi, lf, lf, hg, nw)


def _out_kernel(x_ref, attn_ref, ag_ref, hgb_ref, anw_ref, w_ref, lnw_ref, lnb_ref, o_ref):
    tm = x_ref.shape[0]
    for r0 in range(0, tm, OUT_SUB):
        rows = pl.ds(r0, OUT_SUB)
        a = jnp.concatenate([attn_ref[g, rows, :] for g in range(N_KV_HEADS)],
                            axis=1).astype(F32)
        ms = jnp.mean(a * a, axis=-1, keepdims=True)
        ab = a * lax.rsqrt(ms + RMS_EPS) * anw_ref[...] * ag_ref[rows, :].astype(F32)
        y = (jnp.dot(ab.astype(BF16), w_ref[:ATTN_WIDTH, :], preferred_element_type=F32)
             + jnp.dot(hgb_ref[rows, :], w_ref[ATTN_WIDTH:, :], preferred_element_type=F32))
        z = DEEPNORM_ALPHA * x_ref[rows, :] + y
        mu = jnp.mean(z, axis=-1, keepdims=True)
        zc = z - mu
        var = jnp.mean(zc * zc, axis=-1, keepdims=True)
        o_ref[rows, :] = zc * lax.rsqrt(var + LN_EPS) * lnw_ref[...] + lnb_ref[...]


def _out_call(x, attn, ag, hgb, anw, w_out_b, lnw, lnb):
    B, S, D = x.shape
    tm = OUT_TM
    row = lambda b, i: (b, i, 0)
    const2 = lambda b, i: (0, 0)
    return pl.pallas_call(
        _out_kernel,
        grid=(B, S // tm),
        in_specs=[
            pl.BlockSpec((None, tm, D), row),
            pl.BlockSpec((None, N_KV_HEADS, tm, ATTN_WIDTH // N_KV_HEADS), lambda b, i: (b, 0, i, 0)),
            pl.BlockSpec((None, tm, ATTN_WIDTH), row),
            pl.BlockSpec((None, tm, HG_WIDTH), row),
            pl.BlockSpec((1, ATTN_WIDTH), const2),
            pl.BlockSpec((ATTN_WIDTH + HG_WIDTH, D), const2),
            pl.BlockSpec((1, D), const2),
            pl.BlockSpec((1, D), const2),
        ],
        out_specs=pl.BlockSpec((None, tm, D), row),
        out_shape=jax.ShapeDtypeStruct((B, S, D), x.dtype),
        compiler_params=pltpu.CompilerParams(
            dimension_semantics=("arbitrary", "arbitrary"),
            vmem_limit_bytes=VMEM_LIMIT),
        name="out_proj_ln",
    )(x, attn, ag, hgb, anw, w_out_b, lnw, lnb)


def _rope_tables(seq_len):
    t = jnp.arange(seq_len, dtype=jnp.int32)
    row = (t // GRID_W).astype(F32)
    col = (t % GRID_W).astype(F32)
    inv = ROPE_THETA ** (-jnp.arange(0, AXIS_DIM, 2, dtype=F32) / AXIS_DIM)
    lane = jnp.arange(LANES, dtype=jnp.int32)
    d = lane % HEAD_DIM
    use_col = (d // AXIS_DIM) == 1
    dd = d % AXIS_DIM
    first = dd < (AXIS_DIM // 2)
    freq = inv[dd % (AXIS_DIM // 2)]
    pos = jnp.where(use_col[None, :], col[:, None], row[:, None])
    ang = pos * freq[None, :]
    cos_t = jnp.cos(ang)
    sin_t = jnp.where(first[None, :], -jnp.sin(ang), jnp.sin(ang))
    return cos_t, sin_t


def kernel(x, w_in, q_norm_w, k_norm_w, attn_norm_w, hg_lb_logits, hg_norm_w, w_out, ln_w, ln_b):
    B, S, D = x.shape
    assert w_in.shape == (DEPTH, D, IN_WIDTH) and DEPTH == 1
    cos_t, sin_t = _rope_tables(S)
    qw = jnp.tile(q_norm_w[0].astype(F32), N_Q_HEADS).reshape(1, ATTN_WIDTH)
    kw = jnp.tile(k_norm_w[0].astype(F32), N_KV_HEADS).reshape(1, KV_WIDTH)
    lbl = hg_lb_logits.astype(F32).reshape(2 * (DEPTH + 1), HG_WIDTH)
    hid = jnp.arange(ATTN_WIDTH, dtype=jnp.int32) // HEAD_DIM
    seg = jnp.where(hid[:, None] == hid[None, :], 1.0 / HEAD_DIM, 0.0).astype(BF16)

    qt, k, vt, ag, hq, hi, lf, hg = _proj_call(
        x, w_in[0].astype(BF16), cos_t, sin_t, qw, kw, lbl, seg)
    attn = _attn_call(qt, k, vt)
    hgb = _hgrn_call(hq, hi, lf, hg, hg_norm_w[0].astype(F32).reshape(1, HG_EXPAND))
    return _out_call(
        x, attn, ag, hgb,
        attn_norm_w[0].astype(F32).reshape(1, ATTN_WIDTH),
        w_out[0].astype(BF16),
        ln_w[0].astype(F32).reshape(1, D),
        ln_b[0].astype(F32).reshape(1, D))
```

```python
import functools

import jax
import jax.numpy as jnp
from jax import lax
from jax.experimental import pallas as pl
from jax.experimental.pallas import tpu as pltpu

F32 = jnp.float32
BF16 = jnp.bfloat16

GRID_W = 64
ATTN_WIDTH = 512
HEAD_DIM = 64
N_Q_HEADS = 8
N_KV_HEADS = 2
KV_WIDTH = 128
AXIS_DIM = 32
ROPE_THETA = 10000.0
HG_WIDTH = 512
HG_EXPAND = 128
HG_HEADS = 4
HG_CHUNK = 64
RMS_EPS = 1e-6
LN_EPS = 1e-5
DEPTH = 1
DEEPNORM_ALPHA = (2 * DEPTH) ** 0.25
IN_WIDTH = 3840
LOG2E = 1.4426950408889634
Q_SCALE = HEAD_DIM ** -0.5 * LOG2E

LANES = 128
VMEM_LIMIT = 56 * 1024 * 1024

PROJ_TM = 512
ATTN_TQ = 128
ATTN_TK = 512
HG_BLOCK = 8
HG_BLOCK_OUT = 32
OUT_TM = 1024


def _exact_zero(x):
    bits = lax.bitcast_convert_type(x, jnp.uint32)
    return ((bits >> 16) >> 16).astype(F32)


def _sigmoid(x):
    return 1.0 / (1.0 + jnp.exp(-x))


def _proj_kernel(x_ref, w_ref, cos_ref, sin_ref, qw_ref, kw_ref, lbl_ref, seg_ref,
                 qt_ref, k_ref, vt_ref, ag_ref, hq_ref, hi_ref, lf_ref, hg_ref):
    xb = x_ref[...].astype(BF16)
    tm = xb.shape[0]

    def proj(a, b):
        return jnp.dot(xb, w_ref[:, a:b], preferred_element_type=F32)

    cosv = cos_ref[...]
    sinv = sin_ref[...]
    lane = lax.broadcasted_iota(jnp.int32, (tm, LANES), 1)
    first_half = (lane % AXIS_DIM) < (AXIS_DIM // 2)

    def norm_rope(a, w_row, seg, scale):
        ms = jnp.dot((a * a).astype(BF16), seg, preferred_element_type=F32)
        y = a * lax.rsqrt(ms + RMS_EPS) * w_row
        outs = []
        for c in range(a.shape[1] // LANES):
            yc = y[:, c * LANES:(c + 1) * LANES]
            partner = jnp.where(first_half,
                                pltpu.roll(yc, LANES - AXIS_DIM // 2, 1),
                                pltpu.roll(yc, AXIS_DIM // 2, 1))
            outs.append((yc * cosv + partner * sinv) * scale)
        return outs

    aq = proj(0, ATTN_WIDTH)
    qs = norm_rope(aq, qw_ref[...], seg_ref[...], Q_SCALE)
    for c, qc in enumerate(qs):
        qt_ref[c * LANES:(c + 1) * LANES, :] = qc.T.astype(BF16)
    akv = proj(ATTN_WIDTH, ATTN_WIDTH + 2 * KV_WIDTH)
    kk = norm_rope(akv[:, :KV_WIDTH], kw_ref[...], seg_ref[:KV_WIDTH, :KV_WIDTH], 1.0)[0]
    k_ref[...] = kk.astype(BF16)
    vt_ref[...] = akv[:, KV_WIDTH:].T.astype(BF16)

    base = ATTN_WIDTH + 2 * KV_WIDTH
    ag = proj(base, base + ATTN_WIDTH)
    ag_ref[...] = (ag * _sigmoid(ag)).astype(BF16)
    base += ATTN_WIDTH
    hq = proj(base, base + HG_WIDTH)
    hq_ref[...] = (hq * _sigmoid(hq) * (HG_EXPAND ** -0.5)).astype(BF16)
    base += HG_WIDTH
    hi_ref[...] = proj(base, base + HG_WIDTH).astype(BF16)
    base += HG_WIDTH
    lbl = lbl_ref[...]
    for d in range(2):
        l0 = lbl[2 * d:2 * d + 1, :]
        l1 = lbl[2 * d + 1:2 * d + 2, :]
        mx = jnp.maximum(l0, l1)
        e0 = jnp.exp(l0 - mx)
        e1 = jnp.exp(l1 - mx)
        lb = e0 / (e0 + e1)
        z = proj(base, base + HG_WIDTH)
        lf_ref[:, d * HG_WIDTH:(d + 1) * HG_WIDTH] = jnp.log(lb + (1.0 - lb) * _sigmoid(z)) * LOG2E
        base += HG_WIDTH
    hg = proj(base, base + HG_WIDTH)
    hg_ref[...] = (hg * _sigmoid(hg)).astype(BF16)


def _proj_call(x, w_in_b, cos_t, sin_t, qw, kw, lbl, seg):
    B, S, D = x.shape
    tm = PROJ_TM
    nt = S // tm
    row = lambda b, i: (b, i, 0)
    const2 = lambda b, i: (0, 0)
    out_shape = (
        jax.ShapeDtypeStruct((B, ATTN_WIDTH, S), BF16),
        jax.ShapeDtypeStruct((B, S, KV_WIDTH), BF16),
        jax.ShapeDtypeStruct((B, KV_WIDTH, S), BF16),
        jax.ShapeDtypeStruct((B, S, ATTN_WIDTH), BF16),
        jax.ShapeDtypeStruct((B, S, HG_WIDTH), BF16),
        jax.ShapeDtypeStruct((B, S, HG_WIDTH), BF16),
        jax.ShapeDtypeStruct((B, S, 2 * HG_WIDTH), F32),
        jax.ShapeDtypeStruct((B, S, HG_WIDTH), BF16),
    )
    in_specs = [
        pl.BlockSpec((None, tm, D), row),
        pl.BlockSpec((D, IN_WIDTH), const2),
        pl.BlockSpec((tm, LANES), lambda b, i: (i, 0)),
        pl.BlockSpec((tm, LANES), lambda b, i: (i, 0)),
        pl.BlockSpec((1, ATTN_WIDTH), const2),
        pl.BlockSpec((1, KV_WIDTH), const2),
        pl.BlockSpec((4, HG_WIDTH), const2),
        pl.BlockSpec((ATTN_WIDTH, ATTN_WIDTH), const2),
    ]
    out_specs = (
        pl.BlockSpec((None, ATTN_WIDTH, tm), lambda b, i: (b, 0, i)),
        pl.BlockSpec((None, tm, KV_WIDTH), row),
        pl.BlockSpec((None, KV_WIDTH, tm), lambda b, i: (b, 0, i)),
        pl.BlockSpec((None, tm, ATTN_WIDTH), row),
        pl.BlockSpec((None, tm, HG_WIDTH), row),
        pl.BlockSpec((None, tm, HG_WIDTH), row),
        pl.BlockSpec((None, tm, 2 * HG_WIDTH), row),
        pl.BlockSpec((None, tm, HG_WIDTH), row),
    )
    return pl.pallas_call(
        _proj_kernel,
        grid=(B, nt),
        in_specs=in_specs,
        out_specs=out_specs,
        out_shape=out_shape,
        compiler_params=pltpu.CompilerParams(
            dimension_semantics=("arbitrary", "arbitrary"),
            vmem_limit_bytes=VMEM_LIMIT),
        name="in_proj",
    )(x, w_in_b, cos_t, sin_t, qw, kw, lbl, seg)


def _attn_kernel(qt_ref, k_ref, vt_ref, o_ref, s_ref, p_ref, vx_ref):
    S = k_ref.shape[0]
    G = N_Q_HEADS // N_KV_HEADS
    tq, tk = ATTN_TQ, ATTN_TK
    cols = G * tq
    n_units = N_KV_HEADS * (S // tq)
    n_c = S // tk
    vrows = vx_ref.shape[1]

    for g in range(N_KV_HEADS):
        vx_ref[g, 0:HEAD_DIM, :] = vt_ref[g * HEAD_DIM:(g + 1) * HEAD_DIM, :]
        vx_ref[g, HEAD_DIM:vrows, :] = jnp.ones((vrows - HEAD_DIM, S), BF16)

    row_group = lax.broadcasted_iota(jnp.int32, (N_KV_HEADS * HEAD_DIM, cols), 0) // HEAD_DIM

    def q_weights(u):
        i, g = u >> 1, u & 1
        c0 = pl.multiple_of(i * tq, tq)
        qs = jnp.concatenate(
            [qt_ref[pl.ds(pl.multiple_of((G * g + h) * HEAD_DIM, HEAD_DIM), HEAD_DIM),
                    pl.ds(c0, tq)] for h in range(G)], axis=1)
        return jnp.where(row_group == g, jnp.concatenate([qs, qs], axis=0), 0.0).astype(BF16)

    def stage(u, m_prev, do_scores=True, do_probs=True, do_pv=True):
        if do_scores:
            w = q_weights(u)
            mx = jnp.full((8, cols), -jnp.inf, F32)
        if do_pv:
            g_pv = (u - 2) & 1
            acc = jnp.zeros((vrows, cols), F32)
        for c in range(n_c):
            rows = pl.ds(c * tk, tk)
            m_c = m_prev
            if do_pv:
                acc = acc + jnp.dot(vx_ref[g_pv, :, rows], p_ref[rows, :],
                                    preferred_element_type=F32)
                if do_probs:
                    guard = sum(_exact_zero(acc[0:8, t * 2 * LANES:(t * 2 + 1) * LANES])
                                for t in range(cols // (2 * LANES)))
                    m_c = m_prev + jnp.tile(guard[0:1, :], (1, cols // LANES))
            if do_probs:
                p_ref[rows, :] = jnp.exp2(s_ref[rows, :] - m_c).astype(BF16)
            if do_scores:
                s = jnp.dot(k_ref[rows, :], w, preferred_element_type=F32)
                s_ref[rows, :] = s
                mx = jnp.maximum(mx, jnp.max(s.reshape(tk // 8, 8, cols), axis=0))
        if do_pv:
            o_t = acc[0:HEAD_DIM, :] / acc[HEAD_DIM:HEAD_DIM + 1, :]
            o = o_t.T
            blk = jnp.concatenate([o[h * tq:(h + 1) * tq, :] for h in range(G)], axis=1)
            r0 = pl.multiple_of(((u - 2) >> 1) * tq, tq)
            o_ref[g_pv, pl.ds(r0, tq), :] = blk.astype(o_ref.dtype)
        if do_scores:
            return jnp.max(mx, axis=0, keepdims=True)
        return m_prev

    zero = jnp.int32(0)
    m = stage(zero, None, do_probs=False, do_pv=False)
    m = stage(zero + 1, m, do_pv=False)
    m = lax.fori_loop(2, n_units, stage, m)
    m = stage(zero + n_units, m, do_scores=False)
    stage(zero + n_units + 1, m, do_scores=False, do_probs=False)


def _attn_call(qt, k, vt):
    B, _, S = qt.shape
    G = N_Q_HEADS // N_KV_HEADS
    cols = G * ATTN_TQ
    gw = G * HEAD_DIM
    whole = lambda b: (b, 0, 0)
    return pl.pallas_call(
        _attn_kernel,
        grid=(B,),
        in_specs=[
            pl.BlockSpec((None, ATTN_WIDTH, S), whole),
            pl.BlockSpec((None, S, KV_WIDTH), whole),
            pl.BlockSpec((None, KV_WIDTH, S), whole),
        ],
        out_specs=pl.BlockSpec((None, N_KV_HEADS, S, gw), lambda b: (b, 0, 0, 0)),
        out_shape=jax.ShapeDtypeStruct((B, N_KV_HEADS, S, gw), BF16),
        scratch_shapes=[
            pltpu.VMEM((S, cols), F32),
            pltpu.VMEM((S, cols), BF16),
            pltpu.VMEM((N_KV_HEADS, HEAD_DIM + 16, S), BF16),
        ],
        compiler_params=pltpu.CompilerParams(
            dimension_semantics=("arbitrary",),
            vmem_limit_bytes=VMEM_LIMIT),
        name="attention",
    )(qt, k, vt)


def _hgrn_kernel(q_ref, i_ref, lff_ref, lfb_ref, g_ref, nw_ref, o_ref,
                 qm_ref, km_ref, qb_ref, kd_ref, dec_ref, ut_ref, st_ref):
    S = q_ref.shape[0]
    C = HG_CHUNK
    K = HG_EXPAND
    n_chunks = S // C
    nb = HG_BLOCK
    rows_b = nb * C
    r = lax.broadcasted_iota(jnp.int32, (C, C), 0)
    c = lax.broadcasted_iota(jnp.int32, (C, C), 1)
    lower = r >= c
    upper = r <= c
    tri2 = [jnp.concatenate([t, t], axis=1) for t in
            (jnp.where(lower, 1.0, 0.0).astype(BF16), jnp.where(upper, 1.0, 0.0).astype(BF16))]
    mid = (C // 2, C - 1 - C // 2)
    last = (C - 1, 0)

    def block_rows(t):
        return t * rows_b if isinstance(t, int) else pl.multiple_of(t * rows_b, rows_b)

    def prefix_sums(t):
        rows = pl.ds(block_rows(t), rows_b)
        lfs = (lff_ref[rows, :], lfb_ref[rows, :])
        b3s = []
        for d, lf in enumerate(lfs):
            hi = lf.astype(BF16)
            lo = (lf - hi.astype(F32)).astype(BF16)
            b3s.append(jnp.stack([
                jnp.dot(tri2[d],
                        jnp.concatenate([hi[j * C:(j + 1) * C], lo[j * C:(j + 1) * C]], axis=0),
                        preferred_element_type=F32)
                for j in range(nb)]))
        return lfs, b3s

    def state_updates(t):
        r0 = block_rows(t)
        for j in range(nb):
            cr = pl.ds(r0 + j * C, C)
            ivt = i_ref[cr, :].astype(F32).T.astype(BF16)
            ut_ref[t * nb + j] = jnp.dot(ivt, kd_ref[cr, :], preferred_element_type=F32)

    def scaled_operands(t, lfs, b3s):
        rows = pl.ds(block_rows(t), rows_b)
        q3 = q_ref[rows, :].astype(F32).reshape(nb, C, K)
        for d, (lf, b3) in enumerate(zip(lfs, b3s)):
            k3 = (1.0 - jnp.exp2(lf)).reshape(nb, C, K)
            bm = b3[:, mid[d]:mid[d] + 1, :]
            bl = b3[:, last[d]:last[d] + 1, :]
            e = jnp.exp2(b3 - bm)
            einv = jnp.exp2(bm - b3)
            qm = q3 * e
            km = k3 * einv
            qm_ref[d, rows, :] = qm.reshape(rows_b, K).astype(BF16)
            km_ref[d, rows, :] = km.reshape(rows_b, K).astype(BF16)
            qb_ref[rows, d * K:(d + 1) * K] = (qm * jnp.exp2(bm)).reshape(rows_b, K).astype(BF16)
            kd_ref[rows, d * K:(d + 1) * K] = (km * jnp.exp2(bl - bm)).reshape(rows_b, K).astype(BF16)
            c0 = t * nb if isinstance(t, int) else pl.multiple_of(t * nb, nb)
            dec_ref[d, pl.ds(c0, nb), :] = jnp.exp2(bl).reshape(nb, K)

    n_blocks = n_chunks // nb
    scaled_operands(0, *prefix_sums(0))

    def prep(t, _):
        sums = prefix_sums(t)
        state_updates(t - 1)
        scaled_operands(t, *sums)
        return 0

    lax.fori_loop(1, n_blocks, prep, 0)
    state_updates(n_blocks - 1)

    def chain(n, states):
        sf, sb = states
        m = n_chunks - 1 - n
        st_ref[n, :, 0:K] = sf.astype(BF16)
        st_ref[m, :, K:2 * K] = sb.astype(BF16)
        return (dec_ref[0, pl.ds(n, 1), :] * sf + ut_ref[n, :, 0:K],
                dec_ref[1, pl.ds(m, 1), :] * sb + ut_ref[m, :, K:2 * K])

    zero_state = jnp.zeros((K, K), F32)
    lax.fori_loop(0, n_chunks, chain, (zero_state, zero_state))

    nw = nw_ref[...]
    nt_dims = (((1,), (1,)), ((), ()))

    no = HG_BLOCK_OUT

    def outp(t, _):
        r0 = pl.multiple_of(t * (no * C), no * C)
        crs = [pl.ds(r0 + j * C, C) for j in range(no)]
        a_f = [lax.dot_general(qm_ref[0, cr, :], km_ref[0, cr, :], nt_dims,
                               preferred_element_type=F32) for cr in crs]
        a_b = [lax.dot_general(qm_ref[1, cr, :], km_ref[1, cr, :], nt_dims,
                               preferred_element_type=F32) for cr in crs]
        a = [(jnp.where(lower, f, 0.0) + jnp.where(upper, b, 0.0)).astype(BF16)
             for f, b in zip(a_f, a_b)]
        outs = [jnp.dot(a[j], i_ref[cr, :], preferred_element_type=F32)
                + lax.dot_general(qb_ref[cr, :], st_ref[t * no + j], nt_dims,
                                  preferred_element_type=F32)
                for j, cr in enumerate(crs)]
        for cr, o in zip(crs, outs):
            ms = jnp.mean(o * o, axis=-1, keepdims=True)
            y = o * lax.rsqrt(ms + RMS_EPS) * nw
            o_ref[cr, :] = (y * g_ref[cr, :].astype(F32)).astype(o_ref.dtype)
        return 0

    lax.fori_loop(0, n_chunks // no, outp, 0)


def _hgrn_call(hq, hi, lf, hg, nw):
    B, S, _ = hq.shape
    n_chunks = S // HG_CHUNK
    K = HG_EXPAND
    blk = lambda b, h: (b, 0, h)
    return pl.pallas_call(
        _hgrn_kernel,
        grid=(B, HG_HEADS),
        in_specs=[
            pl.BlockSpec((None, S, K), blk),
            pl.BlockSpec((None, S, K), blk),
            pl.BlockSpec((None, S, K), blk),
            pl.BlockSpec((None, S, K), lambda b, h: (b, 0, HG_HEADS + h)),
            pl.BlockSpec((None, S, K), blk),
            pl.BlockSpec((1, K), lambda b, h: (0, 0)),
        ],
        out_specs=pl.BlockSpec((None, S, K), blk),
        out_shape=jax.ShapeDtypeStruct((B, S, HG_WIDTH), BF16),
        scratch_shapes=[
            pltpu.VMEM((2, S, K), BF16),
            pltpu.VMEM((2, S, K), BF16),
            pltpu.VMEM((S, 2 * K), BF16),
            pltpu.VMEM((S, 2 * K), BF16),
            pltpu.VMEM((2, n_chunks, K), F32),
            pltpu.VMEM((n_chunks, K, 2 * K), F32),
            pltpu.VMEM((n_chunks, K, 2 * K), BF16),
        ],
        compiler_params=pltpu.CompilerParams(
            dimension_semantics=("arbitrary", "arbitrary"),
            vmem_limit_bytes=VMEM_LIMIT),
        name="hgrn2",
    )(hq, hi, lf, lf, hg, nw)


def _out_kernel(x_ref, attn_ref, ag_ref, hgb_ref, anw_ref, w_ref, lnw_ref, lnb_ref, o_ref):
    a = jnp.concatenate([attn_ref[g] for g in range(N_KV_HEADS)], axis=1).astype(F32)
    ms = jnp.mean(a * a, axis=-1, keepdims=True)
    ab = a * lax.rsqrt(ms + RMS_EPS) * anw_ref[...] * ag_ref[...].astype(F32)
    y = (jnp.dot(ab.astype(BF16), w_ref[:ATTN_WIDTH, :], preferred_element_type=F32)
         + jnp.dot(hgb_ref[...], w_ref[ATTN_WIDTH:, :], preferred_element_type=F32))
    z = DEEPNORM_ALPHA * x_ref[...] + y
    mu = jnp.mean(z, axis=-1, keepdims=True)
    zc = z - mu
    var = jnp.mean(zc * zc, axis=-1, keepdims=True)
    o_ref[...] = zc * lax.rsqrt(var + LN_EPS) * lnw_ref[...] + lnb_ref[...]


def _out_call(x, attn, ag, hgb, anw, w_out_b, lnw, lnb):
    B, S, D = x.shape
    tm = OUT_TM
    row = lambda b, i: (b, i, 0)
    const2 = lambda b, i: (0, 0)
    return pl.pallas_call(
        _out_kernel,
        grid=(B, S // tm),
        in_specs=[
            pl.BlockSpec((None, tm, D), row),
            pl.BlockSpec((None, N_KV_HEADS, tm, ATTN_WIDTH // N_KV_HEADS), lambda b, i: (b, 0, i, 0)),
            pl.BlockSpec((None, tm, ATTN_WIDTH), row),
            pl.BlockSpec((None, tm, HG_WIDTH), row),
            pl.BlockSpec((1, ATTN_WIDTH), const2),
            pl.BlockSpec((ATTN_WIDTH + HG_WIDTH, D), const2),
            pl.BlockSpec((1, D), const2),
            pl.BlockSpec((1, D), const2),
        ],
        out_specs=pl.BlockSpec((None, tm, D), row),
        out_shape=jax.ShapeDtypeStruct((B, S, D), x.dtype),
        compiler_params=pltpu.CompilerParams(
            dimension_semantics=("arbitrary", "arbitrary"),
            vmem_limit_bytes=VMEM_LIMIT),
        name="out_proj_ln",
    )(x, attn, ag, hgb, anw, w_out_b, lnw, lnb)


def _rope_tables(seq_len):
    t = jnp.arange(seq_len, dtype=jnp.int32)
    row = (t // GRID_W).astype(F32)
    col = (t % GRID_W).astype(F32)
    inv = ROPE_THETA ** (-jnp.arange(0, AXIS_DIM, 2, dtype=F32) / AXIS_DIM)
    lane = jnp.arange(LANES, dtype=jnp.int32)
    d = lane % HEAD_DIM
    use_col = (d // AXIS_DIM) == 1
    dd = d % AXIS_DIM
    first = dd < (AXIS_DIM // 2)
    freq = inv[dd % (AXIS_DIM // 2)]
    pos = jnp.where(use_col[None, :], col[:, None], row[:, None])
    ang = pos * freq[None, :]
    cos_t = jnp.cos(ang)
    sin_t = jnp.where(first[None, :], -jnp.sin(ang), jnp.sin(ang))
    return cos_t, sin_t


def kernel(x, w_in, q_norm_w, k_norm_w, attn_norm_w, hg_lb_logits, hg_norm_w, w_out, ln_w, ln_b):
    B, S, D = x.shape
    assert w_in.shape == (DEPTH, D, IN_WIDTH) and DEPTH == 1
    cos_t, sin_t = _rope_tables(S)
    qw = jnp.tile(q_norm_w[0].astype(F32), N_Q_HEADS).reshape(1, ATTN_WIDTH)
    kw = jnp.tile(k_norm_w[0].astype(F32), N_KV_HEADS).reshape(1, KV_WIDTH)
    lbl = hg_lb_logits.astype(F32).reshape(2 * (DEPTH + 1), HG_WIDTH)
    hid = jnp.arange(ATTN_WIDTH, dtype=jnp.int32) // HEAD_DIM
    seg = jnp.where(hid[:, None] == hid[None, :], 1.0 / HEAD_DIM, 0.0).astype(BF16)

    qt, k, vt, ag, hq, hi, lf, hg = _proj_call(
        x, w_in[0].astype(BF16), cos_t, sin_t, qw, kw, lbl, seg)
    attn = _attn_call(qt, k, vt)
    hgb = _hgrn_call(hq, hi, lf, hg, hg_norm_w[0].astype(F32).reshape(1, HG_EXPAND))
    return _out_call(
        x, attn, ag, hgb,
        attn_norm_w[0].astype(F32).reshape(1, ATTN_WIDTH),
        w_out[0].astype(BF16),
        ln_w[0].astype(F32).reshape(1, D),
        ln_b[0].astype(F32).reshape(1, D))
```

```python
import functools

import jax
import jax.numpy as jnp
from jax import lax
from jax.experimental import pallas as pl
from jax.experimental.pallas import tpu as pltpu

F32 = jnp.float32
BF16 = jnp.bfloat16

GRID_W = 64
ATTN_WIDTH = 512
HEAD_DIM = 64
N_Q_HEADS = 8
N_KV_HEADS = 2
KV_WIDTH = 128
AXIS_DIM = 32
ROPE_THETA = 10000.0
HG_WIDTH = 512
HG_EXPAND = 128
HG_HEADS = 4
HG_CHUNK = 64
RMS_EPS = 1e-6
LN_EPS = 1e-5
DEPTH = 1
DEEPNORM_ALPHA = (2 * DEPTH) ** 0.25
IN_WIDTH = 3840
LOG2E = 1.4426950408889634
Q_SCALE = HEAD_DIM ** -0.5 * LOG2E

LANES = 128
VMEM_LIMIT = 56 * 1024 * 1024

PROJ_TM = 512
ATTN_TQ = 128
ATTN_TK = 1024
HG_BLOCK = 8
HG_BLOCK_OUT = 32
OUT_TM = 1024


def _exact_zero(x):
    bits = lax.bitcast_convert_type(x, jnp.uint32)
    return ((bits >> 16) >> 16).astype(F32)


def _sigmoid(x):
    return 1.0 / (1.0 + jnp.exp(-x))


def _proj_kernel(x_ref, w_ref, cos_ref, sin_ref, qw_ref, kw_ref, lbl_ref, seg_ref,
                 qt_ref, k_ref, vt_ref, ag_ref, hq_ref, hi_ref, lf_ref, hg_ref):
    xb = x_ref[...].astype(BF16)
    tm = xb.shape[0]

    def proj(a, b):
        return jnp.dot(xb, w_ref[:, a:b], preferred_element_type=F32)

    cosv = cos_ref[...]
    sinv = sin_ref[...]
    lane = lax.broadcasted_iota(jnp.int32, (tm, LANES), 1)
    first_half = (lane % AXIS_DIM) < (AXIS_DIM // 2)

    def norm_rope(a, w_row, seg, scale):
        ms = jnp.dot((a * a).astype(BF16), seg, preferred_element_type=F32)
        y = a * lax.rsqrt(ms + RMS_EPS) * w_row
        outs = []
        for c in range(a.shape[1] // LANES):
            yc = y[:, c * LANES:(c + 1) * LANES]
            partner = jnp.where(first_half,
                                pltpu.roll(yc, LANES - AXIS_DIM // 2, 1),
                                pltpu.roll(yc, AXIS_DIM // 2, 1))
            outs.append((yc * cosv + partner * sinv) * scale)
        return outs

    aq = proj(0, ATTN_WIDTH)
    qs = norm_rope(aq, qw_ref[...], seg_ref[...], Q_SCALE)
    for c, qc in enumerate(qs):
        qt_ref[c * LANES:(c + 1) * LANES, :] = qc.T.astype(BF16)
    akv = proj(ATTN_WIDTH, ATTN_WIDTH + 2 * KV_WIDTH)
    kk = norm_rope(akv[:, :KV_WIDTH], kw_ref[...], seg_ref[:KV_WIDTH, :KV_WIDTH], 1.0)[0]
    k_ref[...] = kk.astype(BF16)
    vt_ref[...] = akv[:, KV_WIDTH:].T.astype(BF16)

    base = ATTN_WIDTH + 2 * KV_WIDTH
    ag = proj(base, base + ATTN_WIDTH)
    ag_ref[...] = (ag * _sigmoid(ag)).astype(BF16)
    base += ATTN_WIDTH
    hq = proj(base, base + HG_WIDTH)
    hq_ref[...] = (hq * _sigmoid(hq) * (HG_EXPAND ** -0.5)).astype(BF16)
    base += HG_WIDTH
    hi_ref[...] = proj(base, base + HG_WIDTH).astype(BF16)
    base += HG_WIDTH
    lbl = lbl_ref[...]
    for d in range(2):
        l0 = lbl[2 * d:2 * d + 1, :]
        l1 = lbl[2 * d + 1:2 * d + 2, :]
        mx = jnp.maximum(l0, l1)
        e0 = jnp.exp(l0 - mx)
        e1 = jnp.exp(l1 - mx)
        lb = e0 / (e0 + e1)
        z = proj(base, base + HG_WIDTH)
        lf_ref[:, d * HG_WIDTH:(d + 1) * HG_WIDTH] = jnp.log(lb + (1.0 - lb) * _sigmoid(z)) * LOG2E
        base += HG_WIDTH
    hg = proj(base, base + HG_WIDTH)
    hg_ref[...] = (hg * _sigmoid(hg)).astype(BF16)


def _proj_call(x, w_in_b, cos_t, sin_t, qw, kw, lbl, seg):
    B, S, D = x.shape
    tm = PROJ_TM
    nt = S // tm
    row = lambda b, i: (b, i, 0)
    const2 = lambda b, i: (0, 0)
    out_shape = (
        jax.ShapeDtypeStruct((B, ATTN_WIDTH, S), BF16),
        jax.ShapeDtypeStruct((B, S, KV_WIDTH), BF16),
        jax.ShapeDtypeStruct((B, KV_WIDTH, S), BF16),
        jax.ShapeDtypeStruct((B, S, ATTN_WIDTH), BF16),
        jax.ShapeDtypeStruct((B, S, HG_WIDTH), BF16),
        jax.ShapeDtypeStruct((B, S, HG_WIDTH), BF16),
        jax.ShapeDtypeStruct((B, S, 2 * HG_WIDTH), F32),
        jax.ShapeDtypeStruct((B, S, HG_WIDTH), BF16),
    )
    in_specs = [
        pl.BlockSpec((None, tm, D), row),
        pl.BlockSpec((D, IN_WIDTH), const2),
        pl.BlockSpec((tm, LANES), lambda b, i: (i, 0)),
        pl.BlockSpec((tm, LANES), lambda b, i: (i, 0)),
        pl.BlockSpec((1, ATTN_WIDTH), const2),
        pl.BlockSpec((1, KV_WIDTH), const2),
        pl.BlockSpec((4, HG_WIDTH), const2),
        pl.BlockSpec((ATTN_WIDTH, ATTN_WIDTH), const2),
    ]
    out_specs = (
        pl.BlockSpec((None, ATTN_WIDTH, tm), lambda b, i: (b, 0, i)),
        pl.BlockSpec((None, tm, KV_WIDTH), row),
        pl.BlockSpec((None, KV_WIDTH, tm), lambda b, i: (b, 0, i)),
        pl.BlockSpec((None, tm, ATTN_WIDTH), row),
        pl.BlockSpec((None, tm, HG_WIDTH), row),
        pl.BlockSpec((None, tm, HG_WIDTH), row),
        pl.BlockSpec((None, tm, 2 * HG_WIDTH), row),
        pl.BlockSpec((None, tm, HG_WIDTH), row),
    )
    return pl.pallas_call(
        _proj_kernel,
        grid=(B, nt),
        in_specs=in_specs,
        out_specs=out_specs,
        out_shape=out_shape,
        compiler_params=pltpu.CompilerParams(
            dimension_semantics=("arbitrary", "arbitrary"),
            vmem_limit_bytes=VMEM_LIMIT),
        name="in_proj",
    )(x, w_in_b, cos_t, sin_t, qw, kw, lbl, seg)


def _attn_kernel(qt_ref, k_ref, vt_ref, o_ref, s_ref, p_ref, vx_ref):
    S = k_ref.shape[0]
    G = N_Q_HEADS // N_KV_HEADS
    tq, tk = ATTN_TQ, ATTN_TK
    cols = G * tq
    n_units = N_KV_HEADS * (S // tq)
    n_c = S // tk
    vrows = vx_ref.shape[1]

    for g in range(N_KV_HEADS):
        vx_ref[g, 0:HEAD_DIM, :] = vt_ref[g * HEAD_DIM:(g + 1) * HEAD_DIM, :]
        vx_ref[g, HEAD_DIM:vrows, :] = jnp.ones((vrows - HEAD_DIM, S), BF16)

    row_group = lax.broadcasted_iota(jnp.int32, (N_KV_HEADS * HEAD_DIM, cols), 0) // HEAD_DIM

    def q_weights(u):
        i, g = u >> 1, u & 1
        c0 = pl.multiple_of(i * tq, tq)
        qs = jnp.concatenate(
            [qt_ref[pl.ds(pl.multiple_of((G * g + h) * HEAD_DIM, HEAD_DIM), HEAD_DIM),
                    pl.ds(c0, tq)] for h in range(G)], axis=1)
        return jnp.where(row_group == g, jnp.concatenate([qs, qs], axis=0), 0.0).astype(BF16)

    def stage(u, m_prev, do_scores=True, do_probs=True, do_pv=True):
        if do_scores:
            w = q_weights(u)
            mx = jnp.full((8, cols), -jnp.inf, F32)
        if do_pv:
            g_pv = (u - 2) & 1
            acc = jnp.zeros((vrows, cols), F32)
        for c in range(n_c):
            rows = pl.ds(c * tk, tk)
            m_c = m_prev
            if do_pv:
                acc = acc + jnp.dot(vx_ref[g_pv, :, rows], p_ref[rows, :],
                                    preferred_element_type=F32)
                if do_probs:
                    guard = sum(_exact_zero(acc[0:8, t * 2 * LANES:(t * 2 + 1) * LANES])
                                for t in range(cols // (2 * LANES)))
                    m_c = m_prev + jnp.tile(guard[0:1, :], (1, cols // LANES))
            if do_probs:
                p_ref[rows, :] = jnp.exp2(s_ref[rows, :] - m_c).astype(BF16)
            if do_scores:
                s = jnp.dot(k_ref[rows, :], w, preferred_element_type=F32)
                s_ref[rows, :] = s
                mx = jnp.maximum(mx, jnp.max(s.reshape(tk // 8, 8, cols), axis=0))
        if do_pv:
            o_t = acc[0:HEAD_DIM, :] / acc[HEAD_DIM:HEAD_DIM + 1, :]
            o = o_t.T
            blk = jnp.concatenate([o[h * tq:(h + 1) * tq, :] for h in range(G)], axis=1)
            r0 = pl.multiple_of(((u - 2) >> 1) * tq, tq)
            o_ref[g_pv, pl.ds(r0, tq), :] = blk.astype(o_ref.dtype)
        if do_scores:
            return jnp.max(mx, axis=0, keepdims=True)
        return m_prev

    zero = jnp.int32(0)
    m = stage(zero, None, do_probs=False, do_pv=False)
    m = stage(zero + 1, m, do_pv=False)
    m = lax.fori_loop(2, n_units, stage, m)
    m = stage(zero + n_units, m, do_scores=False)
    stage(zero + n_units + 1, m, do_scores=False, do_probs=False)


def _attn_call(qt, k, vt):
    B, _, S = qt.shape
    G = N_Q_HEADS // N_KV_HEADS
    cols = G * ATTN_TQ
    gw = G * HEAD_DIM
    whole = lambda b: (b, 0, 0)
    return pl.pallas_call(
        _attn_kernel,
        grid=(B,),
        in_specs=[
            pl.BlockSpec((None, ATTN_WIDTH, S), whole),
            pl.BlockSpec((None, S, KV_WIDTH), whole),
            pl.BlockSpec((None, KV_WIDTH, S), whole),
        ],
        out_specs=pl.BlockSpec((None, N_KV_HEADS, S, gw), lambda b: (b, 0, 0, 0)),
        out_shape=jax.ShapeDtypeStruct((B, N_KV_HEADS, S, gw), BF16),
        scratch_shapes=[
            pltpu.VMEM((S, cols), F32),
            pltpu.VMEM((S, cols), BF16),
            pltpu.VMEM((N_KV_HEADS, HEAD_DIM + 16, S), BF16),
        ],
        compiler_params=pltpu.CompilerParams(
            dimension_semantics=("arbitrary",),
            vmem_limit_bytes=VMEM_LIMIT),
        name="attention",
    )(qt, k, vt)


def _hgrn_kernel(q_ref, i_ref, lff_ref, lfb_ref, g_ref, nw_ref, o_ref,
                 qm_ref, km_ref, qb_ref, kd_ref, dec_ref, ut_ref, st_ref):
    S = q_ref.shape[0]
    C = HG_CHUNK
    K = HG_EXPAND
    n_chunks = S // C
    nb = HG_BLOCK
    rows_b = nb * C
    r = lax.broadcasted_iota(jnp.int32, (C, C), 0)
    c = lax.broadcasted_iota(jnp.int32, (C, C), 1)
    lower = r >= c
    upper = r <= c
    tri2 = [jnp.concatenate([t, t], axis=1) for t in
            (jnp.where(lower, 1.0, 0.0).astype(BF16), jnp.where(upper, 1.0, 0.0).astype(BF16))]
    mid = (C // 2, C - 1 - C // 2)
    last = (C - 1, 0)

    def block_rows(t):
        return t * rows_b if isinstance(t, int) else pl.multiple_of(t * rows_b, rows_b)

    def prefix_sums(t):
        rows = pl.ds(block_rows(t), rows_b)
        lfs = (lff_ref[rows, :], lfb_ref[rows, :])
        b3s = []
        for d, lf in enumerate(lfs):
            hi = lf.astype(BF16)
            lo = (lf - hi.astype(F32)).astype(BF16)
            b3s.append(jnp.stack([
                jnp.dot(tri2[d],
                        jnp.concatenate([hi[j * C:(j + 1) * C], lo[j * C:(j + 1) * C]], axis=0),
                        preferred_element_type=F32)
                for j in range(nb)]))
        return lfs, b3s

    def state_updates(t):
        r0 = block_rows(t)
        for j in range(nb):
            cr = pl.ds(r0 + j * C, C)
            ivt = i_ref[cr, :].astype(F32).T.astype(BF16)
            ut_ref[t * nb + j] = jnp.dot(ivt, kd_ref[cr, :], preferred_element_type=F32)

    def scaled_operands(t, lfs, b3s):
        rows = pl.ds(block_rows(t), rows_b)
        q3 = q_ref[rows, :].astype(F32).reshape(nb, C, K)
        for d, (lf, b3) in enumerate(zip(lfs, b3s)):
            k3 = (1.0 - jnp.exp2(lf)).reshape(nb, C, K)
            bm = b3[:, mid[d]:mid[d] + 1, :]
            bl = b3[:, last[d]:last[d] + 1, :]
            e = jnp.exp2(b3 - bm)
            einv = jnp.exp2(bm - b3)
            qm = q3 * e
            km = k3 * einv
            qm_ref[d, rows, :] = qm.reshape(rows_b, K).astype(BF16)
            km_ref[d, rows, :] = km.reshape(rows_b, K).astype(BF16)
            qb_ref[rows, d * K:(d + 1) * K] = (qm * jnp.exp2(bm)).reshape(rows_b, K).astype(BF16)
            kd_ref[rows, d * K:(d + 1) * K] = (km * jnp.exp2(bl - bm)).reshape(rows_b, K).astype(BF16)
            c0 = t * nb if isinstance(t, int) else pl.multiple_of(t * nb, nb)
            dec_ref[d, pl.ds(c0, nb), :] = jnp.exp2(bl).reshape(nb, K)

    n_blocks = n_chunks // nb
    scaled_operands(0, *prefix_sums(0))

    def prep(t, _):
        sums = prefix_sums(t)
        state_updates(t - 1)
        scaled_operands(t, *sums)
        return 0

    lax.fori_loop(1, n_blocks, prep, 0)
    state_updates(n_blocks - 1)

    def chain(n, states):
        sf, sb = states
        m = n_chunks - 1 - n
        st_ref[n, :, 0:K] = sf.astype(BF16)
        st_ref[m, :, K:2 * K] = sb.astype(BF16)
        return (dec_ref[0, pl.ds(n, 1), :] * sf + ut_ref[n, :, 0:K],
                dec_ref[1, pl.ds(m, 1), :] * sb + ut_ref[m, :, K:2 * K])

    zero_state = jnp.zeros((K, K), F32)
    lax.fori_loop(0, n_chunks, chain, (zero_state, zero_state))

    nw = nw_ref[...]
    nt_dims = (((1,), (1,)), ((), ()))

    no = HG_BLOCK_OUT

    def outp(t, _):
        r0 = pl.multiple_of(t * (no * C), no * C)
        crs = [pl.ds(r0 + j * C, C) for j in range(no)]
        a_f = [lax.dot_general(qm_ref[0, cr, :], km_ref[0, cr, :], nt_dims,
                               preferred_element_type=F32) for cr in crs]
        a_b = [lax.dot_general(qm_ref[1, cr, :], km_ref[1, cr, :], nt_dims,
                               preferred_element_type=F32) for cr in crs]
        a = [(jnp.where(lower, f, 0.0) + jnp.where(upper, b, 0.0)).astype(BF16)
             for f, b in zip(a_f, a_b)]
        outs = [jnp.dot(a[j], i_ref[cr, :], preferred_element_type=F32)
                + lax.dot_general(qb_ref[cr, :], st_ref[t * no + j], nt_dims,
                                  preferred_element_type=F32)
                for j, cr in enumerate(crs)]
        for cr, o in zip(crs, outs):
            ms = jnp.mean(o * o, axis=-1, keepdims=True)
            y = o * lax.rsqrt(ms + RMS_EPS) * nw
            o_ref[cr, :] = (y * g_ref[cr, :].astype(F32)).astype(o_ref.dtype)
        return 0

    lax.fori_loop(0, n_chunks // no, outp, 0)


def _hgrn_call(hq, hi, lf, hg, nw):
    B, S, _ = hq.shape
    n_chunks = S // HG_CHUNK
    K = HG_EXPAND
    blk = lambda b, h: (b, 0, h)
    return pl.pallas_call(
        _hgrn_kernel,
        grid=(B, HG_HEADS),
        in_specs=[
            pl.BlockSpec((None, S, K), blk),
            pl.BlockSpec((None, S, K), blk),
            pl.BlockSpec((None, S, K), blk),
            pl.BlockSpec((None, S, K), lambda b, h: (b, 0, HG_HEADS + h)),
            pl.BlockSpec((None, S, K), blk),
            pl.BlockSpec((1, K), lambda b, h: (0, 0)),
        ],
        out_specs=pl.BlockSpec((None, S, K), blk),
        out_shape=jax.ShapeDtypeStruct((B, S, HG_WIDTH), BF16),
        scratch_shapes=[
            pltpu.VMEM((2, S, K), BF16),
            pltpu.VMEM((2, S, K), BF16),
            pltpu.VMEM((S, 2 * K), BF16),
            pltpu.VMEM((S, 2 * K), BF16),
            pltpu.VMEM((2, n_chunks, K), F32),
            pltpu.VMEM((n_chunks, K, 2 * K), F32),
            pltpu.VMEM((n_chunks, K, 2 * K), BF16),
        ],
        compiler_params=pltpu.CompilerParams(
            dimension_semantics=("arbitrary", "arbitrary"),
            vmem_limit_bytes=VMEM_LIMIT),
        name="hgrn2",
    )(hq, hi, lf, lf, hg, nw)


def _out_kernel(x_ref, attn_ref, ag_ref, hgb_ref, anw_ref, w_ref, lnw_ref, lnb_ref, o_ref):
    a = jnp.concatenate([attn_ref[g] for g in range(N_KV_HEADS)], axis=1).astype(F32)
    ms = jnp.mean(a * a, axis=-1, keepdims=True)
    ab = a * lax.rsqrt(ms + RMS_EPS) * anw_ref[...] * ag_ref[...].astype(F32)
    y = (jnp.dot(ab.astype(BF16), w_ref[:ATTN_WIDTH, :], preferred_element_type=F32)
         + jnp.dot(hgb_ref[...], w_ref[ATTN_WIDTH:, :], preferred_element_type=F32))
    z = DEEPNORM_ALPHA * x_ref[...] + y
    mu = jnp.mean(z, axis=-1, keepdims=True)
    zc = z - mu
    var = jnp.mean(zc * zc, axis=-1, keepdims=True)
    o_ref[...] = zc * lax.rsqrt(var + LN_EPS) * lnw_ref[...] + lnb_ref[...]


def _out_call(x, attn, ag, hgb, anw, w_out_b, lnw, lnb):
    B, S, D = x.shape
    tm = OUT_TM
    row = lambda b, i: (b, i, 0)
    const2 = lambda b, i: (0, 0)
    return pl.pallas_call(
        _out_kernel,
        grid=(B, S // tm),
        in_specs=[
            pl.BlockSpec((None, tm, D), row),
            pl.BlockSpec((None, N_KV_HEADS, tm, ATTN_WIDTH // N_KV_HEADS), lambda b, i: (b, 0, i, 0)),
            pl.BlockSpec((None, tm, ATTN_WIDTH), row),
            pl.BlockSpec((None, tm, HG_WIDTH), row),
            pl.BlockSpec((1, ATTN_WIDTH), const2),
            pl.BlockSpec((ATTN_WIDTH + HG_WIDTH, D), const2),
            pl.BlockSpec((1, D), const2),
            pl.BlockSpec((1, D), const2),
        ],
        out_specs=pl.BlockSpec((None, tm, D), row),
        out_shape=jax.ShapeDtypeStruct((B, S, D), x.dtype),
        compiler_params=pltpu.CompilerParams(
            dimension_semantics=("arbitrary", "arbitrary"),
            vmem_limit_bytes=VMEM_LIMIT),
        name="out_proj_ln",
    )(x, attn, ag, hgb, anw, w_out_b, lnw, lnb)


def _rope_tables(seq_len):
    n_rows = seq_len // GRID_W
    inv = ROPE_THETA ** (-jnp.arange(0, AXIS_DIM, 2, dtype=F32) / AXIS_DIM)
    lane = jnp.arange(LANES, dtype=jnp.int32)
    d = lane % HEAD_DIM
    use_col = (d // AXIS_DIM) == 1
    dd = d % AXIS_DIM
    first = dd < (AXIS_DIM // 2)
    freq = inv[dd % (AXIS_DIM // 2)]
    idx = jnp.arange(max(n_rows, GRID_W), dtype=jnp.int32).astype(F32)
    ang = idx[:, None] * freq[None, :]
    cos_i = jnp.cos(ang)
    sin_i = jnp.where(first[None, :], -jnp.sin(ang), jnp.sin(ang))

    def expand(tab):
        by_row = jnp.repeat(tab[:n_rows], GRID_W, axis=0)
        by_col = jnp.tile(tab[:GRID_W], (n_rows, 1))
        return jnp.where(use_col[None, :], by_col, by_row)

    return expand(cos_i), expand(sin_i)


def kernel(x, w_in, q_norm_w, k_norm_w, attn_norm_w, hg_lb_logits, hg_norm_w, w_out, ln_w, ln_b):
    B, S, D = x.shape
    assert w_in.shape == (DEPTH, D, IN_WIDTH) and DEPTH == 1
    cos_t, sin_t = _rope_tables(S)
    qw = jnp.tile(q_norm_w[0].astype(F32), N_Q_HEADS).reshape(1, ATTN_WIDTH)
    kw = jnp.tile(k_norm_w[0].astype(F32), N_KV_HEADS).reshape(1, KV_WIDTH)
    lbl = hg_lb_logits.astype(F32).reshape(2 * (DEPTH + 1), HG_WIDTH)
    hid = jnp.arange(ATTN_WIDTH, dtype=jnp.int32) // HEAD_DIM
    seg = jnp.where(hid[:, None] == hid[None, :], 1.0 / HEAD_DIM, 0.0).astype(BF16)

    qt, k, vt, ag, hq, hi, lf, hg = _proj_call(
        x, w_in[0].astype(BF16), cos_t, sin_t, qw, kw, lbl, seg)
    attn = _attn_call(qt, k, vt)
    hgb = _hgrn_call(hq, hi, lf, hg, hg_norm_w[0].astype(F32).reshape(1, HG_EXPAND))
    return _out_call(
        x, attn, ag, hgb,
        attn_norm_w[0].astype(F32).reshape(1, ATTN_WIDTH),
        w_out[0].astype(BF16),
        ln_w[0].astype(F32).reshape(1, D),
        ln_b[0].astype(F32).reshape(1, D))
```

```python
import functools

import jax
import jax.numpy as jnp
from jax import lax
from jax.experimental import pallas as pl
from jax.experimental.pallas import tpu as pltpu

F32 = jnp.float32
BF16 = jnp.bfloat16

GRID_W = 64
ATTN_WIDTH = 512
HEAD_DIM = 64
N_Q_HEADS = 8
N_KV_HEADS = 2
KV_WIDTH = 128
AXIS_DIM = 32
ROPE_THETA = 10000.0
HG_WIDTH = 512
HG_EXPAND = 128
HG_HEADS = 4
HG_CHUNK = 64
RMS_EPS = 1e-6
LN_EPS = 1e-5
DEPTH = 1
DEEPNORM_ALPHA = (2 * DEPTH) ** 0.25
IN_WIDTH = 3840
LOG2E = 1.4426950408889634
Q_SCALE = HEAD_DIM ** -0.5 * LOG2E

LANES = 128
VMEM_LIMIT = 56 * 1024 * 1024

PROJ_TM = 512
ATTN_TQ = 128
ATTN_TK = 1024
HG_BLOCK = 8
HG_BLOCK_OUT = 64
OUT_TM = 1024


def _exact_zero(x):
    bits = lax.bitcast_convert_type(x, jnp.uint32)
    return ((bits >> 16) >> 16).astype(F32)


def _sigmoid(x):
    return 1.0 / (1.0 + jnp.exp(-x))


def _proj_kernel(x_ref, w_ref, cos_ref, sin_ref, qw_ref, kw_ref, lbl_ref, seg_ref,
                 qt_ref, k_ref, vt_ref, ag_ref, hq_ref, hi_ref, lf_ref, hg_ref):
    xb = x_ref[...].astype(BF16)
    tm = xb.shape[0]

    def proj(a, b):
        return jnp.dot(xb, w_ref[:, a:b], preferred_element_type=F32)

    cosv = cos_ref[...]
    sinv = sin_ref[...]
    lane = lax.broadcasted_iota(jnp.int32, (tm, LANES), 1)
    first_half = (lane % AXIS_DIM) < (AXIS_DIM // 2)

    def norm_rope(a, w_row, seg, scale):
        ms = jnp.dot((a * a).astype(BF16), seg, preferred_element_type=F32)
        y = a * lax.rsqrt(ms + RMS_EPS) * w_row
        outs = []
        for c in range(a.shape[1] // LANES):
            yc = y[:, c * LANES:(c + 1) * LANES]
            partner = jnp.where(first_half,
                                pltpu.roll(yc, LANES - AXIS_DIM // 2, 1),
                                pltpu.roll(yc, AXIS_DIM // 2, 1))
            outs.append((yc * cosv + partner * sinv) * scale)
        return outs

    aq = proj(0, ATTN_WIDTH)
    qs = norm_rope(aq, qw_ref[...], seg_ref[...], Q_SCALE)
    for c, qc in enumerate(qs):
        qt_ref[c * LANES:(c + 1) * LANES, :] = qc.T.astype(BF16)
    akv = proj(ATTN_WIDTH, ATTN_WIDTH + 2 * KV_WIDTH)
    kk = norm_rope(akv[:, :KV_WIDTH], kw_ref[...], seg_ref[:KV_WIDTH, :KV_WIDTH], 1.0)[0]
    k_ref[...] = kk.astype(BF16)
    vt_ref[...] = akv[:, KV_WIDTH:].T.astype(BF16)

    base = ATTN_WIDTH + 2 * KV_WIDTH
    ag = proj(base, base + ATTN_WIDTH)
    ag_ref[...] = (ag * _sigmoid(ag)).astype(BF16)
    base += ATTN_WIDTH
    hq = proj(base, base + HG_WIDTH)
    hq_ref[...] = (hq * _sigmoid(hq) * (HG_EXPAND ** -0.5)).astype(BF16)
    base += HG_WIDTH
    hi_ref[...] = proj(base, base + HG_WIDTH).astype(BF16)
    base += HG_WIDTH
    lbl = lbl_ref[...]
    for d in range(2):
        l0 = lbl[2 * d:2 * d + 1, :]
        l1 = lbl[2 * d + 1:2 * d + 2, :]
        mx = jnp.maximum(l0, l1)
        e0 = jnp.exp(l0 - mx)
        e1 = jnp.exp(l1 - mx)
        lb = e0 / (e0 + e1)
        z = proj(base, base + HG_WIDTH)
        lf_ref[:, d * HG_WIDTH:(d + 1) * HG_WIDTH] = jnp.log(lb + (1.0 - lb) * _sigmoid(z)) * LOG2E
        base += HG_WIDTH
    hg = proj(base, base + HG_WIDTH)
    hg_ref[...] = (hg * _sigmoid(hg)).astype(BF16)


def _proj_call(x, w_in_b, cos_t, sin_t, qw, kw, lbl, seg):
    B, S, D = x.shape
    tm = PROJ_TM
    nt = S // tm
    row = lambda b, i: (b, i, 0)
    const2 = lambda b, i: (0, 0)
    out_shape = (
        jax.ShapeDtypeStruct((B, ATTN_WIDTH, S), BF16),
        jax.ShapeDtypeStruct((B, S, KV_WIDTH), BF16),
        jax.ShapeDtypeStruct((B, KV_WIDTH, S), BF16),
        jax.ShapeDtypeStruct((B, S, ATTN_WIDTH), BF16),
        jax.ShapeDtypeStruct((B, S, HG_WIDTH), BF16),
        jax.ShapeDtypeStruct((B, S, HG_WIDTH), BF16),
        jax.ShapeDtypeStruct((B, S, 2 * HG_WIDTH), F32),
        jax.ShapeDtypeStruct((B, S, HG_WIDTH), BF16),
    )
    in_specs = [
        pl.BlockSpec((None, tm, D), row),
        pl.BlockSpec((D, IN_WIDTH), const2),
        pl.BlockSpec((tm, LANES), lambda b, i: (i, 0)),
        pl.BlockSpec((tm, LANES), lambda b, i: (i, 0)),
        pl.BlockSpec((1, ATTN_WIDTH), const2),
        pl.BlockSpec((1, KV_WIDTH), const2),
        pl.BlockSpec((4, HG_WIDTH), const2),
        pl.BlockSpec((ATTN_WIDTH, ATTN_WIDTH), const2),
    ]
    out_specs = (
        pl.BlockSpec((None, ATTN_WIDTH, tm), lambda b, i: (b, 0, i)),
        pl.BlockSpec((None, tm, KV_WIDTH), row),
        pl.BlockSpec((None, KV_WIDTH, tm), lambda b, i: (b, 0, i)),
        pl.BlockSpec((None, tm, ATTN_WIDTH), row),
        pl.BlockSpec((None, tm, HG_WIDTH), row),
        pl.BlockSpec((None, tm, HG_WIDTH), row),
        pl.BlockSpec((None, tm, 2 * HG_WIDTH), row),
        pl.BlockSpec((None, tm, HG_WIDTH), row),
    )
    return pl.pallas_call(
        _proj_kernel,
        grid=(B, nt),
        in_specs=in_specs,
        out_specs=out_specs,
        out_shape=out_shape,
        compiler_params=pltpu.CompilerParams(
            dimension_semantics=("arbitrary", "arbitrary"),
            vmem_limit_bytes=VMEM_LIMIT),
        name="in_proj",
    )(x, w_in_b, cos_t, sin_t, qw, kw, lbl, seg)


def _attn_kernel(qt_ref, k_ref, vt_ref, o_ref, s_ref, p_ref, vx_ref):
    S = k_ref.shape[0]
    G = N_Q_HEADS // N_KV_HEADS
    tq, tk = ATTN_TQ, ATTN_TK
    cols = G * tq
    n_units = N_KV_HEADS * (S // tq)
    n_c = S // tk
    vrows = vx_ref.shape[1]

    for g in range(N_KV_HEADS):
        vx_ref[g, 0:HEAD_DIM, :] = vt_ref[g * HEAD_DIM:(g + 1) * HEAD_DIM, :]
        vx_ref[g, HEAD_DIM:vrows, :] = jnp.ones((vrows - HEAD_DIM, S), BF16)

    row_group = lax.broadcasted_iota(jnp.int32, (N_KV_HEADS * HEAD_DIM, cols), 0) // HEAD_DIM

    def q_weights(u):
        i, g = u >> 1, u & 1
        c0 = pl.multiple_of(i * tq, tq)
        qs = jnp.concatenate(
            [qt_ref[pl.ds(pl.multiple_of((G * g + h) * HEAD_DIM, HEAD_DIM), HEAD_DIM),
                    pl.ds(c0, tq)] for h in range(G)], axis=1)
        return jnp.where(row_group == g, jnp.concatenate([qs, qs], axis=0), 0.0).astype(BF16)

    def stage(u, m_prev, do_scores=True, do_probs=True, do_pv=True):
        if do_scores:
            w = q_weights(u)
            mx = jnp.full((8, cols), -jnp.inf, F32)
        if do_pv:
            g_pv = (u - 2) & 1
            acc = jnp.zeros((vrows, cols), F32)
        for c in range(n_c):
            rows = pl.ds(c * tk, tk)
            m_c = m_prev
            if do_pv:
                acc = acc + jnp.dot(vx_ref[g_pv, :, rows], p_ref[rows, :],
                                    preferred_element_type=F32)
                if do_probs:
                    guard = sum(_exact_zero(acc[0:8, t * 2 * LANES:(t * 2 + 1) * LANES])
                                for t in range(cols // (2 * LANES)))
                    m_c = m_prev + jnp.tile(guard[0:1, :], (1, cols // LANES))
            if do_probs:
                p_ref[rows, :] = jnp.exp2(s_ref[rows, :] - m_c).astype(BF16)
            if do_scores:
                s = jnp.dot(k_ref[rows, :], w, preferred_element_type=F32)
                s_ref[rows, :] = s
                mx = jnp.maximum(mx, jnp.max(s.reshape(tk // 8, 8, cols), axis=0))
        if do_pv:
            o_t = acc[0:HEAD_DIM, :] / acc[HEAD_DIM:HEAD_DIM + 1, :]
            o = o_t.T
            blk = jnp.concatenate([o[h * tq:(h + 1) * tq, :] for h in range(G)], axis=1)
            r0 = pl.multiple_of(((u - 2) >> 1) * tq, tq)
            o_ref[g_pv, pl.ds(r0, tq), :] = blk.astype(o_ref.dtype)
        if do_scores:
            return jnp.max(mx, axis=0, keepdims=True)
        return m_prev

    zero = jnp.int32(0)
    m = stage(zero, None, do_probs=False, do_pv=False)
    m = stage(zero + 1, m, do_pv=False)
    m = lax.fori_loop(2, n_units, stage, m)
    m = stage(zero + n_units, m, do_scores=False)
    stage(zero + n_units + 1, m, do_scores=False, do_probs=False)


def _attn_call(qt, k, vt):
    B, _, S = qt.shape
    G = N_Q_HEADS // N_KV_HEADS
    cols = G * ATTN_TQ
    gw = G * HEAD_DIM
    whole = lambda b: (b, 0, 0)
    return pl.pallas_call(
        _attn_kernel,
        grid=(B,),
        in_specs=[
            pl.BlockSpec((None, ATTN_WIDTH, S), whole),
            pl.BlockSpec((None, S, KV_WIDTH), whole),
            pl.BlockSpec((None, KV_WIDTH, S), whole),
        ],
        out_specs=pl.BlockSpec((None, N_KV_HEADS, S, gw), lambda b: (b, 0, 0, 0)),
        out_shape=jax.ShapeDtypeStruct((B, N_KV_HEADS, S, gw), BF16),
        scratch_shapes=[
            pltpu.VMEM((S, cols), F32),
            pltpu.VMEM((S, cols), BF16),
            pltpu.VMEM((N_KV_HEADS, HEAD_DIM + 16, S), BF16),
        ],
        compiler_params=pltpu.CompilerParams(
            dimension_semantics=("arbitrary",),
            vmem_limit_bytes=VMEM_LIMIT),
        name="attention",
    )(qt, k, vt)


def _hgrn_kernel(q_ref, i_ref, lff_ref, lfb_ref, g_ref, nw_ref, o_ref,
                 qm_ref, km_ref, qb_ref, kd_ref, dec_ref, ut_ref, st_ref):
    S = q_ref.shape[0]
    C = HG_CHUNK
    K = HG_EXPAND
    n_chunks = S // C
    nb = HG_BLOCK
    rows_b = nb * C
    r = lax.broadcasted_iota(jnp.int32, (C, C), 0)
    c = lax.broadcasted_iota(jnp.int32, (C, C), 1)
    lower = r >= c
    upper = r <= c
    tri2 = [jnp.concatenate([t, t], axis=1) for t in
            (jnp.where(lower, 1.0, 0.0).astype(BF16), jnp.where(upper, 1.0, 0.0).astype(BF16))]
    mid = (C // 2, C - 1 - C // 2)
    last = (C - 1, 0)

    def block_rows(t):
        return t * rows_b if isinstance(t, int) else pl.multiple_of(t * rows_b, rows_b)

    def prefix_sums(t):
        rows = pl.ds(block_rows(t), rows_b)
        lfs = (lff_ref[rows, :], lfb_ref[rows, :])
        b3s = []
        for d, lf in enumerate(lfs):
            hi = lf.astype(BF16)
            lo = (lf - hi.astype(F32)).astype(BF16)
            b3s.append(jnp.stack([
                jnp.dot(tri2[d],
                        jnp.concatenate([hi[j * C:(j + 1) * C], lo[j * C:(j + 1) * C]], axis=0),
                        preferred_element_type=F32)
                for j in range(nb)]))
        return lfs, b3s

    def state_updates(t):
        r0 = block_rows(t)
        for j in range(nb):
            cr = pl.ds(r0 + j * C, C)
            ivt = i_ref[cr, :].astype(F32).T.astype(BF16)
            ut_ref[t * nb + j] = jnp.dot(ivt, kd_ref[cr, :], preferred_element_type=F32)

    def scaled_operands(t, lfs, b3s):
        rows = pl.ds(block_rows(t), rows_b)
        q3 = q_ref[rows, :].astype(F32).reshape(nb, C, K)
        for d, (lf, b3) in enumerate(zip(lfs, b3s)):
            k3 = (1.0 - jnp.exp2(lf)).reshape(nb, C, K)
            bm = b3[:, mid[d]:mid[d] + 1, :]
            bl = b3[:, last[d]:last[d] + 1, :]
            e = jnp.exp2(b3 - bm)
            einv = jnp.exp2(bm - b3)
            qm = q3 * e
            km = k3 * einv
            qm_ref[d, rows, :] = qm.reshape(rows_b, K).astype(BF16)
            km_ref[d, rows, :] = km.reshape(rows_b, K).astype(BF16)
            qb_ref[rows, d * K:(d + 1) * K] = (qm * jnp.exp2(bm)).reshape(rows_b, K).astype(BF16)
            kd_ref[rows, d * K:(d + 1) * K] = (km * jnp.exp2(bl - bm)).reshape(rows_b, K).astype(BF16)
            c0 = t * nb if isinstance(t, int) else pl.multiple_of(t * nb, nb)
            dec_ref[d, pl.ds(c0, nb), :] = jnp.exp2(bl).reshape(nb, K)

    n_blocks = n_chunks // nb
    scaled_operands(0, *prefix_sums(0))

    def prep(t, _):
        sums = prefix_sums(t)
        state_updates(t - 1)
        scaled_operands(t, *sums)
        return 0

    lax.fori_loop(1, n_blocks, prep, 0)
    state_updates(n_blocks - 1)

    def chain(n, states):
        sf, sb = states
        m = n_chunks - 1 - n
        st_ref[n, :, 0:K] = sf.astype(BF16)
        st_ref[m, :, K:2 * K] = sb.astype(BF16)
        return (dec_ref[0, pl.ds(n, 1), :] * sf + ut_ref[n, :, 0:K],
                dec_ref[1, pl.ds(m, 1), :] * sb + ut_ref[m, :, K:2 * K])

    zero_state = jnp.zeros((K, K), F32)
    lax.fori_loop(0, n_chunks, chain, (zero_state, zero_state))

    nw = nw_ref[...]
    nt_dims = (((1,), (1,)), ((), ()))

    no = HG_BLOCK_OUT

    def outp(t, _):
        r0 = pl.multiple_of(t * (no * C), no * C)
        crs = [pl.ds(r0 + j * C, C) for j in range(no)]
        a_f = [lax.dot_general(qm_ref[0, cr, :], km_ref[0, cr, :], nt_dims,
                               preferred_element_type=F32) for cr in crs]
        a_b = [lax.dot_general(qm_ref[1, cr, :], km_ref[1, cr, :], nt_dims,
                               preferred_element_type=F32) for cr in crs]
        a = [(jnp.where(lower, f, 0.0) + jnp.where(upper, b, 0.0)).astype(BF16)
             for f, b in zip(a_f, a_b)]
        outs = [jnp.dot(a[j], i_ref[cr, :], preferred_element_type=F32)
                + lax.dot_general(qb_ref[cr, :], st_ref[t * no + j], nt_dims,
                                  preferred_element_type=F32)
                for j, cr in enumerate(crs)]
        for cr, o in zip(crs, outs):
            ms = jnp.mean(o * o, axis=-1, keepdims=True)
            y = o * lax.rsqrt(ms + RMS_EPS) * nw
            o_ref[cr, :] = (y * g_ref[cr, :].astype(F32)).astype(o_ref.dtype)
        return 0

    lax.fori_loop(0, n_chunks // no, outp, 0)


def _hgrn_call(hq, hi, lf, hg, nw):
    B, S, _ = hq.shape
    n_chunks = S // HG_CHUNK
    K = HG_EXPAND
    blk = lambda b, h: (b, 0, h)
    return pl.pallas_call(
        _hgrn_kernel,
        grid=(B, HG_HEADS),
        in_specs=[
            pl.BlockSpec((None, S, K), blk),
            pl.BlockSpec((None, S, K), blk),
            pl.BlockSpec((None, S, K), blk),
            pl.BlockSpec((None, S, K), lambda b, h: (b, 0, HG_HEADS + h)),
            pl.BlockSpec((None, S, K), blk),
            pl.BlockSpec((1, K), lambda b, h: (0, 0)),
        ],
        out_specs=pl.BlockSpec((None, S, K), blk),
        out_shape=jax.ShapeDtypeStruct((B, S, HG_WIDTH), BF16),
        scratch_shapes=[
            pltpu.VMEM((2, S, K), BF16),
            pltpu.VMEM((2, S, K), BF16),
            pltpu.VMEM((S, 2 * K), BF16),
            pltpu.VMEM((S, 2 * K), BF16),
            pltpu.VMEM((2, n_chunks, K), F32),
            pltpu.VMEM((n_chunks, K, 2 * K), F32),
            pltpu.VMEM((n_chunks, K, 2 * K), BF16),
        ],
        compiler_params=pltpu.CompilerParams(
            dimension_semantics=("arbitrary", "arbitrary"),
            vmem_limit_bytes=VMEM_LIMIT),
        name="hgrn2",
    )(hq, hi, lf, lf, hg, nw)


def _out_kernel(x_ref, attn_ref, ag_ref, hgb_ref, anw_ref, w_ref, lnw_ref, lnb_ref, o_ref):
    a = jnp.concatenate([attn_ref[g] for g in range(N_KV_HEADS)], axis=1).astype(F32)
    ms = jnp.mean(a * a, axis=-1, keepdims=True)
    ab = a * lax.rsqrt(ms + RMS_EPS) * anw_ref[...] * ag_ref[...].astype(F32)
    y = (jnp.dot(ab.astype(BF16), w_ref[:ATTN_WIDTH, :], preferred_element_type=F32)
         + jnp.dot(hgb_ref[...], w_ref[ATTN_WIDTH:, :], preferred_element_type=F32))
    z = DEEPNORM_ALPHA * x_ref[...] + y
    mu = jnp.mean(z, axis=-1, keepdims=True)
    zc = z - mu
    var = jnp.mean(zc * zc, axis=-1, keepdims=True)
    o_ref[...] = zc * lax.rsqrt(var + LN_EPS) * lnw_ref[...] + lnb_ref[...]


def _out_call(x, attn, ag, hgb, anw, w_out_b, lnw, lnb):
    B, S, D = x.shape
    tm = OUT_TM
    row = lambda b, i: (b, i, 0)
    const2 = lambda b, i: (0, 0)
    return pl.pallas_call(
        _out_kernel,
        grid=(B, S // tm),
        in_specs=[
            pl.BlockSpec((None, tm, D), row),
            pl.BlockSpec((None, N_KV_HEADS, tm, ATTN_WIDTH // N_KV_HEADS), lambda b, i: (b, 0, i, 0)),
            pl.BlockSpec((None, tm, ATTN_WIDTH), row),
            pl.BlockSpec((None, tm, HG_WIDTH), row),
            pl.BlockSpec((1, ATTN_WIDTH), const2),
            pl.BlockSpec((ATTN_WIDTH + HG_WIDTH, D), const2),
            pl.BlockSpec((1, D), const2),
            pl.BlockSpec((1, D), const2),
        ],
        out_specs=pl.BlockSpec((None, tm, D), row),
        out_shape=jax.ShapeDtypeStruct((B, S, D), x.dtype),
        compiler_params=pltpu.CompilerParams(
            dimension_semantics=("arbitrary", "arbitrary"),
            vmem_limit_bytes=VMEM_LIMIT),
        name="out_proj_ln",
    )(x, attn, ag, hgb, anw, w_out_b, lnw, lnb)


def _rope_tables(seq_len):
    n_rows = seq_len // GRID_W
    inv = ROPE_THETA ** (-jnp.arange(0, AXIS_DIM, 2, dtype=F32) / AXIS_DIM)
    lane = jnp.arange(LANES, dtype=jnp.int32)
    d = lane % HEAD_DIM
    use_col = (d // AXIS_DIM) == 1
    dd = d % AXIS_DIM
    first = dd < (AXIS_DIM // 2)
    freq = inv[dd % (AXIS_DIM // 2)]
    idx = jnp.arange(max(n_rows, GRID_W), dtype=jnp.int32).astype(F32)
    ang = idx[:, None] * freq[None, :]
    cos_i = jnp.cos(ang)
    sin_i = jnp.where(first[None, :], -jnp.sin(ang), jnp.sin(ang))
    cos_i, sin_i = lax.optimization_barrier((cos_i, sin_i))

    def expand(tab):
        by_row = jnp.repeat(tab[:n_rows], GRID_W, axis=0)
        by_col = jnp.tile(tab[:GRID_W], (n_rows, 1))
        return jnp.where(use_col[None, :], by_col, by_row)

    return expand(cos_i), expand(sin_i)


def kernel(x, w_in, q_norm_w, k_norm_w, attn_norm_w, hg_lb_logits, hg_norm_w, w_out, ln_w, ln_b):
    B, S, D = x.shape
    assert w_in.shape == (DEPTH, D, IN_WIDTH) and DEPTH == 1
    cos_t, sin_t = _rope_tables(S)
    qw = jnp.tile(q_norm_w[0].astype(F32), N_Q_HEADS).reshape(1, ATTN_WIDTH)
    kw = jnp.tile(k_norm_w[0].astype(F32), N_KV_HEADS).reshape(1, KV_WIDTH)
    lbl = hg_lb_logits.astype(F32).reshape(2 * (DEPTH + 1), HG_WIDTH)
    hid = jnp.arange(ATTN_WIDTH, dtype=jnp.int32) // HEAD_DIM
    seg = jnp.where(hid[:, None] == hid[None, :], 1.0 / HEAD_DIM, 0.0).astype(BF16)

    qt, k, vt, ag, hq, hi, lf, hg = _proj_call(
        x, w_in[0].astype(BF16), cos_t, sin_t, qw, kw, lbl, seg)
    attn = _attn_call(qt, k, vt)
    hgb = _hgrn_call(hq, hi, lf, hg, hg_norm_w[0].astype(F32).reshape(1, HG_EXPAND))
    return _out_call(
        x, attn, ag, hgb,
        attn_norm_w[0].astype(F32).reshape(1, ATTN_WIDTH),
        w_out[0].astype(BF16),
        ln_w[0].astype(F32).reshape(1, D),
        ln_b[0].astype(F32).reshape(1, D))
```

```python
import functools

import jax
import jax.numpy as jnp
from jax import lax
from jax.experimental import pallas as pl
from jax.experimental.pallas import tpu as pltpu

F32 = jnp.float32
BF16 = jnp.bfloat16

GRID_W = 64
ATTN_WIDTH = 512
HEAD_DIM = 64
N_Q_HEADS = 8
N_KV_HEADS = 2
KV_WIDTH = 128
AXIS_DIM = 32
ROPE_THETA = 10000.0
HG_WIDTH = 512
HG_EXPAND = 128
HG_HEADS = 4
HG_CHUNK = 64
RMS_EPS = 1e-6
LN_EPS = 1e-5
DEPTH = 1
DEEPNORM_ALPHA = (2 * DEPTH) ** 0.25
IN_WIDTH = 3840
LOG2E = 1.4426950408889634
Q_SCALE = HEAD_DIM ** -0.5 * LOG2E

LANES = 128
VMEM_LIMIT = 56 * 1024 * 1024

PROJ_TM = 512
ATTN_TQ = 128
ATTN_TK = 1024
HG_BLOCK = 8
HG_BLOCK_OUT = 64
OUT_TM = 1024


def _exact_zero(x):
    bits = lax.bitcast_convert_type(x, jnp.uint32)
    return ((bits >> 16) >> 16).astype(F32)


def _sigmoid(x):
    return 1.0 / (1.0 + jnp.exp(-x))


def _proj_kernel(x_ref, w_ref, cos_ref, sin_ref, qw_ref, kw_ref, lbl_ref, seg_ref,
                 qt_ref, k_ref, vt_ref, ag_ref, hq_ref, hi_ref, lf_ref, hg_ref):
    xb = x_ref[...].astype(BF16)
    tm = xb.shape[0]

    def proj(a, b):
        return jnp.dot(xb, w_ref[:, a:b], preferred_element_type=F32)

    cosv = cos_ref[...]
    sinv = sin_ref[...]
    lane = lax.broadcasted_iota(jnp.int32, (tm, LANES), 1)
    first_half = (lane % AXIS_DIM) < (AXIS_DIM // 2)

    def norm_rope(a, w_row, seg, scale):
        sq = (a * a).astype(BF16)
        blk = min(a.shape[1], 2 * LANES)
        ms = jnp.concatenate(
            [jnp.dot(sq[:, j:j + blk], seg[j:j + blk, j:j + blk], preferred_element_type=F32)
             for j in range(0, a.shape[1], blk)], axis=1)
        y = a * lax.rsqrt(ms + RMS_EPS) * w_row
        outs = []
        for c in range(a.shape[1] // LANES):
            yc = y[:, c * LANES:(c + 1) * LANES]
            partner = jnp.where(first_half,
                                pltpu.roll(yc, LANES - AXIS_DIM // 2, 1),
                                pltpu.roll(yc, AXIS_DIM // 2, 1))
            outs.append((yc * cosv + partner * sinv) * scale)
        return outs

    aq = proj(0, ATTN_WIDTH)
    qs = norm_rope(aq, qw_ref[...], seg_ref[...], Q_SCALE)
    for c, qc in enumerate(qs):
        qt_ref[c * LANES:(c + 1) * LANES, :] = qc.T.astype(BF16)
    akv = proj(ATTN_WIDTH, ATTN_WIDTH + 2 * KV_WIDTH)
    kk = norm_rope(akv[:, :KV_WIDTH], kw_ref[...], seg_ref[:KV_WIDTH, :KV_WIDTH], 1.0)[0]
    k_ref[...] = kk.astype(BF16)
    vt_ref[...] = akv[:, KV_WIDTH:].T.astype(BF16)

    base = ATTN_WIDTH + 2 * KV_WIDTH
    ag = proj(base, base + ATTN_WIDTH)
    ag_ref[...] = (ag * _sigmoid(ag)).astype(BF16)
    base += ATTN_WIDTH
    hq = proj(base, base + HG_WIDTH)
    hq_ref[...] = (hq * _sigmoid(hq) * (HG_EXPAND ** -0.5)).astype(BF16)
    base += HG_WIDTH
    hi_ref[...] = proj(base, base + HG_WIDTH).astype(BF16)
    base += HG_WIDTH
    lbl = lbl_ref[...]
    for d in range(2):
        l0 = lbl[2 * d:2 * d + 1, :]
        l1 = lbl[2 * d + 1:2 * d + 2, :]
        mx = jnp.maximum(l0, l1)
        e0 = jnp.exp(l0 - mx)
        e1 = jnp.exp(l1 - mx)
        lb = e0 / (e0 + e1)
        z = proj(base, base + HG_WIDTH)
        lf_ref[:, d * HG_WIDTH:(d + 1) * HG_WIDTH] = jnp.log(lb + (1.0 - lb) * _sigmoid(z)) * LOG2E
        base += HG_WIDTH
    hg = proj(base, base + HG_WIDTH)
    hg_ref[...] = (hg * _sigmoid(hg)).astype(BF16)


def _proj_call(x, w_in_b, cos_t, sin_t, qw, kw, lbl, seg):
    B, S, D = x.shape
    tm = PROJ_TM
    nt = S // tm
    row = lambda b, i: (b, i, 0)
    const2 = lambda b, i: (0, 0)
    out_shape = (
        jax.ShapeDtypeStruct((B, ATTN_WIDTH, S), BF16),
        jax.ShapeDtypeStruct((B, S, KV_WIDTH), BF16),
        jax.ShapeDtypeStruct((B, KV_WIDTH, S), BF16),
        jax.ShapeDtypeStruct((B, S, ATTN_WIDTH), BF16),
        jax.ShapeDtypeStruct((B, S, HG_WIDTH), BF16),
        jax.ShapeDtypeStruct((B, S, HG_WIDTH), BF16),
        jax.ShapeDtypeStruct((B, S, 2 * HG_WIDTH), F32),
        jax.ShapeDtypeStruct((B, S, HG_WIDTH), BF16),
    )
    in_specs = [
        pl.BlockSpec((None, tm, D), row),
        pl.BlockSpec((D, IN_WIDTH), const2),
        pl.BlockSpec((tm, LANES), lambda b, i: (i, 0)),
        pl.BlockSpec((tm, LANES), lambda b, i: (i, 0)),
        pl.BlockSpec((1, ATTN_WIDTH), const2),
        pl.BlockSpec((1, KV_WIDTH), const2),
        pl.BlockSpec((4, HG_WIDTH), const2),
        pl.BlockSpec((ATTN_WIDTH, ATTN_WIDTH), const2),
    ]
    out_specs = (
        pl.BlockSpec((None, ATTN_WIDTH, tm), lambda b, i: (b, 0, i)),
        pl.BlockSpec((None, tm, KV_WIDTH), row),
        pl.BlockSpec((None, KV_WIDTH, tm), lambda b, i: (b, 0, i)),
        pl.BlockSpec((None, tm, ATTN_WIDTH), row),
        pl.BlockSpec((None, tm, HG_WIDTH), row),
        pl.BlockSpec((None, tm, HG_WIDTH), row),
        pl.BlockSpec((None, tm, 2 * HG_WIDTH), row),
        pl.BlockSpec((None, tm, HG_WIDTH), row),
    )
    return pl.pallas_call(
        _proj_kernel,
        grid=(B, nt),
        in_specs=in_specs,
        out_specs=out_specs,
        out_shape=out_shape,
        compiler_params=pltpu.CompilerParams(
            dimension_semantics=("arbitrary", "arbitrary"),
            vmem_limit_bytes=VMEM_LIMIT),
        name="in_proj",
    )(x, w_in_b, cos_t, sin_t, qw, kw, lbl, seg)


def _attn_kernel(qt_ref, k_ref, vt_ref, o_ref, s_ref, p_ref, vx_ref):
    S = k_ref.shape[0]
    G = N_Q_HEADS // N_KV_HEADS
    tq, tk = ATTN_TQ, ATTN_TK
    cols = G * tq
    n_units = N_KV_HEADS * (S // tq)
    n_c = S // tk
    vrows = vx_ref.shape[1]

    for g in range(N_KV_HEADS):
        vx_ref[g, 0:HEAD_DIM, :] = vt_ref[g * HEAD_DIM:(g + 1) * HEAD_DIM, :]
        vx_ref[g, HEAD_DIM:vrows, :] = jnp.ones((vrows - HEAD_DIM, S), BF16)

    row_group = lax.broadcasted_iota(jnp.int32, (N_KV_HEADS * HEAD_DIM, cols), 0) // HEAD_DIM

    def q_weights(u):
        i, g = u >> 1, u & 1
        c0 = pl.multiple_of(i * tq, tq)
        qs = jnp.concatenate(
            [qt_ref[pl.ds(pl.multiple_of((G * g + h) * HEAD_DIM, HEAD_DIM), HEAD_DIM),
                    pl.ds(c0, tq)] for h in range(G)], axis=1)
        return jnp.where(row_group == g, jnp.concatenate([qs, qs], axis=0), 0.0).astype(BF16)

    def stage(u, m_prev, do_scores=True, do_probs=True, do_pv=True):
        if do_scores:
            w = q_weights(u)
            mx = jnp.full((8, cols), -jnp.inf, F32)
        if do_pv:
            g_pv = (u - 2) & 1
            acc = jnp.zeros((vrows, cols), F32)
        for c in range(n_c):
            rows = pl.ds(c * tk, tk)
            m_c = m_prev
            if do_pv:
                acc = acc + jnp.dot(vx_ref[g_pv, :, rows], p_ref[rows, :],
                                    preferred_element_type=F32)
                if do_probs:
                    guard = sum(_exact_zero(acc[0:8, t * 2 * LANES:(t * 2 + 1) * LANES])
                                for t in range(cols // (2 * LANES)))
                    m_c = m_prev + jnp.tile(guard[0:1, :], (1, cols // LANES))
            if do_probs:
                p_ref[rows, :] = jnp.exp2(s_ref[rows, :] - m_c).astype(BF16)
            if do_scores:
                s = jnp.dot(k_ref[rows, :], w, preferred_element_type=F32)
                s_ref[rows, :] = s
                mx = jnp.maximum(mx, jnp.max(s.reshape(tk // 8, 8, cols), axis=0))
        if do_pv:
            o_t = acc[0:HEAD_DIM, :] / acc[HEAD_DIM:HEAD_DIM + 1, :]
            o = o_t.T
            blk = jnp.concatenate([o[h * tq:(h + 1) * tq, :] for h in range(G)], axis=1)
            r0 = pl.multiple_of(((u - 2) >> 1) * tq, tq)
            o_ref[g_pv, pl.ds(r0, tq), :] = blk.astype(o_ref.dtype)
        if do_scores:
            return jnp.max(mx, axis=0, keepdims=True)
        return m_prev

    zero = jnp.int32(0)
    m = stage(zero, None, do_probs=False, do_pv=False)
    m = stage(zero + 1, m, do_pv=False)
    m = lax.fori_loop(2, n_units, stage, m)
    m = stage(zero + n_units, m, do_scores=False)
    stage(zero + n_units + 1, m, do_scores=False, do_probs=False)


def _attn_call(qt, k, vt):
    B, _, S = qt.shape
    G = N_Q_HEADS // N_KV_HEADS
    cols = G * ATTN_TQ
    gw = G * HEAD_DIM
    whole = lambda b: (b, 0, 0)
    return pl.pallas_call(
        _attn_kernel,
        grid=(B,),
        in_specs=[
            pl.BlockSpec((None, ATTN_WIDTH, S), whole),
            pl.BlockSpec((None, S, KV_WIDTH), whole),
            pl.BlockSpec((None, KV_WIDTH, S), whole),
        ],
        out_specs=pl.BlockSpec((None, N_KV_HEADS, S, gw), lambda b: (b, 0, 0, 0)),
        out_shape=jax.ShapeDtypeStruct((B, N_KV_HEADS, S, gw), BF16),
        scratch_shapes=[
            pltpu.VMEM((S, cols), F32),
            pltpu.VMEM((S, cols), BF16),
            pltpu.VMEM((N_KV_HEADS, HEAD_DIM + 16, S), BF16),
        ],
        compiler_params=pltpu.CompilerParams(
            dimension_semantics=("arbitrary",),
            vmem_limit_bytes=VMEM_LIMIT),
        name="attention",
    )(qt, k, vt)


def _hgrn_kernel(q_ref, i_ref, lff_ref, lfb_ref, g_ref, nw_ref, o_ref,
                 qm_ref, km_ref, qb_ref, kd_ref, dec_ref, ut_ref, st_ref):
    S = q_ref.shape[0]
    C = HG_CHUNK
    K = HG_EXPAND
    n_chunks = S // C
    nb = HG_BLOCK
    rows_b = nb * C
    r = lax.broadcasted_iota(jnp.int32, (C, C), 0)
    c = lax.broadcasted_iota(jnp.int32, (C, C), 1)
    lower = r >= c
    upper = r <= c
    tri2 = [jnp.concatenate([t, t], axis=1) for t in
            (jnp.where(lower, 1.0, 0.0).astype(BF16), jnp.where(upper, 1.0, 0.0).astype(BF16))]
    mid = (C // 2, C - 1 - C // 2)
    last = (C - 1, 0)

    def block_rows(t):
        return t * rows_b if isinstance(t, int) else pl.multiple_of(t * rows_b, rows_b)

    def prefix_sums(t):
        rows = pl.ds(block_rows(t), rows_b)
        lfs = (lff_ref[rows, :], lfb_ref[rows, :])
        b3s = []
        for d, lf in enumerate(lfs):
            hi = lf.astype(BF16)
            lo = (lf - hi.astype(F32)).astype(BF16)
            b3s.append(jnp.stack([
                jnp.dot(tri2[d],
                        jnp.concatenate([hi[j * C:(j + 1) * C], lo[j * C:(j + 1) * C]], axis=0),
                        preferred_element_type=F32)
                for j in range(nb)]))
        return lfs, b3s

    def state_updates(t):
        r0 = block_rows(t)
        for j in range(nb):
            cr = pl.ds(r0 + j * C, C)
            ivt = i_ref[cr, :].astype(F32).T.astype(BF16)
            ut_ref[t * nb + j] = jnp.dot(ivt, kd_ref[cr, :], preferred_element_type=F32)

    def scaled_operands(t, lfs, b3s):
        rows = pl.ds(block_rows(t), rows_b)
        q3 = q_ref[rows, :].astype(F32).reshape(nb, C, K)
        for d, (lf, b3) in enumerate(zip(lfs, b3s)):
            k3 = (1.0 - jnp.exp2(lf)).reshape(nb, C, K)
            bm = b3[:, mid[d]:mid[d] + 1, :]
            bl = b3[:, last[d]:last[d] + 1, :]
            e = jnp.exp2(b3 - bm)
            einv = 1.0 / e
            qm = q3 * e
            km = k3 * einv
            qm_ref[d, rows, :] = qm.reshape(rows_b, K).astype(BF16)
            km_ref[d, rows, :] = km.reshape(rows_b, K).astype(BF16)
            qb_ref[rows, d * K:(d + 1) * K] = (qm * jnp.exp2(bm)).reshape(rows_b, K).astype(BF16)
            kd_ref[rows, d * K:(d + 1) * K] = (km * jnp.exp2(bl - bm)).reshape(rows_b, K).astype(BF16)
            c0 = t * nb if isinstance(t, int) else pl.multiple_of(t * nb, nb)
            dec_ref[d, pl.ds(c0, nb), :] = jnp.exp2(bl).reshape(nb, K)

    n_blocks = n_chunks // nb
    scaled_operands(0, *prefix_sums(0))

    def prep(t, _):
        sums = prefix_sums(t)
        state_updates(t - 1)
        scaled_operands(t, *sums)
        return 0

    lax.fori_loop(1, n_blocks, prep, 0)
    state_updates(n_blocks - 1)

    def chain(n, states):
        sf, sb = states
        m = n_chunks - 1 - n
        st_ref[n, :, 0:K] = sf.astype(BF16)
        st_ref[m, :, K:2 * K] = sb.astype(BF16)
        return (dec_ref[0, pl.ds(n, 1), :] * sf + ut_ref[n, :, 0:K],
                dec_ref[1, pl.ds(m, 1), :] * sb + ut_ref[m, :, K:2 * K])

    zero_state = jnp.zeros((K, K), F32)
    lax.fori_loop(0, n_chunks, chain, (zero_state, zero_state))

    nw = nw_ref[...]
    nt_dims = (((1,), (1,)), ((), ()))

    no = HG_BLOCK_OUT

    def outp(t, _):
        r0 = pl.multiple_of(t * (no * C), no * C)
        crs = [pl.ds(r0 + j * C, C) for j in range(no)]
        a_f = [lax.dot_general(qm_ref[0, cr, :], km_ref[0, cr, :], nt_dims,
                               preferred_element_type=F32) for cr in crs]
        a_b = [lax.dot_general(qm_ref[1, cr, :], km_ref[1, cr, :], nt_dims,
                               preferred_element_type=F32) for cr in crs]
        a = [(jnp.where(lower, f, 0.0) + jnp.where(upper, b, 0.0)).astype(BF16)
             for f, b in zip(a_f, a_b)]
        outs = [jnp.dot(a[j], i_ref[cr, :], preferred_element_type=F32)
                + lax.dot_general(qb_ref[cr, :], st_ref[t * no + j], nt_dims,
                                  preferred_element_type=F32)
                for j, cr in enumerate(crs)]
        for cr, o in zip(crs, outs):
            ms = jnp.mean(o * o, axis=-1, keepdims=True)
            y = o * lax.rsqrt(ms + RMS_EPS) * nw
            o_ref[cr, :] = (y * g_ref[cr, :].astype(F32)).astype(o_ref.dtype)
        return 0

    lax.fori_loop(0, n_chunks // no, outp, 0)


def _hgrn_call(hq, hi, lf, hg, nw):
    B, S, _ = hq.shape
    n_chunks = S // HG_CHUNK
    K = HG_EXPAND
    blk = lambda b, h: (b, 0, h)
    return pl.pallas_call(
        _hgrn_kernel,
        grid=(B, HG_HEADS),
        in_specs=[
            pl.BlockSpec((None, S, K), blk),
            pl.BlockSpec((None, S, K), blk),
            pl.BlockSpec((None, S, K), blk),
            pl.BlockSpec((None, S, K), lambda b, h: (b, 0, HG_HEADS + h)),
            pl.BlockSpec((None, S, K), blk),
            pl.BlockSpec((1, K), lambda b, h: (0, 0)),
        ],
        out_specs=pl.BlockSpec((None, S, K), blk),
        out_shape=jax.ShapeDtypeStruct((B, S, HG_WIDTH), BF16),
        scratch_shapes=[
            pltpu.VMEM((2, S, K), BF16),
            pltpu.VMEM((2, S, K), BF16),
            pltpu.VMEM((S, 2 * K), BF16),
            pltpu.VMEM((S, 2 * K), BF16),
            pltpu.VMEM((2, n_chunks, K), F32),
            pltpu.VMEM((n_chunks, K, 2 * K), F32),
            pltpu.VMEM((n_chunks, K, 2 * K), BF16),
        ],
        compiler_params=pltpu.CompilerParams(
            dimension_semantics=("arbitrary", "arbitrary"),
            vmem_limit_bytes=VMEM_LIMIT),
        name="hgrn2",
    )(hq, hi, lf, lf, hg, nw)


def _out_kernel(x_ref, attn_ref, ag_ref, hgb_ref, anw_ref, w_ref, lnw_ref, lnb_ref, o_ref):
    a = jnp.concatenate([attn_ref[g] for g in range(N_KV_HEADS)], axis=1).astype(F32)
    ms = jnp.mean(a * a, axis=-1, keepdims=True)
    ab = a * lax.rsqrt(ms + RMS_EPS) * anw_ref[...] * ag_ref[...].astype(F32)
    y = (jnp.dot(ab.astype(BF16), w_ref[:ATTN_WIDTH, :], preferred_element_type=F32)
         + jnp.dot(hgb_ref[...], w_ref[ATTN_WIDTH:, :], preferred_element_type=F32))
    z = DEEPNORM_ALPHA * x_ref[...] + y
    mu = jnp.mean(z, axis=-1, keepdims=True)
    zc = z - mu
    var = jnp.mean(zc * zc, axis=-1, keepdims=True)
    o_ref[...] = zc * lax.rsqrt(var + LN_EPS) * lnw_ref[...] + lnb_ref[...]


def _out_call(x, attn, ag, hgb, anw, w_out_b, lnw, lnb):
    B, S, D = x.shape
    tm = OUT_TM
    row = lambda b, i: (b, i, 0)
    const2 = lambda b, i: (0, 0)
    return pl.pallas_call(
        _out_kernel,
        grid=(B, S // tm),
        in_specs=[
            pl.BlockSpec((None, tm, D), row),
            pl.BlockSpec((None, N_KV_HEADS, tm, ATTN_WIDTH // N_KV_HEADS), lambda b, i: (b, 0, i, 0)),
            pl.BlockSpec((None, tm, ATTN_WIDTH), row),
            pl.BlockSpec((None, tm, HG_WIDTH), row),
            pl.BlockSpec((1, ATTN_WIDTH), const2),
            pl.BlockSpec((ATTN_WIDTH + HG_WIDTH, D), const2),
            pl.BlockSpec((1, D), const2),
            pl.BlockSpec((1, D), const2),
        ],
        out_specs=pl.BlockSpec((None, tm, D), row),
        out_shape=jax.ShapeDtypeStruct((B, S, D), x.dtype),
        compiler_params=pltpu.CompilerParams(
            dimension_semantics=("arbitrary", "arbitrary"),
            vmem_limit_bytes=VMEM_LIMIT),
        name="out_proj_ln",
    )(x, attn, ag, hgb, anw, w_out_b, lnw, lnb)


def _rope_tables(seq_len):
    n_rows = seq_len // GRID_W
    inv = ROPE_THETA ** (-jnp.arange(0, AXIS_DIM, 2, dtype=F32) / AXIS_DIM)
    lane = jnp.arange(LANES, dtype=jnp.int32)
    d = lane % HEAD_DIM
    use_col = (d // AXIS_DIM) == 1
    dd = d % AXIS_DIM
    first = dd < (AXIS_DIM // 2)
    freq = inv[dd % (AXIS_DIM // 2)]
    idx = jnp.arange(max(n_rows, GRID_W), dtype=jnp.int32).astype(F32)
    ang = idx[:, None] * freq[None, :]
    cos_i = jnp.cos(ang)
    sin_i = jnp.where(first[None, :], -jnp.sin(ang), jnp.sin(ang))
    cos_i, sin_i = lax.optimization_barrier((cos_i, sin_i))

    def expand(tab):
        by_row = jnp.repeat(tab[:n_rows], GRID_W, axis=0)
        by_col = jnp.tile(tab[:GRID_W], (n_rows, 1))
        return jnp.where(use_col[None, :], by_col, by_row)

    return expand(cos_i), expand(sin_i)


def kernel(x, w_in, q_norm_w, k_norm_w, attn_norm_w, hg_lb_logits, hg_norm_w, w_out, ln_w, ln_b):
    B, S, D = x.shape
    assert w_in.shape == (DEPTH, D, IN_WIDTH) and DEPTH == 1
    cos_t, sin_t = _rope_tables(S)
    qw = jnp.tile(q_norm_w[0].astype(F32), N_Q_HEADS).reshape(1, ATTN_WIDTH)
    kw = jnp.tile(k_norm_w[0].astype(F32), N_KV_HEADS).reshape(1, KV_WIDTH)
    lbl = hg_lb_logits.astype(F32).reshape(2 * (DEPTH + 1), HG_WIDTH)
    hid = jnp.arange(ATTN_WIDTH, dtype=jnp.int32) // HEAD_DIM
    seg = jnp.where(hid[:, None] == hid[None, :], 1.0 / HEAD_DIM, 0.0).astype(BF16)

    qt, k, vt, ag, hq, hi, lf, hg = _proj_call(
        x, w_in[0].astype(BF16), cos_t, sin_t, qw, kw, lbl, seg)
    attn = _attn_call(qt, k, vt)
    hgb = _hgrn_call(hq, hi, lf, hg, hg_norm_w[0].astype(F32).reshape(1, HG_EXPAND))
    return _out_call(
        x, attn, ag, hgb,
        attn_norm_w[0].astype(F32).reshape(1, ATTN_WIDTH),
        w_out[0].astype(BF16),
        ln_w[0].astype(F32).reshape(1, D),
        ln_b[0].astype(F32).reshape(1, D))
```

```python
import jax
import jax.numpy as jnp
from jax import lax
from jax.experimental import pallas as pl
from jax.experimental.pallas import tpu as pltpu

F32 = jnp.float32
BF16 = jnp.bfloat16

GRID_W = 64
ATTN_WIDTH = 512
HEAD_DIM = 64
N_Q_HEADS = 8
N_KV_HEADS = 2
KV_WIDTH = 128
AXIS_DIM = 32
ROPE_THETA = 10000.0
HG_WIDTH = 512
HG_EXPAND = 128
HG_HEADS = 4
HG_CHUNK = 64
RMS_EPS = 1e-6
LN_EPS = 1e-5
DEPTH = 1
DEEPNORM_ALPHA = (2 * DEPTH) ** 0.25
IN_WIDTH = 3840
LOG2E = 1.4426950408889634
Q_SCALE = HEAD_DIM ** -0.5 * LOG2E

LANES = 128
BF16_SUBLANES = 16
VMEM_LIMIT = 56 * 1024 * 1024

PROJ_TM = 512
ATTN_TQ = 128
ATTN_TK = 1024
HG_BLOCK = 16
HG_BLOCK_OUT = 64
OUT_TM = 1024


def _exact_zero(x):
    bits = lax.bitcast_convert_type(x, jnp.uint32)
    return ((bits >> 16) >> 16).astype(F32)


def _sigmoid(x):
    return 1.0 / (1.0 + jnp.exp(-x))


def _proj_kernel(x_ref, w_ref, cos_ref, sin_ref, qw_ref, kw_ref, lbl_ref, seg_ref,
                 qt_ref, k_ref, vt_ref, ag_ref, hq_ref, hi_ref, lf_ref, hg_ref):
    xb = x_ref[...].astype(BF16)
    tm = xb.shape[0]

    def proj(a, b):
        return jnp.dot(xb, w_ref[:, a:b], preferred_element_type=F32)

    cosv = cos_ref[...]
    sinv = sin_ref[...]
    lane = lax.broadcasted_iota(jnp.int32, (tm, LANES), 1)
    first_half = (lane % AXIS_DIM) < (AXIS_DIM // 2)

    def norm_rope(a, w_row, seg, scale):
        sq = (a * a).astype(BF16)
        blk = min(a.shape[1], 2 * LANES)
        ms = jnp.concatenate(
            [jnp.dot(sq[:, j:j + blk], seg[j:j + blk, j:j + blk], preferred_element_type=F32)
             for j in range(0, a.shape[1], blk)], axis=1)
        y = a * lax.rsqrt(ms + RMS_EPS) * w_row
        outs = []
        for c in range(a.shape[1] // LANES):
            yc = y[:, c * LANES:(c + 1) * LANES]
            partner = jnp.where(first_half,
                                pltpu.roll(yc, LANES - AXIS_DIM // 2, 1),
                                pltpu.roll(yc, AXIS_DIM // 2, 1))
            outs.append((yc * cosv + partner * sinv) * scale)
        return outs

    aq = proj(0, ATTN_WIDTH)
    qs = norm_rope(aq, qw_ref[...], seg_ref[...], Q_SCALE)
    for c, qc in enumerate(qs):
        qt_ref[c * LANES:(c + 1) * LANES, :] = qc.T.astype(BF16)
    akv = proj(ATTN_WIDTH, ATTN_WIDTH + 2 * KV_WIDTH)
    kk = norm_rope(akv[:, :KV_WIDTH], kw_ref[...], seg_ref[:KV_WIDTH, :KV_WIDTH], 1.0)[0]
    k_ref[...] = kk.astype(BF16)
    vt_ref[...] = akv[:, KV_WIDTH:].T.astype(BF16)

    base = ATTN_WIDTH + 2 * KV_WIDTH
    ag = proj(base, base + ATTN_WIDTH)
    ag_ref[...] = (ag * _sigmoid(ag)).astype(BF16)
    base += ATTN_WIDTH
    hq = proj(base, base + HG_WIDTH)
    hq_ref[...] = (hq * _sigmoid(hq) * (HG_EXPAND ** -0.5)).astype(BF16)
    base += HG_WIDTH
    hi_ref[...] = proj(base, base + HG_WIDTH).astype(BF16)
    base += HG_WIDTH
    lbl = lbl_ref[...]
    for d in range(2):
        l0 = lbl[2 * d:2 * d + 1, :]
        l1 = lbl[2 * d + 1:2 * d + 2, :]
        mx = jnp.maximum(l0, l1)
        e0 = jnp.exp(l0 - mx)
        e1 = jnp.exp(l1 - mx)
        lb = e0 / (e0 + e1)
        z = proj(base, base + HG_WIDTH)
        lf_ref[:, d * HG_WIDTH:(d + 1) * HG_WIDTH] = jnp.log(lb + (1.0 - lb) * _sigmoid(z)) * LOG2E
        base += HG_WIDTH
    hg = proj(base, base + HG_WIDTH)
    hg_ref[...] = (hg * _sigmoid(hg)).astype(BF16)


def _proj_call(x, w_in_b, cos_t, sin_t, qw, kw, lbl, seg):
    B, S, D = x.shape
    tm = PROJ_TM
    nt = S // tm
    row = lambda b, i: (b, i, 0)
    const2 = lambda b, i: (0, 0)
    out_shape = (
        jax.ShapeDtypeStruct((B, ATTN_WIDTH, S), BF16),
        jax.ShapeDtypeStruct((B, S, KV_WIDTH), BF16),
        jax.ShapeDtypeStruct((B, KV_WIDTH, S), BF16),
        jax.ShapeDtypeStruct((B, S, ATTN_WIDTH), BF16),
        jax.ShapeDtypeStruct((B, S, HG_WIDTH), BF16),
        jax.ShapeDtypeStruct((B, S, HG_WIDTH), BF16),
        jax.ShapeDtypeStruct((B, S, 2 * HG_WIDTH), F32),
        jax.ShapeDtypeStruct((B, S, HG_WIDTH), BF16),
    )
    in_specs = [
        pl.BlockSpec((None, tm, D), row),
        pl.BlockSpec((D, IN_WIDTH), const2),
        pl.BlockSpec((tm, LANES), lambda b, i: (i, 0)),
        pl.BlockSpec((tm, LANES), lambda b, i: (i, 0)),
        pl.BlockSpec((1, ATTN_WIDTH), const2),
        pl.BlockSpec((1, KV_WIDTH), const2),
        pl.BlockSpec((4, HG_WIDTH), const2),
        pl.BlockSpec((ATTN_WIDTH, ATTN_WIDTH), const2),
    ]
    out_specs = (
        pl.BlockSpec((None, ATTN_WIDTH, tm), lambda b, i: (b, 0, i)),
        pl.BlockSpec((None, tm, KV_WIDTH), row),
        pl.BlockSpec((None, KV_WIDTH, tm), lambda b, i: (b, 0, i)),
        pl.BlockSpec((None, tm, ATTN_WIDTH), row),
        pl.BlockSpec((None, tm, HG_WIDTH), row),
        pl.BlockSpec((None, tm, HG_WIDTH), row),
        pl.BlockSpec((None, tm, 2 * HG_WIDTH), row),
        pl.BlockSpec((None, tm, HG_WIDTH), row),
    )
    return pl.pallas_call(
        _proj_kernel,
        grid=(B, nt),
        in_specs=in_specs,
        out_specs=out_specs,
        out_shape=out_shape,
        compiler_params=pltpu.CompilerParams(
            dimension_semantics=("arbitrary", "arbitrary"),
            vmem_limit_bytes=VMEM_LIMIT),
        name="in_proj",
    )(x, w_in_b, cos_t, sin_t, qw, kw, lbl, seg)


def _attn_kernel(qt_ref, k_ref, vt_ref, o_ref, s_ref, p_ref, vx_ref):
    S = k_ref.shape[0]
    G = N_Q_HEADS // N_KV_HEADS
    tq, tk = ATTN_TQ, ATTN_TK
    cols = G * tq
    n_units = N_KV_HEADS * (S // tq)
    n_c = S // tk
    vrows = vx_ref.shape[1]

    for g in range(N_KV_HEADS):
        vx_ref[g, 0:HEAD_DIM, :] = vt_ref[g * HEAD_DIM:(g + 1) * HEAD_DIM, :]
        vx_ref[g, HEAD_DIM:vrows, :] = jnp.ones((vrows - HEAD_DIM, S), BF16)

    row_group = lax.broadcasted_iota(jnp.int32, (N_KV_HEADS * HEAD_DIM, cols), 0) // HEAD_DIM

    def q_weights(u):
        i, g = u >> 1, u & 1
        c0 = pl.multiple_of(i * tq, tq)
        qs = jnp.concatenate(
            [qt_ref[pl.ds(pl.multiple_of((G * g + h) * HEAD_DIM, HEAD_DIM), HEAD_DIM),
                    pl.ds(c0, tq)] for h in range(G)], axis=1)
        return jnp.where(row_group == g, jnp.concatenate([qs, qs], axis=0), 0.0).astype(BF16)

    def stage(u, m_prev, do_scores=True, do_probs=True, do_pv=True):
        if do_scores:
            w = q_weights(u)
            mx = jnp.full((8, cols), -jnp.inf, F32)
        if do_pv:
            g_pv = (u - 2) & 1
            acc = jnp.zeros((vrows, cols), F32)
        for c in range(n_c):
            rows = pl.ds(c * tk, tk)
            m_c = m_prev
            if do_pv:
                acc = acc + jnp.dot(vx_ref[g_pv, :, rows], p_ref[rows, :],
                                    preferred_element_type=F32)
                if do_probs:
                    guard = sum(_exact_zero(acc[0:8, t * 2 * LANES:(t * 2 + 1) * LANES])
                                for t in range(cols // (2 * LANES)))
                    m_c = m_prev + jnp.tile(guard[0:1, :], (1, cols // LANES))
            if do_probs:
                p_ref[rows, :] = jnp.exp2(s_ref[rows, :] - m_c).astype(BF16)
            if do_scores:
                s = jnp.dot(k_ref[rows, :], w, preferred_element_type=F32)
                s_ref[rows, :] = s
                mx = jnp.maximum(mx, jnp.max(s.reshape(tk // 8, 8, cols), axis=0))
        if do_pv:
            o_t = acc[0:HEAD_DIM, :] / acc[HEAD_DIM:HEAD_DIM + 1, :]
            o = o_t.T
            blk = jnp.concatenate([o[h * tq:(h + 1) * tq, :] for h in range(G)], axis=1)
            r0 = pl.multiple_of(((u - 2) >> 1) * tq, tq)
            o_ref[g_pv, pl.ds(r0, tq), :] = blk.astype(o_ref.dtype)
        if do_scores:
            return jnp.max(mx, axis=0, keepdims=True)
        return m_prev

    zero = jnp.int32(0)
    m = stage(zero, None, do_probs=False, do_pv=False)
    m = stage(zero + 1, m, do_pv=False)
    m = lax.fori_loop(2, n_units, stage, m)
    m = stage(zero + n_units, m, do_scores=False)
    stage(zero + n_units + 1, m, do_scores=False, do_probs=False)


def _attn_call(qt, k, vt):
    B, _, S = qt.shape
    G = N_Q_HEADS // N_KV_HEADS
    cols = G * ATTN_TQ
    gw = G * HEAD_DIM
    whole = lambda b: (b, 0, 0)
    return pl.pallas_call(
        _attn_kernel,
        grid=(B,),
        in_specs=[
            pl.BlockSpec((None, ATTN_WIDTH, S), whole),
            pl.BlockSpec((None, S, KV_WIDTH), whole),
            pl.BlockSpec((None, KV_WIDTH, S), whole),
        ],
        out_specs=pl.BlockSpec((None, N_KV_HEADS, S, gw), lambda b: (b, 0, 0, 0)),
        out_shape=jax.ShapeDtypeStruct((B, N_KV_HEADS, S, gw), BF16),
        scratch_shapes=[
            pltpu.VMEM((S, cols), F32),
            pltpu.VMEM((S, cols), BF16),
            pltpu.VMEM((N_KV_HEADS, HEAD_DIM + BF16_SUBLANES, S), BF16),
        ],
        compiler_params=pltpu.CompilerParams(
            dimension_semantics=("arbitrary",),
            vmem_limit_bytes=VMEM_LIMIT),
        name="attention",
    )(qt, k, vt)


def _hgrn_kernel(q_ref, i_ref, lff_ref, lfb_ref, g_ref, nw_ref, o_ref,
                 qm_ref, km_ref, qb_ref, kd_ref, dec_ref, ut_ref, st_ref):
    S = q_ref.shape[0]
    C = HG_CHUNK
    K = HG_EXPAND
    n_chunks = S // C
    nb = HG_BLOCK
    rows_b = nb * C
    r = lax.broadcasted_iota(jnp.int32, (C, C), 0)
    c = lax.broadcasted_iota(jnp.int32, (C, C), 1)
    lower = r >= c
    upper = r <= c
    tri2 = [jnp.concatenate([t, t], axis=1) for t in
            (jnp.where(lower, 1.0, 0.0).astype(BF16), jnp.where(upper, 1.0, 0.0).astype(BF16))]
    mid = (C // 2, C - 1 - C // 2)
    last = (C - 1, 0)

    def block_rows(t):
        return t * rows_b if isinstance(t, int) else pl.multiple_of(t * rows_b, rows_b)

    def prefix_sums(t):
        rows = pl.ds(block_rows(t), rows_b)
        lfs = (lff_ref[rows, :], lfb_ref[rows, :])
        b3s = []
        for d, lf in enumerate(lfs):
            hi = lf.astype(BF16)
            lo = (lf - hi.astype(F32)).astype(BF16)
            b3s.append(jnp.stack([
                jnp.dot(tri2[d],
                        jnp.concatenate([hi[j * C:(j + 1) * C], lo[j * C:(j + 1) * C]], axis=0),
                        preferred_element_type=F32)
                for j in range(nb)]))
        return lfs, b3s

    def state_updates(t):
        r0 = block_rows(t)
        for j in range(nb):
            cr = pl.ds(r0 + j * C, C)
            ivt = i_ref[cr, :].astype(F32).T.astype(BF16)
            ut_ref[t * nb + j] = jnp.dot(ivt, kd_ref[cr, :], preferred_element_type=F32)

    def scaled_operands(t, lfs, b3s):
        rows = pl.ds(block_rows(t), rows_b)
        q3 = q_ref[rows, :].astype(F32).reshape(nb, C, K)
        for d, (lf, b3) in enumerate(zip(lfs, b3s)):
            k3 = (1.0 - jnp.exp2(lf)).reshape(nb, C, K)
            bm = b3[:, mid[d]:mid[d] + 1, :]
            bl = b3[:, last[d]:last[d] + 1, :]
            e = jnp.exp2(b3 - bm)
            einv = 1.0 / e
            qm = q3 * e
            km = k3 * einv
            qm_ref[d, rows, :] = qm.reshape(rows_b, K).astype(BF16)
            km_ref[d, rows, :] = km.reshape(rows_b, K).astype(BF16)
            qb_ref[rows, d * K:(d + 1) * K] = (qm * jnp.exp2(bm)).reshape(rows_b, K).astype(BF16)
            kd_ref[rows, d * K:(d + 1) * K] = (km * jnp.exp2(bl - bm)).reshape(rows_b, K).astype(BF16)
            c0 = t * nb if isinstance(t, int) else pl.multiple_of(t * nb, nb)
            dec_ref[d, pl.ds(c0, nb), :] = jnp.exp2(bl).reshape(nb, K)

    n_blocks = n_chunks // nb
    scaled_operands(0, *prefix_sums(0))

    def prep(t, _):
        sums = prefix_sums(t)
        state_updates(t - 1)
        scaled_operands(t, *sums)
        return 0

    lax.fori_loop(1, n_blocks, prep, 0)
    state_updates(n_blocks - 1)

    def chain(n, states):
        sf, sb = states
        m = n_chunks - 1 - n
        st_ref[n, :, 0:K] = sf.astype(BF16)
        st_ref[m, :, K:2 * K] = sb.astype(BF16)
        return (dec_ref[0, pl.ds(n, 1), :] * sf + ut_ref[n, :, 0:K],
                dec_ref[1, pl.ds(m, 1), :] * sb + ut_ref[m, :, K:2 * K])

    zero_state = jnp.zeros((K, K), F32)
    lax.fori_loop(0, n_chunks, chain, (zero_state, zero_state))

    nw = nw_ref[...]
    nt_dims = (((1,), (1,)), ((), ()))

    no = HG_BLOCK_OUT

    def outp(t, _):
        r0 = pl.multiple_of(t * (no * C), no * C)
        crs = [pl.ds(r0 + j * C, C) for j in range(no)]
        a_f = [lax.dot_general(qm_ref[0, cr, :], km_ref[0, cr, :], nt_dims,
                               preferred_element_type=F32) for cr in crs]
        a_b = [lax.dot_general(qm_ref[1, cr, :], km_ref[1, cr, :], nt_dims,
                               preferred_element_type=F32) for cr in crs]
        a = [(jnp.where(lower, f, 0.0) + jnp.where(upper, b, 0.0)).astype(BF16)
             for f, b in zip(a_f, a_b)]
        outs = [jnp.dot(a[j], i_ref[cr, :], preferred_element_type=F32)
                + lax.dot_general(qb_ref[cr, :], st_ref[t * no + j], nt_dims,
                                  preferred_element_type=F32)
                for j, cr in enumerate(crs)]
        for cr, o in zip(crs, outs):
            ms = jnp.mean(o * o, axis=-1, keepdims=True)
            y = o * lax.rsqrt(ms + RMS_EPS) * nw
            o_ref[cr, :] = (y * g_ref[cr, :].astype(F32)).astype(o_ref.dtype)
        return 0

    lax.fori_loop(0, n_chunks // no, outp, 0)


def _hgrn_call(hq, hi, lf, hg, nw):
    B, S, _ = hq.shape
    n_chunks = S // HG_CHUNK
    K = HG_EXPAND
    blk = lambda b, h: (b, 0, h)
    return pl.pallas_call(
        _hgrn_kernel,
        grid=(B, HG_HEADS),
        in_specs=[
            pl.BlockSpec((None, S, K), blk),
            pl.BlockSpec((None, S, K), blk),
            pl.BlockSpec((None, S, K), blk),
            pl.BlockSpec((None, S, K), lambda b, h: (b, 0, HG_HEADS + h)),
            pl.BlockSpec((None, S, K), blk),
            pl.BlockSpec((1, K), lambda b, h: (0, 0)),
        ],
        out_specs=pl.BlockSpec((None, S, K), blk),
        out_shape=jax.ShapeDtypeStruct((B, S, HG_WIDTH), BF16),
        scratch_shapes=[
            pltpu.VMEM((2, S, K), BF16),
            pltpu.VMEM((2, S, K), BF16),
            pltpu.VMEM((S, 2 * K), BF16),
            pltpu.VMEM((S, 2 * K), BF16),
            pltpu.VMEM((2, n_chunks, K), F32),
            pltpu.VMEM((n_chunks, K, 2 * K), F32),
            pltpu.VMEM((n_chunks, K, 2 * K), BF16),
        ],
        compiler_params=pltpu.CompilerParams(
            dimension_semantics=("arbitrary", "arbitrary"),
            vmem_limit_bytes=VMEM_LIMIT),
        name="hgrn2",
    )(hq, hi, lf, lf, hg, nw)


def _out_kernel(x_ref, attn_ref, ag_ref, hgb_ref, anw_ref, w_ref, lnw_ref, lnb_ref, o_ref):
    a = jnp.concatenate([attn_ref[g] for g in range(N_KV_HEADS)], axis=1).astype(F32)
    ms = jnp.mean(a * a, axis=-1, keepdims=True)
    ab = a * lax.rsqrt(ms + RMS_EPS) * anw_ref[...] * ag_ref[...].astype(F32)
    y = (jnp.dot(ab.astype(BF16), w_ref[:ATTN_WIDTH, :], preferred_element_type=F32)
         + jnp.dot(hgb_ref[...], w_ref[ATTN_WIDTH:, :], preferred_element_type=F32))
    z = DEEPNORM_ALPHA * x_ref[...] + y
    mu = jnp.mean(z, axis=-1, keepdims=True)
    zc = z - mu
    var = jnp.mean(zc * zc, axis=-1, keepdims=True)
    o_ref[...] = zc * lax.rsqrt(var + LN_EPS) * lnw_ref[...] + lnb_ref[...]


def _out_call(x, attn, ag, hgb, anw, w_out_b, lnw, lnb):
    B, S, D = x.shape
    tm = OUT_TM
    row = lambda b, i: (b, i, 0)
    const2 = lambda b, i: (0, 0)
    return pl.pallas_call(
        _out_kernel,
        grid=(B, S // tm),
        in_specs=[
            pl.BlockSpec((None, tm, D), row),
            pl.BlockSpec((None, N_KV_HEADS, tm, ATTN_WIDTH // N_KV_HEADS), lambda b, i: (b, 0, i, 0)),
            pl.BlockSpec((None, tm, ATTN_WIDTH), row),
            pl.BlockSpec((None, tm, HG_WIDTH), row),
            pl.BlockSpec((1, ATTN_WIDTH), const2),
            pl.BlockSpec((ATTN_WIDTH + HG_WIDTH, D), const2),
            pl.BlockSpec((1, D), const2),
            pl.BlockSpec((1, D), const2),
        ],
        out_specs=pl.BlockSpec((None, tm, D), row),
        out_shape=jax.ShapeDtypeStruct((B, S, D), x.dtype),
        compiler_params=pltpu.CompilerParams(
            dimension_semantics=("arbitrary", "arbitrary"),
            vmem_limit_bytes=VMEM_LIMIT),
        name="out_proj_ln",
    )(x, attn, ag, hgb, anw, w_out_b, lnw, lnb)


def _rope_tables(seq_len):
    n_rows = seq_len // GRID_W
    inv = ROPE_THETA ** (-jnp.arange(0, AXIS_DIM, 2, dtype=F32) / AXIS_DIM)
    lane = jnp.arange(LANES, dtype=jnp.int32)
    d = lane % HEAD_DIM
    use_col = (d // AXIS_DIM) == 1
    dd = d % AXIS_DIM
    first = dd < (AXIS_DIM // 2)
    freq = inv[dd % (AXIS_DIM // 2)]
    idx = jnp.arange(max(n_rows, GRID_W), dtype=jnp.int32).astype(F32)
    ang = idx[:, None] * freq[None, :]
    cos_i = jnp.cos(ang)
    sin_i = jnp.where(first[None, :], -jnp.sin(ang), jnp.sin(ang))
    cos_i, sin_i = lax.optimization_barrier((cos_i, sin_i))

    def expand(tab):
        by_row = jnp.repeat(tab[:n_rows], GRID_W, axis=0)
        by_col = jnp.tile(tab[:GRID_W], (n_rows, 1))
        return jnp.where(use_col[None, :], by_col, by_row)

    return expand(cos_i), expand(sin_i)


def kernel(x, w_in, q_norm_w, k_norm_w, attn_norm_w, hg_lb_logits, hg_norm_w, w_out, ln_w, ln_b):
    B, S, D = x.shape
    assert w_in.shape == (DEPTH, D, IN_WIDTH) and DEPTH == 1
    cos_t, sin_t = _rope_tables(S)
    qw = jnp.tile(q_norm_w[0].astype(F32), N_Q_HEADS).reshape(1, ATTN_WIDTH)
    kw = jnp.tile(k_norm_w[0].astype(F32), N_KV_HEADS).reshape(1, KV_WIDTH)
    lbl = hg_lb_logits.astype(F32).reshape(2 * (DEPTH + 1), HG_WIDTH)
    hid = jnp.arange(ATTN_WIDTH, dtype=jnp.int32) // HEAD_DIM
    seg = jnp.where(hid[:, None] == hid[None, :], 1.0 / HEAD_DIM, 0.0).astype(BF16)

    qt, k, vt, ag, hq, hi, lf, hg = _proj_call(
        x, w_in[0].astype(BF16), cos_t, sin_t, qw, kw, lbl, seg)
    attn = _attn_call(qt, k, vt)
    hgb = _hgrn_call(hq, hi, lf, hg, hg_norm_w[0].astype(F32).reshape(1, HG_EXPAND))
    return _out_call(
        x, attn, ag, hgb,
        attn_norm_w[0].astype(F32).reshape(1, ATTN_WIDTH),
        w_out[0].astype(BF16),
        ln_w[0].astype(F32).reshape(1, D),
        ln_b[0].astype(F32).reshape(1, D))
```

```python
import jax
import jax.numpy as jnp
from jax import lax
from jax.experimental import pallas as pl
from jax.experimental.pallas import tpu as pltpu

F32 = jnp.float32
BF16 = jnp.bfloat16

GRID_W = 64
ATTN_WIDTH = 512
HEAD_DIM = 64
N_Q_HEADS = 8
N_KV_HEADS = 2
KV_WIDTH = 128
AXIS_DIM = 32
ROPE_THETA = 10000.0
HG_WIDTH = 512
HG_EXPAND = 128
HG_HEADS = 4
HG_CHUNK = 64
RMS_EPS = 1e-6
LN_EPS = 1e-5
DEPTH = 1
DEEPNORM_ALPHA = (2 * DEPTH) ** 0.25
IN_WIDTH = 3840
LOG2E = 1.4426950408889634
Q_SCALE = HEAD_DIM ** -0.5 * LOG2E

LANES = 128
BF16_SUBLANES = 16
VMEM_LIMIT = 56 * 1024 * 1024

PROJ_TM = 512
ATTN_TQ = 128
ATTN_TK = 1024
HG_BLOCK = 32
HG_BLOCK_OUT = 64
OUT_TM = 1024


def _exact_zero(x):
    bits = lax.bitcast_convert_type(x, jnp.uint32)
    return ((bits >> 16) >> 16).astype(F32)


def _sigmoid(x):
    return 1.0 / (1.0 + jnp.exp(-x))


def _proj_kernel(x_ref, w_ref, cos_ref, sin_ref, qw_ref, kw_ref, lbl_ref, seg_ref,
                 qt_ref, k_ref, vt_ref, ag_ref, hq_ref, hi_ref, lf_ref, hg_ref):
    xb = x_ref[...].astype(BF16)
    tm = xb.shape[0]

    def proj(a, b):
        return jnp.dot(xb, w_ref[:, a:b], preferred_element_type=F32)

    cosv = cos_ref[...]
    sinv = sin_ref[...]
    lane = lax.broadcasted_iota(jnp.int32, (tm, LANES), 1)
    first_half = (lane % AXIS_DIM) < (AXIS_DIM // 2)

    def norm_rope(a, w_row, seg, scale):
        sq = (a * a).astype(BF16)
        blk = min(a.shape[1], 2 * LANES)
        ms = jnp.concatenate(
            [jnp.dot(sq[:, j:j + blk], seg[j:j + blk, j:j + blk], preferred_element_type=F32)
             for j in range(0, a.shape[1], blk)], axis=1)
        y = a * lax.rsqrt(ms + RMS_EPS) * w_row
        outs = []
        for c in range(a.shape[1] // LANES):
            yc = y[:, c * LANES:(c + 1) * LANES]
            partner = jnp.where(first_half,
                                pltpu.roll(yc, LANES - AXIS_DIM // 2, 1),
                                pltpu.roll(yc, AXIS_DIM // 2, 1))
            outs.append((yc * cosv + partner * sinv) * scale)
        return outs

    aq = proj(0, ATTN_WIDTH)
    qs = norm_rope(aq, qw_ref[...], seg_ref[...], Q_SCALE)
    for c, qc in enumerate(qs):
        qt_ref[c * LANES:(c + 1) * LANES, :] = qc.T.astype(BF16)
    akv = proj(ATTN_WIDTH, ATTN_WIDTH + 2 * KV_WIDTH)
    kk = norm_rope(akv[:, :KV_WIDTH], kw_ref[...], seg_ref[:KV_WIDTH, :KV_WIDTH], 1.0)[0]
    k_ref[...] = kk.astype(BF16)
    vt_ref[...] = akv[:, KV_WIDTH:].T.astype(BF16)

    base = ATTN_WIDTH + 2 * KV_WIDTH
    ag = proj(base, base + ATTN_WIDTH)
    ag_ref[...] = (ag * _sigmoid(ag)).astype(BF16)
    base += ATTN_WIDTH
    hq = proj(base, base + HG_WIDTH)
    hq_ref[...] = (hq * _sigmoid(hq) * (HG_EXPAND ** -0.5)).astype(BF16)
    base += HG_WIDTH
    hi_ref[...] = proj(base, base + HG_WIDTH).astype(BF16)
    base += HG_WIDTH
    lbl = lbl_ref[...]
    for d in range(2):
        l0 = lbl[2 * d:2 * d + 1, :]
        l1 = lbl[2 * d + 1:2 * d + 2, :]
        mx = jnp.maximum(l0, l1)
        e0 = jnp.exp(l0 - mx)
        e1 = jnp.exp(l1 - mx)
        lb = e0 / (e0 + e1)
        z = proj(base, base + HG_WIDTH)
        lf_ref[:, d * HG_WIDTH:(d + 1) * HG_WIDTH] = jnp.log(lb + (1.0 - lb) * _sigmoid(z)) * LOG2E
        base += HG_WIDTH
    hg = proj(base, base + HG_WIDTH)
    hg_ref[...] = (hg * _sigmoid(hg)).astype(BF16)


def _proj_call(x, w_in_b, cos_t, sin_t, qw, kw, lbl, seg):
    B, S, D = x.shape
    tm = PROJ_TM
    nt = S // tm
    row = lambda b, i: (b, i, 0)
    const2 = lambda b, i: (0, 0)
    out_shape = (
        jax.ShapeDtypeStruct((B, ATTN_WIDTH, S), BF16),
        jax.ShapeDtypeStruct((B, S, KV_WIDTH), BF16),
        jax.ShapeDtypeStruct((B, KV_WIDTH, S), BF16),
        jax.ShapeDtypeStruct((B, S, ATTN_WIDTH), BF16),
        jax.ShapeDtypeStruct((B, S, HG_WIDTH), BF16),
        jax.ShapeDtypeStruct((B, S, HG_WIDTH), BF16),
        jax.ShapeDtypeStruct((B, S, 2 * HG_WIDTH), F32),
        jax.ShapeDtypeStruct((B, S, HG_WIDTH), BF16),
    )
    in_specs = [
        pl.BlockSpec((None, tm, D), row),
        pl.BlockSpec((D, IN_WIDTH), const2),
        pl.BlockSpec((tm, LANES), lambda b, i: (i, 0)),
        pl.BlockSpec((tm, LANES), lambda b, i: (i, 0)),
        pl.BlockSpec((1, ATTN_WIDTH), const2),
        pl.BlockSpec((1, KV_WIDTH), const2),
        pl.BlockSpec((4, HG_WIDTH), const2),
        pl.BlockSpec((ATTN_WIDTH, ATTN_WIDTH), const2),
    ]
    out_specs = (
        pl.BlockSpec((None, ATTN_WIDTH, tm), lambda b, i: (b, 0, i)),
        pl.BlockSpec((None, tm, KV_WIDTH), row),
        pl.BlockSpec((None, KV_WIDTH, tm), lambda b, i: (b, 0, i)),
        pl.BlockSpec((None, tm, ATTN_WIDTH), row),
        pl.BlockSpec((None, tm, HG_WIDTH), row),
        pl.BlockSpec((None, tm, HG_WIDTH), row),
        pl.BlockSpec((None, tm, 2 * HG_WIDTH), row),
        pl.BlockSpec((None, tm, HG_WIDTH), row),
    )
    return pl.pallas_call(
        _proj_kernel,
        grid=(B, nt),
        in_specs=in_specs,
        out_specs=out_specs,
        out_shape=out_shape,
        compiler_params=pltpu.CompilerParams(
            dimension_semantics=("arbitrary", "arbitrary"),
            vmem_limit_bytes=VMEM_LIMIT),
        name="in_proj",
    )(x, w_in_b, cos_t, sin_t, qw, kw, lbl, seg)


def _attn_kernel(qt_ref, k_ref, vt_ref, o_ref, s_ref, p_ref, vx_ref):
    S = k_ref.shape[0]
    G = N_Q_HEADS // N_KV_HEADS
    tq, tk = ATTN_TQ, ATTN_TK
    cols = G * tq
    n_units = N_KV_HEADS * (S // tq)
    n_c = S // tk
    vrows = vx_ref.shape[1]

    for g in range(N_KV_HEADS):
        vx_ref[g, 0:HEAD_DIM, :] = vt_ref[g * HEAD_DIM:(g + 1) * HEAD_DIM, :]
        vx_ref[g, HEAD_DIM:vrows, :] = jnp.ones((vrows - HEAD_DIM, S), BF16)

    row_group = lax.broadcasted_iota(jnp.int32, (N_KV_HEADS * HEAD_DIM, cols), 0) // HEAD_DIM

    def q_weights(u):
        i, g = u >> 1, u & 1
        c0 = pl.multiple_of(i * tq, tq)
        qs = jnp.concatenate(
            [qt_ref[pl.ds(pl.multiple_of((G * g + h) * HEAD_DIM, HEAD_DIM), HEAD_DIM),
                    pl.ds(c0, tq)] for h in range(G)], axis=1)
        return jnp.where(row_group == g, jnp.concatenate([qs, qs], axis=0), 0.0).astype(BF16)

    def stage(u, m_prev, do_scores=True, do_probs=True, do_pv=True):
        if do_scores:
            w = q_weights(u)
            mx = jnp.full((8, cols), -jnp.inf, F32)
        if do_pv:
            g_pv = (u - 2) & 1
            acc = jnp.zeros((vrows, cols), F32)
        for c in range(n_c):
            rows = pl.ds(c * tk, tk)
            m_c = m_prev
            if do_pv:
                acc = acc + jnp.dot(vx_ref[g_pv, :, rows], p_ref[rows, :],
                                    preferred_element_type=F32)
                if do_probs:
                    guard = sum(_exact_zero(acc[0:8, t * 2 * LANES:(t * 2 + 1) * LANES])
                                for t in range(cols // (2 * LANES)))
                    m_c = m_prev + jnp.tile(guard[0:1, :], (1, cols // LANES))
            if do_probs:
                p_ref[rows, :] = jnp.exp2(s_ref[rows, :] - m_c).astype(BF16)
            if do_scores:
                s = jnp.dot(k_ref[rows, :], w, preferred_element_type=F32)
                s_ref[rows, :] = s
                mx = jnp.maximum(mx, jnp.max(s.reshape(tk // 8, 8, cols), axis=0))
        if do_pv:
            o_t = acc[0:HEAD_DIM, :] / acc[HEAD_DIM:HEAD_DIM + 1, :]
            o = o_t.T
            blk = jnp.concatenate([o[h * tq:(h + 1) * tq, :] for h in range(G)], axis=1)
            r0 = pl.multiple_of(((u - 2) >> 1) * tq, tq)
            o_ref[g_pv, pl.ds(r0, tq), :] = blk.astype(o_ref.dtype)
        if do_scores:
            return jnp.max(mx, axis=0, keepdims=True)
        return m_prev

    zero = jnp.int32(0)
    m = stage(zero, None, do_probs=False, do_pv=False)
    m = stage(zero + 1, m, do_pv=False)
    m = lax.fori_loop(2, n_units, stage, m)
    m = stage(zero + n_units, m, do_scores=False)
    stage(zero + n_units + 1, m, do_scores=False, do_probs=False)


def _attn_call(qt, k, vt):
    B, _, S = qt.shape
    G = N_Q_HEADS // N_KV_HEADS
    cols = G * ATTN_TQ
    gw = G * HEAD_DIM
    whole = lambda b: (b, 0, 0)
    return pl.pallas_call(
        _attn_kernel,
        grid=(B,),
        in_specs=[
            pl.BlockSpec((None, ATTN_WIDTH, S), whole),
            pl.BlockSpec((None, S, KV_WIDTH), whole),
            pl.BlockSpec((None, KV_WIDTH, S), whole),
        ],
        out_specs=pl.BlockSpec((None, N_KV_HEADS, S, gw), lambda b: (b, 0, 0, 0)),
        out_shape=jax.ShapeDtypeStruct((B, N_KV_HEADS, S, gw), BF16),
        scratch_shapes=[
            pltpu.VMEM((S, cols), F32),
            pltpu.VMEM((S, cols), BF16),
            pltpu.VMEM((N_KV_HEADS, HEAD_DIM + BF16_SUBLANES, S), BF16),
        ],
        compiler_params=pltpu.CompilerParams(
            dimension_semantics=("arbitrary",),
            vmem_limit_bytes=VMEM_LIMIT),
        name="attention",
    )(qt, k, vt)


def _hgrn_kernel(q_ref, i_ref, lff_ref, lfb_ref, g_ref, nw_ref, o_ref,
                 qm_ref, km_ref, qb_ref, kd_ref, dec_ref, ut_ref, st_ref):
    S = q_ref.shape[0]
    C = HG_CHUNK
    K = HG_EXPAND
    n_chunks = S // C
    nb = HG_BLOCK
    rows_b = nb * C
    r = lax.broadcasted_iota(jnp.int32, (C, C), 0)
    c = lax.broadcasted_iota(jnp.int32, (C, C), 1)
    lower = r >= c
    upper = r <= c
    tri2 = [jnp.concatenate([t, t], axis=1) for t in
            (jnp.where(lower, 1.0, 0.0).astype(BF16), jnp.where(upper, 1.0, 0.0).astype(BF16))]
    mid = (C // 2, C - 1 - C // 2)
    last = (C - 1, 0)

    def block_rows(t):
        return t * rows_b if isinstance(t, int) else pl.multiple_of(t * rows_b, rows_b)

    def prefix_sums(t):
        rows = pl.ds(block_rows(t), rows_b)
        lfs = (lff_ref[rows, :], lfb_ref[rows, :])
        b3s = []
        for d, lf in enumerate(lfs):
            hi = lf.astype(BF16)
            lo = (lf - hi.astype(F32)).astype(BF16)
            b3s.append(jnp.stack([
                jnp.dot(tri2[d],
                        jnp.concatenate([hi[j * C:(j + 1) * C], lo[j * C:(j + 1) * C]], axis=0),
                        preferred_element_type=F32)
                for j in range(nb)]))
        return lfs, b3s

    def state_updates(t):
        r0 = block_rows(t)
        for j in range(nb):
            cr = pl.ds(r0 + j * C, C)
            ivt = i_ref[cr, :].astype(F32).T.astype(BF16)
            ut_ref[t * nb + j] = jnp.dot(ivt, kd_ref[cr, :], preferred_element_type=F32)

    def scaled_operands(t, lfs, b3s):
        rows = pl.ds(block_rows(t), rows_b)
        q3 = q_ref[rows, :].astype(F32).reshape(nb, C, K)
        for d, (lf, b3) in enumerate(zip(lfs, b3s)):
            k3 = (1.0 - jnp.exp2(lf)).reshape(nb, C, K)
            bm = b3[:, mid[d]:mid[d] + 1, :]
            bl = b3[:, last[d]:last[d] + 1, :]
            e = jnp.exp2(b3 - bm)
            einv = 1.0 / e
            qm = q3 * e
            km = k3 * einv
            qm_ref[d, rows, :] = qm.reshape(rows_b, K).astype(BF16)
            km_ref[d, rows, :] = km.reshape(rows_b, K).astype(BF16)
            qb_ref[rows, d * K:(d + 1) * K] = (qm * jnp.exp2(bm)).reshape(rows_b, K).astype(BF16)
            kd_ref[rows, d * K:(d + 1) * K] = (km * jnp.exp2(bl - bm)).reshape(rows_b, K).astype(BF16)
            c0 = t * nb if isinstance(t, int) else pl.multiple_of(t * nb, nb)
            dec_ref[d, pl.ds(c0, nb), :] = jnp.exp2(bl).reshape(nb, K)

    n_blocks = n_chunks // nb
    scaled_operands(0, *prefix_sums(0))

    def prep(t, _):
        sums = prefix_sums(t)
        state_updates(t - 1)
        scaled_operands(t, *sums)
        return 0

    lax.fori_loop(1, n_blocks, prep, 0)
    state_updates(n_blocks - 1)

    def chain(n, states):
        sf, sb = states
        m = n_chunks - 1 - n
        st_ref[n, :, 0:K] = sf.astype(BF16)
        st_ref[m, :, K:2 * K] = sb.astype(BF16)
        return (dec_ref[0, pl.ds(n, 1), :] * sf + ut_ref[n, :, 0:K],
                dec_ref[1, pl.ds(m, 1), :] * sb + ut_ref[m, :, K:2 * K])

    zero_state = jnp.zeros((K, K), F32)
    lax.fori_loop(0, n_chunks, chain, (zero_state, zero_state))

    nw = nw_ref[...]
    nt_dims = (((1,), (1,)), ((), ()))

    no = HG_BLOCK_OUT

    def outp(t, _):
        r0 = pl.multiple_of(t * (no * C), no * C)
        crs = [pl.ds(r0 + j * C, C) for j in range(no)]
        a_f = [lax.dot_general(qm_ref[0, cr, :], km_ref[0, cr, :], nt_dims,
                               preferred_element_type=F32) for cr in crs]
        a_b = [lax.dot_general(qm_ref[1, cr, :], km_ref[1, cr, :], nt_dims,
                               preferred_element_type=F32) for cr in crs]
        a = [(jnp.where(lower, f, 0.0) + jnp.where(upper, b, 0.0)).astype(BF16)
             for f, b in zip(a_f, a_b)]
        outs = [jnp.dot(a[j], i_ref[cr, :], preferred_element_type=F32)
                + lax.dot_general(qb_ref[cr, :], st_ref[t * no + j], nt_dims,
                                  preferred_element_type=F32)
                for j, cr in enumerate(crs)]
        for cr, o in zip(crs, outs):
            ms = jnp.mean(o * o, axis=-1, keepdims=True)
            y = o * lax.rsqrt(ms + RMS_EPS) * nw
            o_ref[cr, :] = (y * g_ref[cr, :].astype(F32)).astype(o_ref.dtype)
        return 0

    lax.fori_loop(0, n_chunks // no, outp, 0)


def _hgrn_call(hq, hi, lf, hg, nw):
    B, S, _ = hq.shape
    n_chunks = S // HG_CHUNK
    K = HG_EXPAND
    blk = lambda b, h: (b, 0, h)
    return pl.pallas_call(
        _hgrn_kernel,
        grid=(B, HG_HEADS),
        in_specs=[
            pl.BlockSpec((None, S, K), blk),
            pl.BlockSpec((None, S, K), blk),
            pl.BlockSpec((None, S, K), blk),
            pl.BlockSpec((None, S, K), lambda b, h: (b, 0, HG_HEADS + h)),
            pl.BlockSpec((None, S, K), blk),
            pl.BlockSpec((1, K), lambda b, h: (0, 0)),
        ],
        out_specs=pl.BlockSpec((None, S, K), blk),
        out_shape=jax.ShapeDtypeStruct((B, S, HG_WIDTH), BF16),
        scratch_shapes=[
            pltpu.VMEM((2, S, K), BF16),
            pltpu.VMEM((2, S, K), BF16),
            pltpu.VMEM((S, 2 * K), BF16),
            pltpu.VMEM((S, 2 * K), BF16),
            pltpu.VMEM((2, n_chunks, K), F32),
            pltpu.VMEM((n_chunks, K, 2 * K), F32),
            pltpu.VMEM((n_chunks, K, 2 * K), BF16),
        ],
        compiler_params=pltpu.CompilerParams(
            dimension_semantics=("arbitrary", "arbitrary"),
            vmem_limit_bytes=VMEM_LIMIT),
        name="hgrn2",
    )(hq, hi, lf, lf, hg, nw)


def _out_kernel(x_ref, attn_ref, ag_ref, hgb_ref, anw_ref, w_ref, lnw_ref, lnb_ref, o_ref):
    a = jnp.concatenate([attn_ref[g] for g in range(N_KV_HEADS)], axis=1).astype(F32)
    ms = jnp.mean(a * a, axis=-1, keepdims=True)
    ab = a * lax.rsqrt(ms + RMS_EPS) * anw_ref[...] * ag_ref[...].astype(F32)
    y = (jnp.dot(ab.astype(BF16), w_ref[:ATTN_WIDTH, :], preferred_element_type=F32)
         + jnp.dot(hgb_ref[...], w_ref[ATTN_WIDTH:, :], preferred_element_type=F32))
    z = DEEPNORM_ALPHA * x_ref[...] + y
    mu = jnp.mean(z, axis=-1, keepdims=True)
    zc = z - mu
    var = jnp.mean(zc * zc, axis=-1, keepdims=True)
    o_ref[...] = zc * lax.rsqrt(var + LN_EPS) * lnw_ref[...] + lnb_ref[...]


def _out_call(x, attn, ag, hgb, anw, w_out_b, lnw, lnb):
    B, S, D = x.shape
    tm = OUT_TM
    row = lambda b, i: (b, i, 0)
    const2 = lambda b, i: (0, 0)
    return pl.pallas_call(
        _out_kernel,
        grid=(B, S // tm),
        in_specs=[
            pl.BlockSpec((None, tm, D), row),
            pl.BlockSpec((None, N_KV_HEADS, tm, ATTN_WIDTH // N_KV_HEADS), lambda b, i: (b, 0, i, 0)),
            pl.BlockSpec((None, tm, ATTN_WIDTH), row),
            pl.BlockSpec((None, tm, HG_WIDTH), row),
            pl.BlockSpec((1, ATTN_WIDTH), const2),
            pl.BlockSpec((ATTN_WIDTH + HG_WIDTH, D), const2),
            pl.BlockSpec((1, D), const2),
            pl.BlockSpec((1, D), const2),
        ],
        out_specs=pl.BlockSpec((None, tm, D), row),
        out_shape=jax.ShapeDtypeStruct((B, S, D), x.dtype),
        compiler_params=pltpu.CompilerParams(
            dimension_semantics=("arbitrary", "arbitrary"),
            vmem_limit_bytes=VMEM_LIMIT),
        name="out_proj_ln",
    )(x, attn, ag, hgb, anw, w_out_b, lnw, lnb)


def _rope_tables(seq_len):
    n_rows = seq_len // GRID_W
    inv = ROPE_THETA ** (-jnp.arange(0, AXIS_DIM, 2, dtype=F32) / AXIS_DIM)
    lane = jnp.arange(LANES, dtype=jnp.int32)
    d = lane % HEAD_DIM
    use_col = (d // AXIS_DIM) == 1
    dd = d % AXIS_DIM
    first = dd < (AXIS_DIM // 2)
    freq = inv[dd % (AXIS_DIM // 2)]
    idx = jnp.arange(max(n_rows, GRID_W), dtype=jnp.int32).astype(F32)
    ang = idx[:, None] * freq[None, :]
    cos_i = jnp.cos(ang)
    sin_i = jnp.where(first[None, :], -jnp.sin(ang), jnp.sin(ang))
    cos_i, sin_i = lax.optimization_barrier((cos_i, sin_i))

    def expand(tab):
        by_row = jnp.repeat(tab[:n_rows], GRID_W, axis=0)
        by_col = jnp.tile(tab[:GRID_W], (n_rows, 1))
        return jnp.where(use_col[None, :], by_col, by_row)

    return expand(cos_i), expand(sin_i)


def kernel(x, w_in, q_norm_w, k_norm_w, attn_norm_w, hg_lb_logits, hg_norm_w, w_out, ln_w, ln_b):
    B, S, D = x.shape
    assert w_in.shape == (DEPTH, D, IN_WIDTH) and DEPTH == 1
    cos_t, sin_t = _rope_tables(S)
    qw = jnp.tile(q_norm_w[0].astype(F32), N_Q_HEADS).reshape(1, ATTN_WIDTH)
    kw = jnp.tile(k_norm_w[0].astype(F32), N_KV_HEADS).reshape(1, KV_WIDTH)
    lbl = hg_lb_logits.astype(F32).reshape(2 * (DEPTH + 1), HG_WIDTH)
    hid = jnp.arange(ATTN_WIDTH, dtype=jnp.int32) // HEAD_DIM
    seg = jnp.where(hid[:, None] == hid[None, :], 1.0 / HEAD_DIM, 0.0).astype(BF16)

    qt, k, vt, ag, hq, hi, lf, hg = _proj_call(
        x, w_in[0].astype(BF16), cos_t, sin_t, qw, kw, lbl, seg)
    attn = _attn_call(qt, k, vt)
    hgb = _hgrn_call(hq, hi, lf, hg, hg_norm_w[0].astype(F32).reshape(1, HG_EXPAND))
    return _out_call(
        x, attn, ag, hgb,
        attn_norm_w[0].astype(F32).reshape(1, ATTN_WIDTH),
        w_out[0].astype(BF16),
        ln_w[0].astype(F32).reshape(1, D),
        ln_b[0].astype(F32).reshape(1, D))
```

```python
import jax
import jax.numpy as jnp
from jax import lax
from jax.experimental import pallas as pl
from jax.experimental.pallas import tpu as pltpu

F32 = jnp.float32
BF16 = jnp.bfloat16

GRID_W = 64
ATTN_WIDTH = 512
HEAD_DIM = 64
N_Q_HEADS = 8
N_KV_HEADS = 2
KV_WIDTH = 128
AXIS_DIM = 32
ROPE_THETA = 10000.0
HG_WIDTH = 512
HG_EXPAND = 128
HG_HEADS = 4
HG_CHUNK = 64
RMS_EPS = 1e-6
LN_EPS = 1e-5
DEPTH = 1
DEEPNORM_ALPHA = (2 * DEPTH) ** 0.25
IN_WIDTH = 3840
LOG2E = 1.4426950408889634
Q_SCALE = HEAD_DIM ** -0.5 * LOG2E

LANES = 128
BF16_SUBLANES = 16
VMEM_LIMIT = 56 * 1024 * 1024

PROJ_TM = 512
ATTN_TQ = 128
ATTN_TK = 1024
HG_BLOCK = 32
HG_BLOCK_OUT = 64
OUT_TM = 1024


def _exact_zero(x):
    bits = lax.bitcast_convert_type(x, jnp.uint32)
    return ((bits >> 16) >> 16).astype(F32)


def _sigmoid(x):
    return 1.0 / (1.0 + jnp.exp(-x))


def _proj_kernel(x_ref, w_ref, cos_ref, sin_ref, qw_ref, kw_ref, lbl_ref, seg_ref,
                 qt_ref, k_ref, vt_ref, ag_ref, hq_ref, hi_ref, lf_ref, hg_ref):
    xb = x_ref[...].astype(BF16)
    tm = xb.shape[0]

    def proj(a, b):
        return jnp.dot(xb, w_ref[:, a:b], preferred_element_type=F32)

    cosv = cos_ref[...]
    sinv = sin_ref[...]
    lane = lax.broadcasted_iota(jnp.int32, (tm, LANES), 1)
    first_half = (lane % AXIS_DIM) < (AXIS_DIM // 2)

    def norm_rope(a, w_row, seg, scale):
        sq = (a * a).astype(BF16)
        blk = min(a.shape[1], 2 * LANES)
        ms = jnp.concatenate(
            [jnp.dot(sq[:, j:j + blk], seg[j:j + blk, j:j + blk], preferred_element_type=F32)
             for j in range(0, a.shape[1], blk)], axis=1)
        y = a * lax.rsqrt(ms + RMS_EPS) * w_row
        outs = []
        for c in range(a.shape[1] // LANES):
            yc = y[:, c * LANES:(c + 1) * LANES]
            partner = jnp.where(first_half,
                                pltpu.roll(yc, LANES - AXIS_DIM // 2, 1),
                                pltpu.roll(yc, AXIS_DIM // 2, 1))
            outs.append((yc * cosv + partner * sinv) * scale)
        return outs

    aq = proj(0, ATTN_WIDTH)
    qs = norm_rope(aq, qw_ref[...], seg_ref[...], Q_SCALE)
    for c, qc in enumerate(qs):
        qt_ref[c * LANES:(c + 1) * LANES, :] = qc.T.astype(BF16)
    akv = proj(ATTN_WIDTH, ATTN_WIDTH + 2 * KV_WIDTH)
    kk = norm_rope(akv[:, :KV_WIDTH], kw_ref[...], seg_ref[:KV_WIDTH, :KV_WIDTH], 1.0)[0]
    k_ref[...] = kk.astype(BF16)
    vt_ref[...] = akv[:, KV_WIDTH:].T.astype(BF16)

    base = ATTN_WIDTH + 2 * KV_WIDTH
    ag = proj(base, base + ATTN_WIDTH)
    ag_ref[...] = (ag * _sigmoid(ag)).astype(BF16)
    base += ATTN_WIDTH
    hq = proj(base, base + HG_WIDTH)
    hq_ref[...] = (hq * _sigmoid(hq) * (HG_EXPAND ** -0.5)).astype(BF16)
    base += HG_WIDTH
    hi_ref[...] = proj(base, base + HG_WIDTH).astype(BF16)
    base += HG_WIDTH
    lbl = lbl_ref[...]
    for d in range(2):
        l0 = lbl[2 * d:2 * d + 1, :]
        l1 = lbl[2 * d + 1:2 * d + 2, :]
        mx = jnp.maximum(l0, l1)
        e0 = jnp.exp(l0 - mx)
        e1 = jnp.exp(l1 - mx)
        lb = e0 / (e0 + e1)
        z = proj(base, base + HG_WIDTH)
        lf_ref[:, d * HG_WIDTH:(d + 1) * HG_WIDTH] = jnp.log(lb + (1.0 - lb) * _sigmoid(z)) * LOG2E
        base += HG_WIDTH
    hg = proj(base, base + HG_WIDTH)
    hg_ref[...] = (hg * _sigmoid(hg)).astype(BF16)


def _proj_call(x, w_in_b, cos_t, sin_t, qw, kw, lbl, seg):
    B, S, D = x.shape
    tm = PROJ_TM
    nt = S // tm
    row = lambda b, i: (b, i, 0)
    const2 = lambda b, i: (0, 0)
    out_shape = (
        jax.ShapeDtypeStruct((B, ATTN_WIDTH, S), BF16),
        jax.ShapeDtypeStruct((B, S, KV_WIDTH), BF16),
        jax.ShapeDtypeStruct((B, KV_WIDTH, S), BF16),
        jax.ShapeDtypeStruct((B, S, ATTN_WIDTH), BF16),
        jax.ShapeDtypeStruct((B, S, HG_WIDTH), BF16),
        jax.ShapeDtypeStruct((B, S, HG_WIDTH), BF16),
        jax.ShapeDtypeStruct((B, S, 2 * HG_WIDTH), F32),
        jax.ShapeDtypeStruct((B, S, HG_WIDTH), BF16),
    )
    in_specs = [
        pl.BlockSpec((None, tm, D), row),
        pl.BlockSpec((D, IN_WIDTH), const2),
        pl.BlockSpec((tm, LANES), lambda b, i: (i, 0)),
        pl.BlockSpec((tm, LANES), lambda b, i: (i, 0)),
        pl.BlockSpec((1, ATTN_WIDTH), const2),
        pl.BlockSpec((1, KV_WIDTH), const2),
        pl.BlockSpec((4, HG_WIDTH), const2),
        pl.BlockSpec((ATTN_WIDTH, ATTN_WIDTH), const2),
    ]
    out_specs = (
        pl.BlockSpec((None, ATTN_WIDTH, tm), lambda b, i: (b, 0, i)),
        pl.BlockSpec((None, tm, KV_WIDTH), row),
        pl.BlockSpec((None, KV_WIDTH, tm), lambda b, i: (b, 0, i)),
        pl.BlockSpec((None, tm, ATTN_WIDTH), row),
        pl.BlockSpec((None, tm, HG_WIDTH), row),
        pl.BlockSpec((None, tm, HG_WIDTH), row),
        pl.BlockSpec((None, tm, 2 * HG_WIDTH), row),
        pl.BlockSpec((None, tm, HG_WIDTH), row),
    )
    return pl.pallas_call(
        _proj_kernel,
        grid=(B, nt),
        in_specs=in_specs,
        out_specs=out_specs,
        out_shape=out_shape,
        compiler_params=pltpu.CompilerParams(
            dimension_semantics=("arbitrary", "arbitrary"),
            vmem_limit_bytes=VMEM_LIMIT),
        name="in_proj",
    )(x, w_in_b, cos_t, sin_t, qw, kw, lbl, seg)


def _attn_kernel(qt_ref, k_ref, vt_ref, o_ref, s_ref, p_ref, vx_ref):
    S = k_ref.shape[0]
    G = N_Q_HEADS // N_KV_HEADS
    tq, tk = ATTN_TQ, ATTN_TK
    cols = G * tq
    n_units = N_KV_HEADS * (S // tq)
    n_c = S // tk
    vrows = vx_ref.shape[1]

    for g in range(N_KV_HEADS):
        vx_ref[g, 0:HEAD_DIM, :] = vt_ref[g * HEAD_DIM:(g + 1) * HEAD_DIM, :]
        vx_ref[g, HEAD_DIM:vrows, :] = jnp.ones((vrows - HEAD_DIM, S), BF16)

    row_group = lax.broadcasted_iota(jnp.int32, (N_KV_HEADS * HEAD_DIM, cols), 0) // HEAD_DIM

    def q_weights(u):
        i, g = u >> 1, u & 1
        c0 = pl.multiple_of(i * tq, tq)
        qs = jnp.concatenate(
            [qt_ref[pl.ds(pl.multiple_of((G * g + h) * HEAD_DIM, HEAD_DIM), HEAD_DIM),
                    pl.ds(c0, tq)] for h in range(G)], axis=1)
        return jnp.where(row_group == g, jnp.concatenate([qs, qs], axis=0), 0.0).astype(BF16)

    def stage(u, m_prev, do_scores=True, do_probs=True, do_pv=True):
        if do_scores:
            w = q_weights(u)
            mx = jnp.full((8, cols), -jnp.inf, F32)
        if do_pv:
            g_pv = (u - 2) & 1
            acc = jnp.zeros((vrows, cols), F32)
        for c in range(n_c):
            rows = pl.ds(c * tk, tk)
            m_c = m_prev
            if do_pv:
                acc = acc + jnp.dot(vx_ref[g_pv, :, rows], p_ref[rows, :],
                                    preferred_element_type=F32)
                if do_probs:
                    guard = sum(_exact_zero(acc[0:8, t * 2 * LANES:(t * 2 + 1) * LANES])
                                for t in range(cols // (2 * LANES)))
                    m_c = m_prev + jnp.tile(guard[0:1, :], (1, cols // LANES))
            if do_probs:
                p_ref[rows, :] = jnp.exp2(s_ref[rows, :] - m_c).astype(BF16)
            if do_scores:
                s = jnp.dot(k_ref[rows, :], w, preferred_element_type=F32)
                s_ref[rows, :] = s
                mx = jnp.maximum(mx, jnp.max(s.reshape(tk // 8, 8, cols), axis=0))
        if do_pv:
            o_t = acc[0:HEAD_DIM, :] / acc[HEAD_DIM:HEAD_DIM + 1, :]
            o = o_t.T
            blk = jnp.concatenate([o[h * tq:(h + 1) * tq, :] for h in range(G)], axis=1)
            r0 = pl.multiple_of(((u - 2) >> 1) * tq, tq)
            o_ref[g_pv, pl.ds(r0, tq), :] = blk.astype(o_ref.dtype)
        if do_scores:
            return jnp.max(mx, axis=0, keepdims=True)
        return m_prev

    zero = jnp.int32(0)
    m = stage(zero, None, do_probs=False, do_pv=False)
    m = stage(zero + 1, m, do_pv=False)
    m = lax.fori_loop(2, n_units, stage, m)
    m = stage(zero + n_units, m, do_scores=False)
    stage(zero + n_units + 1, m, do_scores=False, do_probs=False)


def _attn_call(qt, k, vt):
    B, _, S = qt.shape
    G = N_Q_HEADS // N_KV_HEADS
    cols = G * ATTN_TQ
    gw = G * HEAD_DIM
    whole = lambda b: (b, 0, 0)
    return pl.pallas_call(
        _attn_kernel,
        grid=(B,),
        in_specs=[
            pl.BlockSpec((None, ATTN_WIDTH, S), whole),
            pl.BlockSpec((None, S, KV_WIDTH), whole),
            pl.BlockSpec((None, KV_WIDTH, S), whole),
        ],
        out_specs=pl.BlockSpec((None, N_KV_HEADS, S, gw), lambda b: (b, 0, 0, 0)),
        out_shape=jax.ShapeDtypeStruct((B, N_KV_HEADS, S, gw), BF16),
        scratch_shapes=[
            pltpu.VMEM((S, cols), F32),
            pltpu.VMEM((S, cols), BF16),
            pltpu.VMEM((N_KV_HEADS, HEAD_DIM + BF16_SUBLANES, S), BF16),
        ],
        compiler_params=pltpu.CompilerParams(
            dimension_semantics=("arbitrary",),
            vmem_limit_bytes=VMEM_LIMIT),
        name="attention",
    )(qt, k, vt)


def _hgrn_kernel(q_ref, i_ref, lff_ref, lfb_ref, g_ref, nw_ref, o_ref,
                 qm_ref, km_ref, qb_ref, kd_ref, dec_ref, ut_ref, st_ref):
    S = q_ref.shape[0]
    C = HG_CHUNK
    K = HG_EXPAND
    n_chunks = S // C
    nb = HG_BLOCK
    rows_b = nb * C
    r = lax.broadcasted_iota(jnp.int32, (C, C), 0)
    c = lax.broadcasted_iota(jnp.int32, (C, C), 1)
    lower = r >= c
    upper = r <= c
    tri2 = [jnp.concatenate([t, t], axis=1) for t in
            (jnp.where(lower, 1.0, 0.0).astype(BF16), jnp.where(upper, 1.0, 0.0).astype(BF16))]
    mid = (C // 2, C - 1 - C // 2)
    last = (C - 1, 0)

    def block_rows(t):
        return t * rows_b if isinstance(t, int) else pl.multiple_of(t * rows_b, rows_b)

    def prefix_sums(t):
        rows = pl.ds(block_rows(t), rows_b)
        lfs = (lff_ref[rows, :], lfb_ref[rows, :])
        b3s = []
        for d, lf in enumerate(lfs):
            hi = lf.astype(BF16)
            lo = (lf - hi.astype(F32)).astype(BF16)
            b3s.append(jnp.stack([
                jnp.dot(tri2[d],
                        jnp.concatenate([hi[j * C:(j + 1) * C], lo[j * C:(j + 1) * C]], axis=0),
                        preferred_element_type=F32)
                for j in range(nb)]))
        return lfs, b3s

    def state_updates(t):
        r0 = block_rows(t)
        for j in range(nb):
            cr = pl.ds(r0 + j * C, C)
            ivt = i_ref[cr, :].astype(F32).T.astype(BF16)
            ut_ref[t * nb + j] = jnp.dot(ivt, kd_ref[cr, :], preferred_element_type=F32)

    def scaled_operands(t, lfs, b3s):
        rows = pl.ds(block_rows(t), rows_b)
        q3 = q_ref[rows, :].astype(F32).reshape(nb, C, K)
        for d, (lf, b3) in enumerate(zip(lfs, b3s)):
            k3 = (1.0 - jnp.exp2(lf)).reshape(nb, C, K)
            bm = b3[:, mid[d]:mid[d] + 1, :]
            bl = b3[:, last[d]:last[d] + 1, :]
            e = jnp.exp2(b3 - bm)
            einv = 1.0 / e
            qm = q3 * e
            km = k3 * einv
            qm = qm.astype(BF16)
            km = km.astype(BF16)
            qm_ref[d, rows, :] = qm.reshape(rows_b, K)
            km_ref[d, rows, :] = km.reshape(rows_b, K)
            qb_ref[rows, d * K:(d + 1) * K] = (qm * jnp.exp2(bm).astype(BF16)).reshape(rows_b, K)
            kd_ref[rows, d * K:(d + 1) * K] = (km * jnp.exp2(bl - bm).astype(BF16)).reshape(rows_b, K)
            c0 = t * nb if isinstance(t, int) else pl.multiple_of(t * nb, nb)
            dec_ref[d, pl.ds(c0, nb), :] = jnp.exp2(bl).reshape(nb, K)

    n_blocks = n_chunks // nb
    scaled_operands(0, *prefix_sums(0))

    def prep(t, _):
        sums = prefix_sums(t)
        state_updates(t - 1)
        scaled_operands(t, *sums)
        return 0

    lax.fori_loop(1, n_blocks, prep, 0)
    state_updates(n_blocks - 1)

    def chain(n, states):
        sf, sb = states
        m = n_chunks - 1 - n
        st_ref[n, :, 0:K] = sf.astype(BF16)
        st_ref[m, :, K:2 * K] = sb.astype(BF16)
        return (dec_ref[0, pl.ds(n, 1), :] * sf + ut_ref[n, :, 0:K],
                dec_ref[1, pl.ds(m, 1), :] * sb + ut_ref[m, :, K:2 * K])

    zero_state = jnp.zeros((K, K), F32)
    lax.fori_loop(0, n_chunks, chain, (zero_state, zero_state))

    nw = nw_ref[...]
    nt_dims = (((1,), (1,)), ((), ()))

    no = HG_BLOCK_OUT

    def outp(t, _):
        r0 = pl.multiple_of(t * (no * C), no * C)
        crs = [pl.ds(r0 + j * C, C) for j in range(no)]
        a_f = [lax.dot_general(qm_ref[0, cr, :], km_ref[0, cr, :], nt_dims,
                               preferred_element_type=F32) for cr in crs]
        a_b = [lax.dot_general(qm_ref[1, cr, :], km_ref[1, cr, :], nt_dims,
                               preferred_element_type=F32) for cr in crs]
        a = [(jnp.where(lower, f, 0.0) + jnp.where(upper, b, 0.0)).astype(BF16)
             for f, b in zip(a_f, a_b)]
        outs = [jnp.dot(a[j], i_ref[cr, :], preferred_element_type=F32)
                + lax.dot_general(qb_ref[cr, :], st_ref[t * no + j], nt_dims,
                                  preferred_element_type=F32)
                for j, cr in enumerate(crs)]
        for cr, o in zip(crs, outs):
            ms = jnp.mean(o * o, axis=-1, keepdims=True)
            y = o * lax.rsqrt(ms + RMS_EPS) * nw
            o_ref[cr, :] = (y * g_ref[cr, :].astype(F32)).astype(o_ref.dtype)
        return 0

    lax.fori_loop(0, n_chunks // no, outp, 0)


def _hgrn_call(hq, hi, lf, hg, nw):
    B, S, _ = hq.shape
    n_chunks = S // HG_CHUNK
    K = HG_EXPAND
    blk = lambda b, h: (b, 0, h)
    return pl.pallas_call(
        _hgrn_kernel,
        grid=(B, HG_HEADS),
        in_specs=[
            pl.BlockSpec((None, S, K), blk),
            pl.BlockSpec((None, S, K), blk),
            pl.BlockSpec((None, S, K), blk),
            pl.BlockSpec((None, S, K), lambda b, h: (b, 0, HG_HEADS + h)),
            pl.BlockSpec((None, S, K), blk),
            pl.BlockSpec((1, K), lambda b, h: (0, 0)),
        ],
        out_specs=pl.BlockSpec((None, S, K), blk),
        out_shape=jax.ShapeDtypeStruct((B, S, HG_WIDTH), BF16),
        scratch_shapes=[
            pltpu.VMEM((2, S, K), BF16),
            pltpu.VMEM((2, S, K), BF16),
            pltpu.VMEM((S, 2 * K), BF16),
            pltpu.VMEM((S, 2 * K), BF16),
            pltpu.VMEM((2, n_chunks, K), F32),
            pltpu.VMEM((n_chunks, K, 2 * K), F32),
            pltpu.VMEM((n_chunks, K, 2 * K), BF16),
        ],
        compiler_params=pltpu.CompilerParams(
            dimension_semantics=("arbitrary", "arbitrary"),
            vmem_limit_bytes=VMEM_LIMIT),
        name="hgrn2",
    )(hq, hi, lf, lf, hg, nw)


def _out_kernel(x_ref, attn_ref, ag_ref, hgb_ref, anw_ref, w_ref, lnw_ref, lnb_ref, o_ref):
    a = jnp.concatenate([attn_ref[g] for g in range(N_KV_HEADS)], axis=1).astype(F32)
    ms = jnp.mean(a * a, axis=-1, keepdims=True)
    ab = a * lax.rsqrt(ms + RMS_EPS) * anw_ref[...] * ag_ref[...].astype(F32)
    y = (jnp.dot(ab.astype(BF16), w_ref[:ATTN_WIDTH, :], preferred_element_type=F32)
         + jnp.dot(hgb_ref[...], w_ref[ATTN_WIDTH:, :], preferred_element_type=F32))
    z = DEEPNORM_ALPHA * x_ref[...] + y
    mu = jnp.mean(z, axis=-1, keepdims=True)
    zc = z - mu
    var = jnp.mean(zc * zc, axis=-1, keepdims=True)
    o_ref[...] = zc * lax.rsqrt(var + LN_EPS) * lnw_ref[...] + lnb_ref[...]


def _out_call(x, attn, ag, hgb, anw, w_out_b, lnw, lnb):
    B, S, D = x.shape
    tm = OUT_TM
    row = lambda b, i: (b, i, 0)
    const2 = lambda b, i: (0, 0)
    return pl.pallas_call(
        _out_kernel,
        grid=(B, S // tm),
        in_specs=[
            pl.BlockSpec((None, tm, D), row),
            pl.BlockSpec((None, N_KV_HEADS, tm, ATTN_WIDTH // N_KV_HEADS), lambda b, i: (b, 0, i, 0)),
            pl.BlockSpec((None, tm, ATTN_WIDTH), row),
            pl.BlockSpec((None, tm, HG_WIDTH), row),
            pl.BlockSpec((1, ATTN_WIDTH), const2),
            pl.BlockSpec((ATTN_WIDTH + HG_WIDTH, D), const2),
            pl.BlockSpec((1, D), const2),
            pl.BlockSpec((1, D), const2),
        ],
        out_specs=pl.BlockSpec((None, tm, D), row),
        out_shape=jax.ShapeDtypeStruct((B, S, D), x.dtype),
        compiler_params=pltpu.CompilerParams(
            dimension_semantics=("arbitrary", "arbitrary"),
            vmem_limit_bytes=VMEM_LIMIT),
        name="out_proj_ln",
    )(x, attn, ag, hgb, anw, w_out_b, lnw, lnb)


def _rope_tables(seq_len):
    n_rows = seq_len // GRID_W
    inv = ROPE_THETA ** (-jnp.arange(0, AXIS_DIM, 2, dtype=F32) / AXIS_DIM)
    lane = jnp.arange(LANES, dtype=jnp.int32)
    d = lane % HEAD_DIM
    use_col = (d // AXIS_DIM) == 1
    dd = d % AXIS_DIM
    first = dd < (AXIS_DIM // 2)
    freq = inv[dd % (AXIS_DIM // 2)]
    idx = jnp.arange(max(n_rows, GRID_W), dtype=jnp.int32).astype(F32)
    ang = idx[:, None] * freq[None, :]
    cos_i = jnp.cos(ang)
    sin_i = jnp.where(first[None, :], -jnp.sin(ang), jnp.sin(ang))
    cos_i, sin_i = lax.optimization_barrier((cos_i, sin_i))

    def expand(tab):
        by_row = jnp.repeat(tab[:n_rows], GRID_W, axis=0)
        by_col = jnp.tile(tab[:GRID_W], (n_rows, 1))
        return jnp.where(use_col[None, :], by_col, by_row)

    return expand(cos_i), expand(sin_i)


def kernel(x, w_in, q_norm_w, k_norm_w, attn_norm_w, hg_lb_logits, hg_norm_w, w_out, ln_w, ln_b):
    B, S, D = x.shape
    assert w_in.shape == (DEPTH, D, IN_WIDTH) and DEPTH == 1
    cos_t, sin_t = _rope_tables(S)
    qw = jnp.tile(q_norm_w[0].astype(F32), N_Q_HEADS).reshape(1, ATTN_WIDTH)
    kw = jnp.tile(k_norm_w[0].astype(F32), N_KV_HEADS).reshape(1, KV_WIDTH)
    lbl = hg_lb_logits.astype(F32).reshape(2 * (DEPTH + 1), HG_WIDTH)
    hid = jnp.arange(ATTN_WIDTH, dtype=jnp.int32) // HEAD_DIM
    seg = jnp.where(hid[:, None] == hid[None, :], 1.0 / HEAD_DIM, 0.0).astype(BF16)

    qt, k, vt, ag, hq, hi, lf, hg = _proj_call(
        x, w_in[0].astype(BF16), cos_t, sin_t, qw, kw, lbl, seg)
    attn = _attn_call(qt, k, vt)
    hgb = _hgrn_call(hq, hi, lf, hg, hg_norm_w[0].astype(F32).reshape(1, HG_EXPAND))
    return _out_call(
        x, attn, ag, hgb,
        attn_norm_w[0].astype(F32).reshape(1, ATTN_WIDTH),
        w_out[0].astype(BF16),
        ln_w[0].astype(F32).reshape(1, D),
        ln_b[0].astype(F32).reshape(1, D))
```

```python
import jax
import jax.numpy as jnp
from jax import lax
from jax.experimental import pallas as pl
from jax.experimental.pallas import tpu as pltpu

F32 = jnp.float32
BF16 = jnp.bfloat16

GRID_W = 64
ATTN_WIDTH = 512
HEAD_DIM = 64
N_Q_HEADS = 8
N_KV_HEADS = 2
KV_WIDTH = 128
AXIS_DIM = 32
ROPE_THETA = 10000.0
HG_WIDTH = 512
HG_EXPAND = 128
HG_HEADS = 4
HG_CHUNK = 64
RMS_EPS = 1e-6
LN_EPS = 1e-5
DEPTH = 1
DEEPNORM_ALPHA = (2 * DEPTH) ** 0.25
IN_WIDTH = 3840
LOG2E = 1.4426950408889634
Q_SCALE = HEAD_DIM ** -0.5 * LOG2E

LANES = 128
BF16_SUBLANES = 16
VMEM_LIMIT = 56 * 1024 * 1024

PROJ_TM = 512
ATTN_TQ = 128
ATTN_TK = 1024
HG_BLOCK = 32
HG_BLOCK_OUT = 64
OUT_TM = 1024
OUT_SUB = 256


def _exact_zero(x):
    bits = lax.bitcast_convert_type(x, jnp.uint32)
    return ((bits >> 16) >> 16).astype(F32)


def _sigmoid(x):
    return 1.0 / (1.0 + jnp.exp(-x))


def _proj_kernel(x_ref, w_ref, cos_ref, sin_ref, qw_ref, kw_ref, lbl_ref, seg_ref,
                 qt_ref, k_ref, vt_ref, ag_ref, hq_ref, hi_ref, lf_ref, hg_ref):
    xb = x_ref[...].astype(BF16)
    tm = xb.shape[0]

    def proj(a, b):
        return jnp.dot(xb, w_ref[:, a:b], preferred_element_type=F32)

    cosv = cos_ref[...]
    sinv = sin_ref[...]
    lane = lax.broadcasted_iota(jnp.int32, (tm, LANES), 1)
    first_half = (lane % AXIS_DIM) < (AXIS_DIM // 2)

    def norm_rope(a, w_row, seg, scale):
        sq = (a * a).astype(BF16)
        blk = min(a.shape[1], 2 * LANES)
        ms = jnp.concatenate(
            [jnp.dot(sq[:, j:j + blk], seg[j:j + blk, j:j + blk], preferred_element_type=F32)
             for j in range(0, a.shape[1], blk)], axis=1)
        y = a * lax.rsqrt(ms + RMS_EPS) * w_row
        outs = []
        for c in range(a.shape[1] // LANES):
            yc = y[:, c * LANES:(c + 1) * LANES]
            partner = jnp.where(first_half,
                                pltpu.roll(yc, LANES - AXIS_DIM // 2, 1),
                                pltpu.roll(yc, AXIS_DIM // 2, 1))
            outs.append((yc * cosv + partner * sinv) * scale)
        return outs

    aq = proj(0, ATTN_WIDTH)
    qs = norm_rope(aq, qw_ref[...], seg_ref[...], Q_SCALE)
    for c, qc in enumerate(qs):
        qt_ref[c * LANES:(c + 1) * LANES, :] = qc.T.astype(BF16)
    akv = proj(ATTN_WIDTH, ATTN_WIDTH + 2 * KV_WIDTH)
    kk = norm_rope(akv[:, :KV_WIDTH], kw_ref[...], seg_ref[:KV_WIDTH, :KV_WIDTH], 1.0)[0]
    k_ref[...] = kk.astype(BF16)
    vt_ref[...] = akv[:, KV_WIDTH:].T.astype(BF16)

    base = ATTN_WIDTH + 2 * KV_WIDTH
    ag = proj(base, base + ATTN_WIDTH)
    ag_ref[...] = (ag * _sigmoid(ag)).astype(BF16)
    base += ATTN_WIDTH
    hq = proj(base, base + HG_WIDTH)
    hq_ref[...] = (hq * _sigmoid(hq) * (HG_EXPAND ** -0.5)).astype(BF16)
    base += HG_WIDTH
    hi_ref[...] = proj(base, base + HG_WIDTH).astype(BF16)
    base += HG_WIDTH
    lbl = lbl_ref[...]
    for d in range(2):
        l0 = lbl[2 * d:2 * d + 1, :]
        l1 = lbl[2 * d + 1:2 * d + 2, :]
        mx = jnp.maximum(l0, l1)
        e0 = jnp.exp(l0 - mx)
        e1 = jnp.exp(l1 - mx)
        lb = e0 / (e0 + e1)
        z = proj(base, base + HG_WIDTH)
        lf_ref[:, d * HG_WIDTH:(d + 1) * HG_WIDTH] = jnp.log(lb + (1.0 - lb) * _sigmoid(z)) * LOG2E
        base += HG_WIDTH
    hg = proj(base, base + HG_WIDTH)
    hg_ref[...] = (hg * _sigmoid(hg)).astype(BF16)


def _proj_call(x, w_in_b, cos_t, sin_t, qw, kw, lbl, seg):
    B, S, D = x.shape
    tm = PROJ_TM
    nt = S // tm
    row = lambda b, i: (b, i, 0)
    const2 = lambda b, i: (0, 0)
    out_shape = (
        jax.ShapeDtypeStruct((B, ATTN_WIDTH, S), BF16),
        jax.ShapeDtypeStruct((B, S, KV_WIDTH), BF16),
        jax.ShapeDtypeStruct((B, KV_WIDTH, S), BF16),
        jax.ShapeDtypeStruct((B, S, ATTN_WIDTH), BF16),
        jax.ShapeDtypeStruct((B, S, HG_WIDTH), BF16),
        jax.ShapeDtypeStruct((B, S, HG_WIDTH), BF16),
        jax.ShapeDtypeStruct((B, S, 2 * HG_WIDTH), F32),
        jax.ShapeDtypeStruct((B, S, HG_WIDTH), BF16),
    )
    in_specs = [
        pl.BlockSpec((None, tm, D), row),
        pl.BlockSpec((D, IN_WIDTH), const2),
        pl.BlockSpec((tm, LANES), lambda b, i: (i, 0)),
        pl.BlockSpec((tm, LANES), lambda b, i: (i, 0)),
        pl.BlockSpec((1, ATTN_WIDTH), const2),
        pl.BlockSpec((1, KV_WIDTH), const2),
        pl.BlockSpec((4, HG_WIDTH), const2),
        pl.BlockSpec((ATTN_WIDTH, ATTN_WIDTH), const2),
    ]
    out_specs = (
        pl.BlockSpec((None, ATTN_WIDTH, tm), lambda b, i: (b, 0, i)),
        pl.BlockSpec((None, tm, KV_WIDTH), row),
        pl.BlockSpec((None, KV_WIDTH, tm), lambda b, i: (b, 0, i)),
        pl.BlockSpec((None, tm, ATTN_WIDTH), row),
        pl.BlockSpec((None, tm, HG_WIDTH), row),
        pl.BlockSpec((None, tm, HG_WIDTH), row),
        pl.BlockSpec((None, tm, 2 * HG_WIDTH), row),
        pl.BlockSpec((None, tm, HG_WIDTH), row),
    )
    return pl.pallas_call(
        _proj_kernel,
        grid=(B, nt),
        in_specs=in_specs,
        out_specs=out_specs,
        out_shape=out_shape,
        compiler_params=pltpu.CompilerParams(
            dimension_semantics=("arbitrary", "arbitrary"),
            vmem_limit_bytes=VMEM_LIMIT),
        name="in_proj",
    )(x, w_in_b, cos_t, sin_t, qw, kw, lbl, seg)


def _attn_kernel(qt_ref, k_ref, vt_ref, o_ref, s_ref, p_ref, vx_ref):
    S = k_ref.shape[0]
    G = N_Q_HEADS // N_KV_HEADS
    tq, tk = ATTN_TQ, ATTN_TK
    cols = G * tq
    n_units = N_KV_HEADS * (S // tq)
    n_c = S // tk
    vrows = vx_ref.shape[1]

    for g in range(N_KV_HEADS):
        vx_ref[g, 0:HEAD_DIM, :] = vt_ref[g * HEAD_DIM:(g + 1) * HEAD_DIM, :]
        vx_ref[g, HEAD_DIM:vrows, :] = jnp.ones((vrows - HEAD_DIM, S), BF16)

    row_group = lax.broadcasted_iota(jnp.int32, (N_KV_HEADS * HEAD_DIM, cols), 0) // HEAD_DIM

    def q_weights(u):
        i, g = u >> 1, u & 1
        c0 = pl.multiple_of(i * tq, tq)
        qs = jnp.concatenate(
            [qt_ref[pl.ds(pl.multiple_of((G * g + h) * HEAD_DIM, HEAD_DIM), HEAD_DIM),
                    pl.ds(c0, tq)] for h in range(G)], axis=1)
        return jnp.where(row_group == g, jnp.concatenate([qs, qs], axis=0), 0.0).astype(BF16)

    def stage(u, m_prev, do_scores=True, do_probs=True, do_pv=True):
        if do_scores:
            w = q_weights(u)
            mx = jnp.full((8, cols), -jnp.inf, F32)
        if do_pv:
            g_pv = (u - 2) & 1
            acc = jnp.zeros((vrows, cols), F32)
        for c in range(n_c):
            rows = pl.ds(c * tk, tk)
            m_c = m_prev
            if do_pv:
                acc = acc + jnp.dot(vx_ref[g_pv, :, rows], p_ref[rows, :],
                                    preferred_element_type=F32)
                if do_probs:
                    guard = sum(_exact_zero(acc[0:8, t * 2 * LANES:(t * 2 + 1) * LANES])
                                for t in range(cols // (2 * LANES)))
                    m_c = m_prev + jnp.tile(guard[0:1, :], (1, cols // LANES))
            if do_probs:
                p_ref[rows, :] = jnp.exp2(s_ref[rows, :] - m_c).astype(BF16)
            if do_scores:
                s = jnp.dot(k_ref[rows, :], w, preferred_element_type=F32)
                s_ref[rows, :] = s
                mx = jnp.maximum(mx, jnp.max(s.reshape(tk // 8, 8, cols), axis=0))
        if do_pv:
            o_t = acc[0:HEAD_DIM, :] / acc[HEAD_DIM:HEAD_DIM + 1, :]
            o = o_t.T
            blk = jnp.concatenate([o[h * tq:(h + 1) * tq, :] for h in range(G)], axis=1)
            r0 = pl.multiple_of(((u - 2) >> 1) * tq, tq)
            o_ref[g_pv, pl.ds(r0, tq), :] = blk.astype(o_ref.dtype)
        if do_scores:
            return jnp.max(mx, axis=0, keepdims=True)
        return m_prev

    zero = jnp.int32(0)
    m = stage(zero, None, do_probs=False, do_pv=False)
    m = stage(zero + 1, m, do_pv=False)
    m = lax.fori_loop(2, n_units, stage, m)
    m = stage(zero + n_units, m, do_scores=False)
    stage(zero + n_units + 1, m, do_scores=False, do_probs=False)


def _attn_call(qt, k, vt):
    B, _, S = qt.shape
    G = N_Q_HEADS // N_KV_HEADS
    cols = G * ATTN_TQ
    gw = G * HEAD_DIM
    whole = lambda b: (b, 0, 0)
    return pl.pallas_call(
        _attn_kernel,
        grid=(B,),
        in_specs=[
            pl.BlockSpec((None, ATTN_WIDTH, S), whole),
            pl.BlockSpec((None, S, KV_WIDTH), whole),
            pl.BlockSpec((None, KV_WIDTH, S), whole),
        ],
        out_specs=pl.BlockSpec((None, N_KV_HEADS, S, gw), lambda b: (b, 0, 0, 0)),
        out_shape=jax.ShapeDtypeStruct((B, N_KV_HEADS, S, gw), BF16),
        scratch_shapes=[
            pltpu.VMEM((S, cols), F32),
            pltpu.VMEM((S, cols), BF16),
            pltpu.VMEM((N_KV_HEADS, HEAD_DIM + BF16_SUBLANES, S), BF16),
        ],
        compiler_params=pltpu.CompilerParams(
            dimension_semantics=("arbitrary",),
            vmem_limit_bytes=VMEM_LIMIT),
        name="attention",
    )(qt, k, vt)


def _hgrn_kernel(q_ref, i_ref, lff_ref, lfb_ref, g_ref, nw_ref, o_ref,
                 qm_ref, km_ref, qb_ref, kd_ref, dec_ref, ut_ref, st_ref):
    S = q_ref.shape[0]
    C = HG_CHUNK
    K = HG_EXPAND
    n_chunks = S // C
    nb = HG_BLOCK
    rows_b = nb * C
    r = lax.broadcasted_iota(jnp.int32, (C, C), 0)
    c = lax.broadcasted_iota(jnp.int32, (C, C), 1)
    lower = r >= c
    upper = r <= c
    tri2 = [jnp.concatenate([t, t], axis=1) for t in
            (jnp.where(lower, 1.0, 0.0).astype(BF16), jnp.where(upper, 1.0, 0.0).astype(BF16))]
    mid = (C // 2, C - 1 - C // 2)
    last = (C - 1, 0)

    def block_rows(t):
        return t * rows_b if isinstance(t, int) else pl.multiple_of(t * rows_b, rows_b)

    def prefix_sums(t):
        rows = pl.ds(block_rows(t), rows_b)
        lfs = (lff_ref[rows, :], lfb_ref[rows, :])
        b3s = []
        for d, lf in enumerate(lfs):
            hi = lf.astype(BF16)
            lo = (lf - hi.astype(F32)).astype(BF16)
            b3s.append(jnp.stack([
                jnp.dot(tri2[d],
                        jnp.concatenate([hi[j * C:(j + 1) * C], lo[j * C:(j + 1) * C]], axis=0),
                        preferred_element_type=F32)
                for j in range(nb)]))
        return lfs, b3s

    def state_updates(t):
        r0 = block_rows(t)
        for j in range(nb):
            cr = pl.ds(r0 + j * C, C)
            ivt = i_ref[cr, :].astype(F32).T.astype(BF16)
            ut_ref[t * nb + j] = jnp.dot(ivt, kd_ref[cr, :], preferred_element_type=F32)

    def scaled_operands(t, lfs, b3s):
        rows = pl.ds(block_rows(t), rows_b)
        q3 = q_ref[rows, :].astype(F32).reshape(nb, C, K)
        for d, (lf, b3) in enumerate(zip(lfs, b3s)):
            k3 = (1.0 - jnp.exp2(lf)).reshape(nb, C, K)
            bm = b3[:, mid[d]:mid[d] + 1, :]
            bl = b3[:, last[d]:last[d] + 1, :]
            e = jnp.exp2(b3 - bm)
            einv = 1.0 / e
            qm = q3 * e
            km = k3 * einv
            qm_ref[d, rows, :] = qm.reshape(rows_b, K).astype(BF16)
            km_ref[d, rows, :] = km.reshape(rows_b, K).astype(BF16)
            qb_ref[rows, d * K:(d + 1) * K] = (qm * jnp.exp2(bm)).reshape(rows_b, K).astype(BF16)
            kd_ref[rows, d * K:(d + 1) * K] = (km * jnp.exp2(bl - bm)).reshape(rows_b, K).astype(BF16)
            c0 = t * nb if isinstance(t, int) else pl.multiple_of(t * nb, nb)
            dec_ref[d, pl.ds(c0, nb), :] = jnp.exp2(bl).reshape(nb, K)

    n_blocks = n_chunks // nb
    scaled_operands(0, *prefix_sums(0))

    def prep(t, _):
        sums = prefix_sums(t)
        state_updates(t - 1)
        scaled_operands(t, *sums)
        return 0

    lax.fori_loop(1, n_blocks, prep, 0)
    state_updates(n_blocks - 1)

    def chain(n, states):
        sf, sb = states
        m = n_chunks - 1 - n
        st_ref[n, :, 0:K] = sf.astype(BF16)
        st_ref[m, :, K:2 * K] = sb.astype(BF16)
        return (dec_ref[0, pl.ds(n, 1), :] * sf + ut_ref[n, :, 0:K],
                dec_ref[1, pl.ds(m, 1), :] * sb + ut_ref[m, :, K:2 * K])

    zero_state = jnp.zeros((K, K), F32)
    lax.fori_loop(0, n_chunks, chain, (zero_state, zero_state))

    nw = nw_ref[...]
    nt_dims = (((1,), (1,)), ((), ()))

    no = HG_BLOCK_OUT

    def outp(t, _):
        r0 = pl.multiple_of(t * (no * C), no * C)
        crs = [pl.ds(r0 + j * C, C) for j in range(no)]
        a_f = [lax.dot_general(qm_ref[0, cr, :], km_ref[0, cr, :], nt_dims,
                               preferred_element_type=F32) for cr in crs]
        a_b = [lax.dot_general(qm_ref[1, cr, :], km_ref[1, cr, :], nt_dims,
                               preferred_element_type=F32) for cr in crs]
        a = [(jnp.where(lower, f, 0.0) + jnp.where(upper, b, 0.0)).astype(BF16)
             for f, b in zip(a_f, a_b)]
        outs = [jnp.dot(a[j], i_ref[cr, :], preferred_element_type=F32)
                + lax.dot_general(qb_ref[cr, :], st_ref[t * no + j], nt_dims,
                                  preferred_element_type=F32)
                for j, cr in enumerate(crs)]
        for cr, o in zip(crs, outs):
            ms = jnp.mean(o * o, axis=-1, keepdims=True)
            y = o * lax.rsqrt(ms + RMS_EPS) * nw
            o_ref[cr, :] = (y * g_ref[cr, :].astype(F32)).astype(o_ref.dtype)
        return 0

    lax.fori_loop(0, n_chunks // no, outp, 0)


def _hgrn_call(hq, hi, lf, hg, nw):
    B, S, _ = hq.shape
    n_chunks = S // HG_CHUNK
    K = HG_EXPAND
    blk = lambda b, h: (b, 0, h)
    return pl.pallas_call(
        _hgrn_kernel,
        grid=(B, HG_HEADS),
        in_specs=[
            pl.BlockSpec((None, S, K), blk),
            pl.BlockSpec((None, S, K), blk),
            pl.BlockSpec((None, S, K), blk),
            pl.BlockSpec((None, S, K), lambda b, h: (b, 0, HG_HEADS + h)),
            pl.BlockSpec((None, S, K), blk),
            pl.BlockSpec((1, K), lambda b, h: (0, 0)),
        ],
        out_specs=pl.BlockSpec((None, S, K), blk),
        out_shape=jax.ShapeDtypeStruct((B, S, HG_WIDTH), BF16),
        scratch_shapes=[
            pltpu.VMEM((2, S, K), BF16),
            pltpu.VMEM((2, S, K), BF16),
            pltpu.VMEM((S, 2 * K), BF16),
            pltpu.VMEM((S, 2 * K), BF16),
            pltpu.VMEM((2, n_chunks, K), F32),
            pltpu.VMEM((n_chunks, K, 2 * K), F32),
            pltpu.VMEM((n_chunks, K, 2 * K), BF16),
        ],
        compiler_params=pltpu.CompilerParams(
            dimension_semantics=("arbitrary", "arbitrary"),
            vmem_limit_bytes=VMEM_LIMIT),
        name="hgrn2",
    )(hq, hi, lf, lf, hg, nw)


def _out_kernel(x_ref, attn_ref, ag_ref, hgb_ref, anw_ref, w_ref, lnw_ref, lnb_ref, o_ref):
    tm = x_ref.shape[0]

    def project(r0):
        rows = pl.ds(r0, OUT_SUB)
        a = jnp.concatenate([attn_ref[g, rows, :] for g in range(N_KV_HEADS)],
                            axis=1).astype(F32)
        ms = jnp.mean(a * a, axis=-1, keepdims=True)
        ab = a * lax.rsqrt(ms + RMS_EPS) * anw_ref[...] * ag_ref[rows, :].astype(F32)
        return (jnp.dot(ab.astype(BF16), w_ref[:ATTN_WIDTH, :], preferred_element_type=F32)
                + jnp.dot(hgb_ref[rows, :], w_ref[ATTN_WIDTH:, :], preferred_element_type=F32))

    def finish(r0, y):
        rows = pl.ds(r0, OUT_SUB)
        z = DEEPNORM_ALPHA * x_ref[rows, :] + y
        mu = jnp.mean(z, axis=-1, keepdims=True)
        zc = z - mu
        var = jnp.mean(zc * zc, axis=-1, keepdims=True)
        o_ref[rows, :] = zc * lax.rsqrt(var + LN_EPS) * lnw_ref[...] + lnb_ref[...]

    pending = None
    for r0 in range(0, tm, OUT_SUB):
        y = project(r0)
        if pending is not None:
            finish(*pending)
        pending = (r0, y)
    finish(*pending)


def _out_call(x, attn, ag, hgb, anw, w_out_b, lnw, lnb):
    B, S, D = x.shape
    tm = OUT_TM
    row = lambda b, i: (b, i, 0)
    const2 = lambda b, i: (0, 0)
    return pl.pallas_call(
        _out_kernel,
        grid=(B, S // tm),
        in_specs=[
            pl.BlockSpec((None, tm, D), row),
            pl.BlockSpec((None, N_KV_HEADS, tm, ATTN_WIDTH // N_KV_HEADS), lambda b, i: (b, 0, i, 0)),
            pl.BlockSpec((None, tm, ATTN_WIDTH), row),
            pl.BlockSpec((None, tm, HG_WIDTH), row),
            pl.BlockSpec((1, ATTN_WIDTH), const2),
            pl.BlockSpec((ATTN_WIDTH + HG_WIDTH, D), const2),
            pl.BlockSpec((1, D), const2),
            pl.BlockSpec((1, D), const2),
        ],
        out_specs=pl.BlockSpec((None, tm, D), row),
        out_shape=jax.ShapeDtypeStruct((B, S, D), x.dtype),
        compiler_params=pltpu.CompilerParams(
            dimension_semantics=("arbitrary", "arbitrary"),
            vmem_limit_bytes=VMEM_LIMIT),
        name="out_proj_ln",
    )(x, attn, ag, hgb, anw, w_out_b, lnw, lnb)


def _rope_tables(seq_len):
    n_rows = seq_len // GRID_W
    inv = ROPE_THETA ** (-jnp.arange(0, AXIS_DIM, 2, dtype=F32) / AXIS_DIM)
    lane = jnp.arange(LANES, dtype=jnp.int32)
    d = lane % HEAD_DIM
    use_col = (d // AXIS_DIM) == 1
    dd = d % AXIS_DIM
    first = dd < (AXIS_DIM // 2)
    freq = inv[dd % (AXIS_DIM // 2)]
    idx = jnp.arange(max(n_rows, GRID_W), dtype=jnp.int32).astype(F32)
    ang = idx[:, None] * freq[None, :]
    cos_i = jnp.cos(ang)
    sin_i = jnp.where(first[None, :], -jnp.sin(ang), jnp.sin(ang))
    cos_i, sin_i = lax.optimization_barrier((cos_i, sin_i))

    def expand(tab):
        by_row = jnp.repeat(tab[:n_rows], GRID_W, axis=0)
        by_col = jnp.tile(tab[:GRID_W], (n_rows, 1))
        return jnp.where(use_col[None, :], by_col, by_row)

    return expand(cos_i), expand(sin_i)


def kernel(x, w_in, q_norm_w, k_norm_w, attn_norm_w, hg_lb_logits, hg_norm_w, w_out, ln_w, ln_b):
    B, S, D = x.shape
    assert w_in.shape == (DEPTH, D, IN_WIDTH) and DEPTH == 1
    cos_t, sin_t = _rope_tables(S)
    qw = jnp.tile(q_norm_w[0].astype(F32), N_Q_HEADS).reshape(1, ATTN_WIDTH)
    kw = jnp.tile(k_norm_w[0].astype(F32), N_KV_HEADS).reshape(1, KV_WIDTH)
    lbl = hg_lb_logits.astype(F32).reshape(2 * (DEPTH + 1), HG_WIDTH)
    hid = jnp.arange(ATTN_WIDTH, dtype=jnp.int32) // HEAD_DIM
    seg = jnp.where(hid[:, None] == hid[None, :], 1.0 / HEAD_DIM, 0.0).astype(BF16)

    qt, k, vt, ag, hq, hi, lf, hg = _proj_call(
        x, w_in[0].astype(BF16), cos_t, sin_t, qw, kw, lbl, seg)
    attn = _attn_call(qt, k, vt)
    hgb = _hgrn_call(hq, hi, lf, hg, hg_norm_w[0].astype(F32).reshape(1, HG_EXPAND))
    return _out_call(
        x, attn, ag, hgb,
        attn_norm_w[0].astype(F32).reshape(1, ATTN_WIDTH),
        w_out[0].astype(BF16),
        ln_w[0].astype(F32).reshape(1, D),
        ln_b[0].astype(F32).reshape(1, D))
```

```python
import jax
import jax.numpy as jnp
from jax import lax
from jax.experimental import pallas as pl
from jax.experimental.pallas import tpu as pltpu

F32 = jnp.float32
BF16 = jnp.bfloat16

GRID_W = 64
ATTN_WIDTH = 512
HEAD_DIM = 64
N_Q_HEADS = 8
N_KV_HEADS = 2
KV_WIDTH = 128
AXIS_DIM = 32
ROPE_THETA = 10000.0
HG_WIDTH = 512
HG_EXPAND = 128
HG_HEADS = 4
HG_CHUNK = 64
RMS_EPS = 1e-6
LN_EPS = 1e-5
DEPTH = 1
DEEPNORM_ALPHA = (2 * DEPTH) ** 0.25
IN_WIDTH = 3840
LOG2E = 1.4426950408889634
Q_SCALE = HEAD_DIM ** -0.5 * LOG2E

LANES = 128
BF16_SUBLANES = 16
VMEM_LIMIT = 56 * 1024 * 1024

PROJ_TM = 512
ATTN_TQ = 128
ATTN_TK = 256
HG_BLOCK = 32
HG_BLOCK_OUT = 64
OUT_TM = 1024


def _exact_zero(x):
    bits = lax.bitcast_convert_type(x, jnp.uint32)
    return ((bits >> 16) >> 16).astype(F32)


def _sigmoid(x):
    return 1.0 / (1.0 + jnp.exp(-x))


def _proj_kernel(x_ref, w_ref, cos_ref, sin_ref, qw_ref, kw_ref, lbl_ref, seg_ref,
                 qt_ref, k_ref, vt_ref, ag_ref, hq_ref, hi_ref, lf_ref, hg_ref):
    xb = x_ref[...].astype(BF16)
    tm = xb.shape[0]

    def proj(a, b):
        return jnp.dot(xb, w_ref[:, a:b], preferred_element_type=F32)

    cosv = cos_ref[...]
    sinv = sin_ref[...]
    lane = lax.broadcasted_iota(jnp.int32, (tm, LANES), 1)
    first_half = (lane % AXIS_DIM) < (AXIS_DIM // 2)

    def norm_rope(a, w_row, seg, scale):
        sq = (a * a).astype(BF16)
        blk = min(a.shape[1], 2 * LANES)
        ms = jnp.concatenate(
            [jnp.dot(sq[:, j:j + blk], seg[j:j + blk, j:j + blk], preferred_element_type=F32)
             for j in range(0, a.shape[1], blk)], axis=1)
        y = a * lax.rsqrt(ms + RMS_EPS) * w_row
        outs = []
        for c in range(a.shape[1] // LANES):
            yc = y[:, c * LANES:(c + 1) * LANES]
            partner = jnp.where(first_half,
                                pltpu.roll(yc, LANES - AXIS_DIM // 2, 1),
                                pltpu.roll(yc, AXIS_DIM // 2, 1))
            outs.append((yc * cosv + partner * sinv) * scale)
        return outs

    aq = proj(0, ATTN_WIDTH)
    qs = norm_rope(aq, qw_ref[...], seg_ref[...], Q_SCALE)
    for c, qc in enumerate(qs):
        qt_ref[c * LANES:(c + 1) * LANES, :] = qc.T.astype(BF16)
    akv = proj(ATTN_WIDTH, ATTN_WIDTH + 2 * KV_WIDTH)
    kk = norm_rope(akv[:, :KV_WIDTH], kw_ref[...], seg_ref[:KV_WIDTH, :KV_WIDTH], 1.0)[0]
    k_ref[...] = kk.astype(BF16)
    vt_ref[...] = akv[:, KV_WIDTH:].T.astype(BF16)

    base = ATTN_WIDTH + 2 * KV_WIDTH
    ag = proj(base, base + ATTN_WIDTH)
    ag_ref[...] = (ag * _sigmoid(ag)).astype(BF16)
    base += ATTN_WIDTH
    hq = proj(base, base + HG_WIDTH)
    hq_ref[...] = (hq * _sigmoid(hq) * (HG_EXPAND ** -0.5)).astype(BF16)
    base += HG_WIDTH
    hi_ref[...] = proj(base, base + HG_WIDTH).astype(BF16)
    base += HG_WIDTH
    lbl = lbl_ref[...]
    for d in range(2):
        l0 = lbl[2 * d:2 * d + 1, :]
        l1 = lbl[2 * d + 1:2 * d + 2, :]
        mx = jnp.maximum(l0, l1)
        e0 = jnp.exp(l0 - mx)
        e1 = jnp.exp(l1 - mx)
        lb = e0 / (e0 + e1)
        z = proj(base, base + HG_WIDTH)
        lf_ref[:, d * HG_WIDTH:(d + 1) * HG_WIDTH] = jnp.log(lb + (1.0 - lb) * _sigmoid(z)) * LOG2E
        base += HG_WIDTH
    hg = proj(base, base + HG_WIDTH)
    hg_ref[...] = (hg * _sigmoid(hg)).astype(BF16)


def _proj_call(x, w_in_b, cos_t, sin_t, qw, kw, lbl, seg):
    B, S, D = x.shape
    tm = PROJ_TM
    nt = S // tm
    row = lambda b, i: (b, i, 0)
    const2 = lambda b, i: (0, 0)
    out_shape = (
        jax.ShapeDtypeStruct((B, ATTN_WIDTH, S), BF16),
        jax.ShapeDtypeStruct((B, S, KV_WIDTH), BF16),
        jax.ShapeDtypeStruct((B, KV_WIDTH, S), BF16),
        jax.ShapeDtypeStruct((B, S, ATTN_WIDTH), BF16),
        jax.ShapeDtypeStruct((B, S, HG_WIDTH), BF16),
        jax.ShapeDtypeStruct((B, S, HG_WIDTH), BF16),
        jax.ShapeDtypeStruct((B, S, 2 * HG_WIDTH), F32),
        jax.ShapeDtypeStruct((B, S, HG_WIDTH), BF16),
    )
    in_specs = [
        pl.BlockSpec((None, tm, D), row),
        pl.BlockSpec((D, IN_WIDTH), const2),
        pl.BlockSpec((tm, LANES), lambda b, i: (i, 0)),
        pl.BlockSpec((tm, LANES), lambda b, i: (i, 0)),
        pl.BlockSpec((1, ATTN_WIDTH), const2),
        pl.BlockSpec((1, KV_WIDTH), const2),
        pl.BlockSpec((4, HG_WIDTH), const2),
        pl.BlockSpec((ATTN_WIDTH, ATTN_WIDTH), const2),
    ]
    out_specs = (
        pl.BlockSpec((None, ATTN_WIDTH, tm), lambda b, i: (b, 0, i)),
        pl.BlockSpec((None, tm, KV_WIDTH), row),
        pl.BlockSpec((None, KV_WIDTH, tm), lambda b, i: (b, 0, i)),
        pl.BlockSpec((None, tm, ATTN_WIDTH), row),
        pl.BlockSpec((None, tm, HG_WIDTH), row),
        pl.BlockSpec((None, tm, HG_WIDTH), row),
        pl.BlockSpec((None, tm, 2 * HG_WIDTH), row),
        pl.BlockSpec((None, tm, HG_WIDTH), row),
    )
    return pl.pallas_call(
        _proj_kernel,
        grid=(B, nt),
        in_specs=in_specs,
        out_specs=out_specs,
        out_shape=out_shape,
        compiler_params=pltpu.CompilerParams(
            dimension_semantics=("arbitrary", "arbitrary"),
            vmem_limit_bytes=VMEM_LIMIT),
        name="in_proj",
    )(x, w_in_b, cos_t, sin_t, qw, kw, lbl, seg)


def _attn_kernel(qt_ref, k_ref, vt_ref, o_ref, s_ref, p_ref, vx_ref):
    S = k_ref.shape[0]
    G = N_Q_HEADS // N_KV_HEADS
    tq, tk = ATTN_TQ, ATTN_TK
    cols = G * tq
    n_units = N_KV_HEADS * (S // tq)
    n_c = S // tk
    vrows = vx_ref.shape[1]

    for g in range(N_KV_HEADS):
        vx_ref[g, 0:HEAD_DIM, :] = vt_ref[g * HEAD_DIM:(g + 1) * HEAD_DIM, :]
        vx_ref[g, HEAD_DIM:vrows, :] = jnp.ones((vrows - HEAD_DIM, S), BF16)

    row_group = lax.broadcasted_iota(jnp.int32, (N_KV_HEADS * HEAD_DIM, cols), 0) // HEAD_DIM

    def q_weights(u):
        i, g = u >> 1, u & 1
        c0 = pl.multiple_of(i * tq, tq)
        qs = jnp.concatenate(
            [qt_ref[pl.ds(pl.multiple_of((G * g + h) * HEAD_DIM, HEAD_DIM), HEAD_DIM),
                    pl.ds(c0, tq)] for h in range(G)], axis=1)
        return jnp.where(row_group == g, jnp.concatenate([qs, qs], axis=0), 0.0).astype(BF16)

    def stage(u, m_prev, do_scores=True, do_probs=True, do_pv=True):
        if do_scores:
            w = q_weights(u)
            mx = jnp.full((8, cols), -jnp.inf, F32)
        if do_pv:
            g_pv = (u - 2) & 1
            acc = jnp.zeros((vrows, cols), F32)
        acc_after = []
        for c in range(n_c + 1):
            if do_pv and c < n_c:
                rows = pl.ds(c * tk, tk)
                acc = acc + jnp.dot(vx_ref[g_pv, :, rows], p_ref[rows, :],
                                    preferred_element_type=F32)
                acc_after.append(acc)
            if c == 0:
                continue
            rows = pl.ds((c - 1) * tk, tk)
            m_c = m_prev
            if do_pv and do_probs:
                done = acc_after[c - 1]
                guard = sum(_exact_zero(done[0:8, t * 2 * LANES:(t * 2 + 1) * LANES])
                            for t in range(cols // (2 * LANES)))
                m_c = m_prev + jnp.tile(guard[0:1, :], (1, cols // LANES))
            if do_probs:
                p_ref[rows, :] = jnp.exp2(s_ref[rows, :] - m_c).astype(BF16)
            if do_scores:
                s = jnp.dot(k_ref[rows, :], w, preferred_element_type=F32)
                s_ref[rows, :] = s
                mx = jnp.maximum(mx, jnp.max(s.reshape(tk // 8, 8, cols), axis=0))
        if do_pv:
            o_t = acc[0:HEAD_DIM, :] / acc[HEAD_DIM:HEAD_DIM + 1, :]
            o = o_t.T
            blk = jnp.concatenate([o[h * tq:(h + 1) * tq, :] for h in range(G)], axis=1)
            r0 = pl.multiple_of(((u - 2) >> 1) * tq, tq)
            o_ref[g_pv, pl.ds(r0, tq), :] = blk.astype(o_ref.dtype)
        if do_scores:
            return jnp.max(mx, axis=0, keepdims=True)
        return m_prev

    zero = jnp.int32(0)
    m = stage(zero, None, do_probs=False, do_pv=False)
    m = stage(zero + 1, m, do_pv=False)
    m = lax.fori_loop(2, n_units, stage, m)
    m = stage(zero + n_units, m, do_scores=False)
    stage(zero + n_units + 1, m, do_scores=False, do_probs=False)


def _attn_call(qt, k, vt):
    B, _, S = qt.shape
    G = N_Q_HEADS // N_KV_HEADS
    cols = G * ATTN_TQ
    gw = G * HEAD_DIM
    whole = lambda b: (b, 0, 0)
    return pl.pallas_call(
        _attn_kernel,
        grid=(B,),
        in_specs=[
            pl.BlockSpec((None, ATTN_WIDTH, S), whole),
            pl.BlockSpec((None, S, KV_WIDTH), whole),
            pl.BlockSpec((None, KV_WIDTH, S), whole),
        ],
        out_specs=pl.BlockSpec((None, N_KV_HEADS, S, gw), lambda b: (b, 0, 0, 0)),
        out_shape=jax.ShapeDtypeStruct((B, N_KV_HEADS, S, gw), BF16),
        scratch_shapes=[
            pltpu.VMEM((S, cols), F32),
            pltpu.VMEM((S, cols), BF16),
            pltpu.VMEM((N_KV_HEADS, HEAD_DIM + BF16_SUBLANES, S), BF16),
        ],
        compiler_params=pltpu.CompilerParams(
            dimension_semantics=("arbitrary",),
            vmem_limit_bytes=VMEM_LIMIT),
        name="attention",
    )(qt, k, vt)


def _hgrn_kernel(q_ref, i_ref, lff_ref, lfb_ref, g_ref, nw_ref, o_ref,
                 qm_ref, km_ref, qb_ref, kd_ref, dec_ref, ut_ref, st_ref):
    S = q_ref.shape[0]
    C = HG_CHUNK
    K = HG_EXPAND
    n_chunks = S // C
    nb = HG_BLOCK
    rows_b = nb * C
    r = lax.broadcasted_iota(jnp.int32, (C, C), 0)
    c = lax.broadcasted_iota(jnp.int32, (C, C), 1)
    lower = r >= c
    upper = r <= c
    tri2 = [jnp.concatenate([t, t], axis=1) for t in
            (jnp.where(lower, 1.0, 0.0).astype(BF16), jnp.where(upper, 1.0, 0.0).astype(BF16))]
    mid = (C // 2, C - 1 - C // 2)
    last = (C - 1, 0)

    def block_rows(t):
        return t * rows_b if isinstance(t, int) else pl.multiple_of(t * rows_b, rows_b)

    def prefix_sums(t):
        rows = pl.ds(block_rows(t), rows_b)
        lfs = (lff_ref[rows, :], lfb_ref[rows, :])
        b3s = []
        for d, lf in enumerate(lfs):
            hi = lf.astype(BF16)
            lo = (lf - hi.astype(F32)).astype(BF16)
            b3s.append(jnp.stack([
                jnp.dot(tri2[d],
                        jnp.concatenate([hi[j * C:(j + 1) * C], lo[j * C:(j + 1) * C]], axis=0),
                        preferred_element_type=F32)
                for j in range(nb)]))
        return lfs, b3s

    def state_updates(t):
        r0 = block_rows(t)
        for j in range(nb):
            cr = pl.ds(r0 + j * C, C)
            ivt = i_ref[cr, :].astype(F32).T.astype(BF16)
            ut_ref[t * nb + j] = jnp.dot(ivt, kd_ref[cr, :], preferred_element_type=F32)

    def scaled_operands(t, lfs, b3s):
        rows = pl.ds(block_rows(t), rows_b)
        q3 = q_ref[rows, :].astype(F32).reshape(nb, C, K)
        for d, (lf, b3) in enumerate(zip(lfs, b3s)):
            k3 = (1.0 - jnp.exp2(lf)).reshape(nb, C, K)
            bm = b3[:, mid[d]:mid[d] + 1, :]
            bl = b3[:, last[d]:last[d] + 1, :]
            e = jnp.exp2(b3 - bm)
            einv = 1.0 / e
            qm = q3 * e
            km = k3 * einv
            qm_ref[d, rows, :] = qm.reshape(rows_b, K).astype(BF16)
            km_ref[d, rows, :] = km.reshape(rows_b, K).astype(BF16)
            qb_ref[rows, d * K:(d + 1) * K] = (qm * jnp.exp2(bm)).reshape(rows_b, K).astype(BF16)
            kd_ref[rows, d * K:(d + 1) * K] = (km * jnp.exp2(bl - bm)).reshape(rows_b, K).astype(BF16)
            c0 = t * nb if isinstance(t, int) else pl.multiple_of(t * nb, nb)
            dec_ref[d, pl.ds(c0, nb), :] = jnp.exp2(bl).reshape(nb, K)

    n_blocks = n_chunks // nb
    scaled_operands(0, *prefix_sums(0))

    def prep(t, _):
        sums = prefix_sums(t)
        state_updates(t - 1)
        scaled_operands(t, *sums)
        return 0

    lax.fori_loop(1, n_blocks, prep, 0)
    state_updates(n_blocks - 1)

    def chain(n, states):
        sf, sb = states
        m = n_chunks - 1 - n
        st_ref[n, :, 0:K] = sf.astype(BF16)
        st_ref[m, :, K:2 * K] = sb.astype(BF16)
        return (dec_ref[0, pl.ds(n, 1), :] * sf + ut_ref[n, :, 0:K],
                dec_ref[1, pl.ds(m, 1), :] * sb + ut_ref[m, :, K:2 * K])

    zero_state = jnp.zeros((K, K), F32)
    lax.fori_loop(0, n_chunks, chain, (zero_state, zero_state))

    nw = nw_ref[...]
    nt_dims = (((1,), (1,)), ((), ()))

    no = HG_BLOCK_OUT

    def outp(t, _):
        r0 = pl.multiple_of(t * (no * C), no * C)
        crs = [pl.ds(r0 + j * C, C) for j in range(no)]
        a_f = [lax.dot_general(qm_ref[0, cr, :], km_ref[0, cr, :], nt_dims,
                               preferred_element_type=F32) for cr in crs]
        a_b = [lax.dot_general(qm_ref[1, cr, :], km_ref[1, cr, :], nt_dims,
                               preferred_element_type=F32) for cr in crs]
        a = [(jnp.where(lower, f, 0.0) + jnp.where(upper, b, 0.0)).astype(BF16)
             for f, b in zip(a_f, a_b)]
        outs = [jnp.dot(a[j], i_ref[cr, :], preferred_element_type=F32)
                + lax.dot_general(qb_ref[cr, :], st_ref[t * no + j], nt_dims,
                                  preferred_element_type=F32)
                for j, cr in enumerate(crs)]
        for cr, o in zip(crs, outs):
            ms = jnp.mean(o * o, axis=-1, keepdims=True)
            y = o * lax.rsqrt(ms + RMS_EPS) * nw
            o_ref[cr, :] = (y * g_ref[cr, :].astype(F32)).astype(o_ref.dtype)
        return 0

    lax.fori_loop(0, n_chunks // no, outp, 0)


def _hgrn_call(hq, hi, lf, hg, nw):
    B, S, _ = hq.shape
    n_chunks = S // HG_CHUNK
    K = HG_EXPAND
    blk = lambda b, h: (b, 0, h)
    return pl.pallas_call(
        _hgrn_kernel,
        grid=(B, HG_HEADS),
        in_specs=[
            pl.BlockSpec((None, S, K), blk),
            pl.BlockSpec((None, S, K), blk),
            pl.BlockSpec((None, S, K), blk),
            pl.BlockSpec((None, S, K), lambda b, h: (b, 0, HG_HEADS + h)),
            pl.BlockSpec((None, S, K), blk),
            pl.BlockSpec((1, K), lambda b, h: (0, 0)),
        ],
        out_specs=pl.BlockSpec((None, S, K), blk),
        out_shape=jax.ShapeDtypeStruct((B, S, HG_WIDTH), BF16),
        scratch_shapes=[
            pltpu.VMEM((2, S, K), BF16),
            pltpu.VMEM((2, S, K), BF16),
            pltpu.VMEM((S, 2 * K), BF16),
            pltpu.VMEM((S, 2 * K), BF16),
            pltpu.VMEM((2, n_chunks, K), F32),
            pltpu.VMEM((n_chunks, K, 2 * K), F32),
            pltpu.VMEM((n_chunks, K, 2 * K), BF16),
        ],
        compiler_params=pltpu.CompilerParams(
            dimension_semantics=("arbitrary", "arbitrary"),
            vmem_limit_bytes=VMEM_LIMIT),
        name="hgrn2",
    )(hq, hi, lf, lf, hg, nw)


def _out_kernel(x_ref, attn_ref, ag_ref, hgb_ref, anw_ref, w_ref, lnw_ref, lnb_ref, o_ref):
    a = jnp.concatenate([attn_ref[g] for g in range(N_KV_HEADS)], axis=1).astype(F32)
    ms = jnp.mean(a * a, axis=-1, keepdims=True)
    ab = a * lax.rsqrt(ms + RMS_EPS) * anw_ref[...] * ag_ref[...].astype(F32)
    y = (jnp.dot(ab.astype(BF16), w_ref[:ATTN_WIDTH, :], preferred_element_type=F32)
         + jnp.dot(hgb_ref[...], w_ref[ATTN_WIDTH:, :], preferred_element_type=F32))
    z = DEEPNORM_ALPHA * x_ref[...] + y
    mu = jnp.mean(z, axis=-1, keepdims=True)
    zc = z - mu
    var = jnp.mean(zc * zc, axis=-1, keepdims=True)
    o_ref[...] = zc * lax.rsqrt(var + LN_EPS) * lnw_ref[...] + lnb_ref[...]


def _out_call(x, attn, ag, hgb, anw, w_out_b, lnw, lnb):
    B, S, D = x.shape
    tm = OUT_TM
    row = lambda b, i: (b, i, 0)
    const2 = lambda b, i: (0, 0)
    return pl.pallas_call(
        _out_kernel,
        grid=(B, S // tm),
        in_specs=[
            pl.BlockSpec((None, tm, D), row),
            pl.BlockSpec((None, N_KV_HEADS, tm, ATTN_WIDTH // N_KV_HEADS), lambda b, i: (b, 0, i, 0)),
            pl.BlockSpec((None, tm, ATTN_WIDTH), row),
            pl.BlockSpec((None, tm, HG_WIDTH), row),
            pl.BlockSpec((1, ATTN_WIDTH), const2),
            pl.BlockSpec((ATTN_WIDTH + HG_WIDTH, D), const2),
            pl.BlockSpec((1, D), const2),
            pl.BlockSpec((1, D), const2),
        ],
        out_specs=pl.BlockSpec((None, tm, D), row),
        out_shape=jax.ShapeDtypeStruct((B, S, D), x.dtype),
        compiler_params=pltpu.CompilerParams(
            dimension_semantics=("arbitrary", "arbitrary"),
            vmem_limit_bytes=VMEM_LIMIT),
        name="out_proj_ln",
    )(x, attn, ag, hgb, anw, w_out_b, lnw, lnb)


def _rope_tables(seq_len):
    n_rows = seq_len // GRID_W
    inv = ROPE_THETA ** (-jnp.arange(0, AXIS_DIM, 2, dtype=F32) / AXIS_DIM)
    lane = jnp.arange(LANES, dtype=jnp.int32)
    d = lane % HEAD_DIM
    use_col = (d // AXIS_DIM) == 1
    dd = d % AXIS_DIM
    first = dd < (AXIS_DIM // 2)
    freq = inv[dd % (AXIS_DIM // 2)]
    idx = jnp.arange(max(n_rows, GRID_W), dtype=jnp.int32).astype(F32)
    ang = idx[:, None] * freq[None, :]
    cos_i = jnp.cos(ang)
    sin_i = jnp.where(first[None, :], -jnp.sin(ang), jnp.sin(ang))
    cos_i, sin_i = lax.optimization_barrier((cos_i, sin_i))

    def expand(tab):
        by_row = jnp.repeat(tab[:n_rows], GRID_W, axis=0)
        by_col = jnp.tile(tab[:GRID_W], (n_rows, 1))
        return jnp.where(use_col[None, :], by_col, by_row)

    return expand(cos_i), expand(sin_i)


def kernel(x, w_in, q_norm_w, k_norm_w, attn_norm_w, hg_lb_logits, hg_norm_w, w_out, ln_w, ln_b):
    B, S, D = x.shape
    assert w_in.shape == (DEPTH, D, IN_WIDTH) and DEPTH == 1
    cos_t, sin_t = _rope_tables(S)
    qw = jnp.tile(q_norm_w[0].astype(F32), N_Q_HEADS).reshape(1, ATTN_WIDTH)
    kw = jnp.tile(k_norm_w[0].astype(F32), N_KV_HEADS).reshape(1, KV_WIDTH)
    lbl = hg_lb_logits.astype(F32).reshape(2 * (DEPTH + 1), HG_WIDTH)
    hid = jnp.arange(ATTN_WIDTH, dtype=jnp.int32) // HEAD_DIM
    seg = jnp.where(hid[:, None] == hid[None, :], 1.0 / HEAD_DIM, 0.0).astype(BF16)

    qt, k, vt, ag, hq, hi, lf, hg = _proj_call(
        x, w_in[0].astype(BF16), cos_t, sin_t, qw, kw, lbl, seg)
    attn = _attn_call(qt, k, vt)
    hgb = _hgrn_call(hq, hi, lf, hg, hg_norm_w[0].astype(F32).reshape(1, HG_EXPAND))
    return _out_call(
        x, attn, ag, hgb,
        attn_norm_w[0].astype(F32).reshape(1, ATTN_WIDTH),
        w_out[0].astype(BF16),
        ln_w[0].astype(F32).reshape(1, D),
        ln_b[0].astype(F32).reshape(1, D))
```

```python
import jax
import jax.numpy as jnp
from jax import lax
from jax.experimental import pallas as pl
from jax.experimental.pallas import tpu as pltpu

F32 = jnp.float32
BF16 = jnp.bfloat16

GRID_W = 64
ATTN_WIDTH = 512
HEAD_DIM = 64
N_Q_HEADS = 8
N_KV_HEADS = 2
KV_WIDTH = 128
AXIS_DIM = 32
ROPE_THETA = 10000.0
HG_WIDTH = 512
HG_EXPAND = 128
HG_HEADS = 4
HG_CHUNK = 64
RMS_EPS = 1e-6
LN_EPS = 1e-5
DEPTH = 1
DEEPNORM_ALPHA = (2 * DEPTH) ** 0.25
IN_WIDTH = 3840
LOG2E = 1.4426950408889634
Q_SCALE = HEAD_DIM ** -0.5 * LOG2E

LANES = 128
BF16_SUBLANES = 16
VMEM_LIMIT = 56 * 1024 * 1024

PROJ_TM = 512
ATTN_TQ = 128
ATTN_LAG = 2
ATTN_TK = 256
HG_BLOCK = 32
HG_BLOCK_OUT = 64
OUT_TM = 1024


def _exact_zero(x):
    bits = lax.bitcast_convert_type(x, jnp.uint32)
    return ((bits >> 16) >> 16).astype(F32)


def _sigmoid(x):
    return 1.0 / (1.0 + jnp.exp(-x))


def _proj_kernel(x_ref, w_ref, cos_ref, sin_ref, qw_ref, kw_ref, lbl_ref, seg_ref,
                 qt_ref, k_ref, vt_ref, ag_ref, hq_ref, hi_ref, lf_ref, hg_ref):
    xb = x_ref[...].astype(BF16)
    tm = xb.shape[0]

    def proj(a, b):
        return jnp.dot(xb, w_ref[:, a:b], preferred_element_type=F32)

    cosv = cos_ref[...]
    sinv = sin_ref[...]
    lane = lax.broadcasted_iota(jnp.int32, (tm, LANES), 1)
    first_half = (lane % AXIS_DIM) < (AXIS_DIM // 2)

    def norm_rope(a, w_row, seg, scale):
        sq = (a * a).astype(BF16)
        blk = min(a.shape[1], 2 * LANES)
        ms = jnp.concatenate(
            [jnp.dot(sq[:, j:j + blk], seg[j:j + blk, j:j + blk], preferred_element_type=F32)
             for j in range(0, a.shape[1], blk)], axis=1)
        y = a * lax.rsqrt(ms + RMS_EPS) * w_row
        outs = []
        for c in range(a.shape[1] // LANES):
            yc = y[:, c * LANES:(c + 1) * LANES]
            partner = jnp.where(first_half,
                                pltpu.roll(yc, LANES - AXIS_DIM // 2, 1),
                                pltpu.roll(yc, AXIS_DIM // 2, 1))
            outs.append((yc * cosv + partner * sinv) * scale)
        return outs

    aq = proj(0, ATTN_WIDTH)
    qs = norm_rope(aq, qw_ref[...], seg_ref[...], Q_SCALE)
    for c, qc in enumerate(qs):
        qt_ref[c * LANES:(c + 1) * LANES, :] = qc.T.astype(BF16)
    akv = proj(ATTN_WIDTH, ATTN_WIDTH + 2 * KV_WIDTH)
    kk = norm_rope(akv[:, :KV_WIDTH], kw_ref[...], seg_ref[:KV_WIDTH, :KV_WIDTH], 1.0)[0]
    k_ref[...] = kk.astype(BF16)
    vt_ref[...] = akv[:, KV_WIDTH:].T.astype(BF16)

    base = ATTN_WIDTH + 2 * KV_WIDTH
    ag = proj(base, base + ATTN_WIDTH)
    ag_ref[...] = (ag * _sigmoid(ag)).astype(BF16)
    base += ATTN_WIDTH
    hq = proj(base, base + HG_WIDTH)
    hq_ref[...] = (hq * _sigmoid(hq) * (HG_EXPAND ** -0.5)).astype(BF16)
    base += HG_WIDTH
    hi_ref[...] = proj(base, base + HG_WIDTH).astype(BF16)
    base += HG_WIDTH
    lbl = lbl_ref[...]
    for d in range(2):
        l0 = lbl[2 * d:2 * d + 1, :]
        l1 = lbl[2 * d + 1:2 * d + 2, :]
        mx = jnp.maximum(l0, l1)
        e0 = jnp.exp(l0 - mx)
        e1 = jnp.exp(l1 - mx)
        lb = e0 / (e0 + e1)
        z = proj(base, base + HG_WIDTH)
        lf_ref[:, d * HG_WIDTH:(d + 1) * HG_WIDTH] = jnp.log(lb + (1.0 - lb) * _sigmoid(z)) * LOG2E
        base += HG_WIDTH
    hg = proj(base, base + HG_WIDTH)
    hg_ref[...] = (hg * _sigmoid(hg)).astype(BF16)


def _proj_call(x, w_in_b, cos_t, sin_t, qw, kw, lbl, seg):
    B, S, D = x.shape
    tm = PROJ_TM
    nt = S // tm
    row = lambda b, i: (b, i, 0)
    const2 = lambda b, i: (0, 0)
    out_shape = (
        jax.ShapeDtypeStruct((B, ATTN_WIDTH, S), BF16),
        jax.ShapeDtypeStruct((B, S, KV_WIDTH), BF16),
        jax.ShapeDtypeStruct((B, KV_WIDTH, S), BF16),
        jax.ShapeDtypeStruct((B, S, ATTN_WIDTH), BF16),
        jax.ShapeDtypeStruct((B, S, HG_WIDTH), BF16),
        jax.ShapeDtypeStruct((B, S, HG_WIDTH), BF16),
        jax.ShapeDtypeStruct((B, S, 2 * HG_WIDTH), F32),
        jax.ShapeDtypeStruct((B, S, HG_WIDTH), BF16),
    )
    in_specs = [
        pl.BlockSpec((None, tm, D), row),
        pl.BlockSpec((D, IN_WIDTH), const2),
        pl.BlockSpec((tm, LANES), lambda b, i: (i, 0)),
        pl.BlockSpec((tm, LANES), lambda b, i: (i, 0)),
        pl.BlockSpec((1, ATTN_WIDTH), const2),
        pl.BlockSpec((1, KV_WIDTH), const2),
        pl.BlockSpec((4, HG_WIDTH), const2),
        pl.BlockSpec((ATTN_WIDTH, ATTN_WIDTH), const2),
    ]
    out_specs = (
        pl.BlockSpec((None, ATTN_WIDTH, tm), lambda b, i: (b, 0, i)),
        pl.BlockSpec((None, tm, KV_WIDTH), row),
        pl.BlockSpec((None, KV_WIDTH, tm), lambda b, i: (b, 0, i)),
        pl.BlockSpec((None, tm, ATTN_WIDTH), row),
        pl.BlockSpec((None, tm, HG_WIDTH), row),
        pl.BlockSpec((None, tm, HG_WIDTH), row),
        pl.BlockSpec((None, tm, 2 * HG_WIDTH), row),
        pl.BlockSpec((None, tm, HG_WIDTH), row),
    )
    return pl.pallas_call(
        _proj_kernel,
        grid=(B, nt),
        in_specs=in_specs,
        out_specs=out_specs,
        out_shape=out_shape,
        compiler_params=pltpu.CompilerParams(
            dimension_semantics=("arbitrary", "arbitrary"),
            vmem_limit_bytes=VMEM_LIMIT),
        name="in_proj",
    )(x, w_in_b, cos_t, sin_t, qw, kw, lbl, seg)


def _attn_kernel(qt_ref, k_ref, vt_ref, o_ref, s_ref, p_ref, vx_ref):
    S = k_ref.shape[0]
    G = N_Q_HEADS // N_KV_HEADS
    tq, tk = ATTN_TQ, ATTN_TK
    cols = G * tq
    n_units = N_KV_HEADS * (S // tq)
    n_c = S // tk
    vrows = vx_ref.shape[1]

    for g in range(N_KV_HEADS):
        vx_ref[g, 0:HEAD_DIM, :] = vt_ref[g * HEAD_DIM:(g + 1) * HEAD_DIM, :]
        vx_ref[g, HEAD_DIM:vrows, :] = jnp.ones((vrows - HEAD_DIM, S), BF16)

    row_group = lax.broadcasted_iota(jnp.int32, (N_KV_HEADS * HEAD_DIM, cols), 0) // HEAD_DIM

    def q_weights(u):
        i, g = u >> 1, u & 1
        c0 = pl.multiple_of(i * tq, tq)
        qs = jnp.concatenate(
            [qt_ref[pl.ds(pl.multiple_of((G * g + h) * HEAD_DIM, HEAD_DIM), HEAD_DIM),
                    pl.ds(c0, tq)] for h in range(G)], axis=1)
        return jnp.where(row_group == g, jnp.concatenate([qs, qs], axis=0), 0.0).astype(BF16)

    def stage(u, m_prev, do_scores=True, do_probs=True, do_pv=True):
        if do_scores:
            w = q_weights(u)
            mx = jnp.full((8, cols), -jnp.inf, F32)
        if do_pv:
            g_pv = (u - 2) & 1
            acc = jnp.zeros((vrows, cols), F32)
        acc_after = []
        for c in range(n_c + ATTN_LAG):
            if do_pv and c < n_c:
                rows = pl.ds(c * tk, tk)
                acc = acc + jnp.dot(vx_ref[g_pv, :, rows], p_ref[rows, :],
                                    preferred_element_type=F32)
                acc_after.append(acc)
            if c < ATTN_LAG:
                continue
            rows = pl.ds((c - ATTN_LAG) * tk, tk)
            m_c = m_prev
            if do_pv and do_probs:
                done = acc_after[c - ATTN_LAG]
                guard = sum(_exact_zero(done[0:8, t * 2 * LANES:(t * 2 + 1) * LANES])
                            for t in range(cols // (2 * LANES)))
                m_c = m_prev + jnp.tile(guard[0:1, :], (1, cols // LANES))
            if do_probs:
                p_ref[rows, :] = jnp.exp2(s_ref[rows, :] - m_c).astype(BF16)
            if do_scores:
                s = jnp.dot(k_ref[rows, :], w, preferred_element_type=F32)
                s_ref[rows, :] = s
                mx = jnp.maximum(mx, jnp.max(s.reshape(tk // 8, 8, cols), axis=0))
        if do_pv:
            o_t = acc[0:HEAD_DIM, :] / acc[HEAD_DIM:HEAD_DIM + 1, :]
            o = o_t.T
            blk = jnp.concatenate([o[h * tq:(h + 1) * tq, :] for h in range(G)], axis=1)
            r0 = pl.multiple_of(((u - 2) >> 1) * tq, tq)
            o_ref[g_pv, pl.ds(r0, tq), :] = blk.astype(o_ref.dtype)
        if do_scores:
            return jnp.max(mx, axis=0, keepdims=True)
        return m_prev

    zero = jnp.int32(0)
    m = stage(zero, None, do_probs=False, do_pv=False)
    m = stage(zero + 1, m, do_pv=False)
    m = lax.fori_loop(2, n_units, stage, m)
    m = stage(zero + n_units, m, do_scores=False)
    stage(zero + n_units + 1, m, do_scores=False, do_probs=False)


def _attn_call(qt, k, vt):
    B, _, S = qt.shape
    G = N_Q_HEADS // N_KV_HEADS
    cols = G * ATTN_TQ
    gw = G * HEAD_DIM
    whole = lambda b: (b, 0, 0)
    return pl.pallas_call(
        _attn_kernel,
        grid=(B,),
        in_specs=[
            pl.BlockSpec((None, ATTN_WIDTH, S), whole),
            pl.BlockSpec((None, S, KV_WIDTH), whole),
            pl.BlockSpec((None, KV_WIDTH, S), whole),
        ],
        out_specs=pl.BlockSpec((None, N_KV_HEADS, S, gw), lambda b: (b, 0, 0, 0)),
        out_shape=jax.ShapeDtypeStruct((B, N_KV_HEADS, S, gw), BF16),
        scratch_shapes=[
            pltpu.VMEM((S, cols), F32),
            pltpu.VMEM((S, cols), BF16),
            pltpu.VMEM((N_KV_HEADS, HEAD_DIM + BF16_SUBLANES, S), BF16),
        ],
        compiler_params=pltpu.CompilerParams(
            dimension_semantics=("arbitrary",),
            vmem_limit_bytes=VMEM_LIMIT),
        name="attention",
    )(qt, k, vt)


def _hgrn_kernel(q_ref, i_ref, lff_ref, lfb_ref, g_ref, nw_ref, o_ref,
                 qm_ref, km_ref, qb_ref, kd_ref, dec_ref, ut_ref, st_ref):
    S = q_ref.shape[0]
    C = HG_CHUNK
    K = HG_EXPAND
    n_chunks = S // C
    nb = HG_BLOCK
    rows_b = nb * C
    r = lax.broadcasted_iota(jnp.int32, (C, C), 0)
    c = lax.broadcasted_iota(jnp.int32, (C, C), 1)
    lower = r >= c
    upper = r <= c
    tri2 = [jnp.concatenate([t, t], axis=1) for t in
            (jnp.where(lower, 1.0, 0.0).astype(BF16), jnp.where(upper, 1.0, 0.0).astype(BF16))]
    mid = (C // 2, C - 1 - C // 2)
    last = (C - 1, 0)

    def block_rows(t):
        return t * rows_b if isinstance(t, int) else pl.multiple_of(t * rows_b, rows_b)

    def prefix_sums(t):
        rows = pl.ds(block_rows(t), rows_b)
        lfs = (lff_ref[rows, :], lfb_ref[rows, :])
        b3s = []
        for d, lf in enumerate(lfs):
            hi = lf.astype(BF16)
            lo = (lf - hi.astype(F32)).astype(BF16)
            b3s.append(jnp.stack([
                jnp.dot(tri2[d],
                        jnp.concatenate([hi[j * C:(j + 1) * C], lo[j * C:(j + 1) * C]], axis=0),
                        preferred_element_type=F32)
                for j in range(nb)]))
        return lfs, b3s

    def state_updates(t):
        r0 = block_rows(t)
        for j in range(nb):
            cr = pl.ds(r0 + j * C, C)
            ivt = i_ref[cr, :].astype(F32).T.astype(BF16)
            ut_ref[t * nb + j] = jnp.dot(ivt, kd_ref[cr, :], preferred_element_type=F32)

    def scaled_operands(t, lfs, b3s):
        rows = pl.ds(block_rows(t), rows_b)
        q3 = q_ref[rows, :].astype(F32).reshape(nb, C, K)
        for d, (lf, b3) in enumerate(zip(lfs, b3s)):
            k3 = (1.0 - jnp.exp2(lf)).reshape(nb, C, K)
            bm = b3[:, mid[d]:mid[d] + 1, :]
            bl = b3[:, last[d]:last[d] + 1, :]
            e = jnp.exp2(b3 - bm)
            einv = 1.0 / e
            qm = q3 * e
            km = k3 * einv
            qm_ref[d, rows, :] = qm.reshape(rows_b, K).astype(BF16)
            km_ref[d, rows, :] = km.reshape(rows_b, K).astype(BF16)
            qb_ref[rows, d * K:(d + 1) * K] = (qm * jnp.exp2(bm)).reshape(rows_b, K).astype(BF16)
            kd_ref[rows, d * K:(d + 1) * K] = (km * jnp.exp2(bl - bm)).reshape(rows_b, K).astype(BF16)
            c0 = t * nb if isinstance(t, int) else pl.multiple_of(t * nb, nb)
            dec_ref[d, pl.ds(c0, nb), :] = jnp.exp2(bl).reshape(nb, K)

    n_blocks = n_chunks // nb
    scaled_operands(0, *prefix_sums(0))

    def prep(t, _):
        sums = prefix_sums(t)
        state_updates(t - 1)
        scaled_operands(t, *sums)
        return 0

    lax.fori_loop(1, n_blocks, prep, 0)
    state_updates(n_blocks - 1)

    def chain(n, states):
        sf, sb = states
        m = n_chunks - 1 - n
        st_ref[n, :, 0:K] = sf.astype(BF16)
        st_ref[m, :, K:2 * K] = sb.astype(BF16)
        return (dec_ref[0, pl.ds(n, 1), :] * sf + ut_ref[n, :, 0:K],
                dec_ref[1, pl.ds(m, 1), :] * sb + ut_ref[m, :, K:2 * K])

    zero_state = jnp.zeros((K, K), F32)
    lax.fori_loop(0, n_chunks, chain, (zero_state, zero_state))

    nw = nw_ref[...]
    nt_dims = (((1,), (1,)), ((), ()))

    no = HG_BLOCK_OUT

    def outp(t, _):
        r0 = pl.multiple_of(t * (no * C), no * C)
        crs = [pl.ds(r0 + j * C, C) for j in range(no)]
        a_f = [lax.dot_general(qm_ref[0, cr, :], km_ref[0, cr, :], nt_dims,
                               preferred_element_type=F32) for cr in crs]
        a_b = [lax.dot_general(qm_ref[1, cr, :], km_ref[1, cr, :], nt_dims,
                               preferred_element_type=F32) for cr in crs]
        a = [(jnp.where(lower, f, 0.0) + jnp.where(upper, b, 0.0)).astype(BF16)
             for f, b in zip(a_f, a_b)]
        outs = [jnp.dot(a[j], i_ref[cr, :], preferred_element_type=F32)
                + lax.dot_general(qb_ref[cr, :], st_ref[t * no + j], nt_dims,
                                  preferred_element_type=F32)
                for j, cr in enumerate(crs)]
        for cr, o in zip(crs, outs):
            ms = jnp.mean(o * o, axis=-1, keepdims=True)
            y = o * lax.rsqrt(ms + RMS_EPS) * nw
            o_ref[cr, :] = (y * g_ref[cr, :].astype(F32)).astype(o_ref.dtype)
        return 0

    lax.fori_loop(0, n_chunks // no, outp, 0)


def _hgrn_call(hq, hi, lf, hg, nw):
    B, S, _ = hq.shape
    n_chunks = S // HG_CHUNK
    K = HG_EXPAND
    blk = lambda b, h: (b, 0, h)
    return pl.pallas_call(
        _hgrn_kernel,
        grid=(B, HG_HEADS),
        in_specs=[
            pl.BlockSpec((None, S, K), blk),
            pl.BlockSpec((None, S, K), blk),
            pl.BlockSpec((None, S, K), blk),
            pl.BlockSpec((None, S, K), lambda b, h: (b, 0, HG_HEADS + h)),
            pl.BlockSpec((None, S, K), blk),
            pl.BlockSpec((1, K), lambda b, h: (0, 0)),
        ],
        out_specs=pl.BlockSpec((None, S, K), blk),
        out_shape=jax.ShapeDtypeStruct((B, S, HG_WIDTH), BF16),
        scratch_shapes=[
            pltpu.VMEM((2, S, K), BF16),
            pltpu.VMEM((2, S, K), BF16),
            pltpu.VMEM((S, 2 * K), BF16),
            pltpu.VMEM((S, 2 * K), BF16),
            pltpu.VMEM((2, n_chunks, K), F32),
            pltpu.VMEM((n_chunks, K, 2 * K), F32),
            pltpu.VMEM((n_chunks, K, 2 * K), BF16),
        ],
        compiler_params=pltpu.CompilerParams(
            dimension_semantics=("arbitrary", "arbitrary"),
            vmem_limit_bytes=VMEM_LIMIT),
        name="hgrn2",
    )(hq, hi, lf, lf, hg, nw)


def _out_kernel(x_ref, attn_ref, ag_ref, hgb_ref, anw_ref, w_ref, lnw_ref, lnb_ref, o_ref):
    a = jnp.concatenate([attn_ref[g] for g in range(N_KV_HEADS)], axis=1).astype(F32)
    ms = jnp.mean(a * a, axis=-1, keepdims=True)
    ab = a * lax.rsqrt(ms + RMS_EPS) * anw_ref[...] * ag_ref[...].astype(F32)
    y = (jnp.dot(ab.astype(BF16), w_ref[:ATTN_WIDTH, :], preferred_element_type=F32)
         + jnp.dot(hgb_ref[...], w_ref[ATTN_WIDTH:, :], preferred_element_type=F32))
    z = DEEPNORM_ALPHA * x_ref[...] + y
    mu = jnp.mean(z, axis=-1, keepdims=True)
    zc = z - mu
    var = jnp.mean(zc * zc, axis=-1, keepdims=True)
    o_ref[...] = zc * lax.rsqrt(var + LN_EPS) * lnw_ref[...] + lnb_ref[...]


def _out_call(x, attn, ag, hgb, anw, w_out_b, lnw, lnb):
    B, S, D = x.shape
    tm = OUT_TM
    row = lambda b, i: (b, i, 0)
    const2 = lambda b, i: (0, 0)
    return pl.pallas_call(
        _out_kernel,
        grid=(B, S // tm),
        in_specs=[
            pl.BlockSpec((None, tm, D), row),
            pl.BlockSpec((None, N_KV_HEADS, tm, ATTN_WIDTH // N_KV_HEADS), lambda b, i: (b, 0, i, 0)),
            pl.BlockSpec((None, tm, ATTN_WIDTH), row),
            pl.BlockSpec((None, tm, HG_WIDTH), row),
            pl.BlockSpec((1, ATTN_WIDTH), const2),
            pl.BlockSpec((ATTN_WIDTH + HG_WIDTH, D), const2),
            pl.BlockSpec((1, D), const2),
            pl.BlockSpec((1, D), const2),
        ],
        out_specs=pl.BlockSpec((None, tm, D), row),
        out_shape=jax.ShapeDtypeStruct((B, S, D), x.dtype),
        compiler_params=pltpu.CompilerParams(
            dimension_semantics=("arbitrary", "arbitrary"),
            vmem_limit_bytes=VMEM_LIMIT),
        name="out_proj_ln",
    )(x, attn, ag, hgb, anw, w_out_b, lnw, lnb)


def _rope_tables(seq_len):
    n_rows = seq_len // GRID_W
    inv = ROPE_THETA ** (-jnp.arange(0, AXIS_DIM, 2, dtype=F32) / AXIS_DIM)
    lane = jnp.arange(LANES, dtype=jnp.int32)
    d = lane % HEAD_DIM
    use_col = (d // AXIS_DIM) == 1
    dd = d % AXIS_DIM
    first = dd < (AXIS_DIM // 2)
    freq = inv[dd % (AXIS_DIM // 2)]
    idx = jnp.arange(max(n_rows, GRID_W), dtype=jnp.int32).astype(F32)
    ang = idx[:, None] * freq[None, :]
    cos_i = jnp.cos(ang)
    sin_i = jnp.where(first[None, :], -jnp.sin(ang), jnp.sin(ang))
    cos_i, sin_i = lax.optimization_barrier((cos_i, sin_i))

    def expand(tab):
        by_row = jnp.repeat(tab[:n_rows], GRID_W, axis=0)
        by_col = jnp.tile(tab[:GRID_W], (n_rows, 1))
        return jnp.where(use_col[None, :], by_col, by_row)

    return expand(cos_i), expand(sin_i)


def kernel(x, w_in, q_norm_w, k_norm_w, attn_norm_w, hg_lb_logits, hg_norm_w, w_out, ln_w, ln_b):
    B, S, D = x.shape
    assert w_in.shape == (DEPTH, D, IN_WIDTH) and DEPTH == 1
    cos_t, sin_t = _rope_tables(S)
    qw = jnp.tile(q_norm_w[0].astype(F32), N_Q_HEADS).reshape(1, ATTN_WIDTH)
    kw = jnp.tile(k_norm_w[0].astype(F32), N_KV_HEADS).reshape(1, KV_WIDTH)
    lbl = hg_lb_logits.astype(F32).reshape(2 * (DEPTH + 1), HG_WIDTH)
    hid = jnp.arange(ATTN_WIDTH, dtype=jnp.int32) // HEAD_DIM
    seg = jnp.where(hid[:, None] == hid[None, :], 1.0 / HEAD_DIM, 0.0).astype(BF16)

    qt, k, vt, ag, hq, hi, lf, hg = _proj_call(
        x, w_in[0].astype(BF16), cos_t, sin_t, qw, kw, lbl, seg)
    attn = _attn_call(qt, k, vt)
    hgb = _hgrn_call(hq, hi, lf, hg, hg_norm_w[0].astype(F32).reshape(1, HG_EXPAND))
    return _out_call(
        x, attn, ag, hgb,
        attn_norm_w[0].astype(F32).reshape(1, ATTN_WIDTH),
        w_out[0].astype(BF16),
        ln_w[0].astype(F32).reshape(1, D),
        ln_b[0].astype(F32).reshape(1, D))
```

```python
import jax
import jax.numpy as jnp
from jax import lax
from jax.experimental import pallas as pl
from jax.experimental.pallas import tpu as pltpu

F32 = jnp.float32
BF16 = jnp.bfloat16

GRID_W = 64
ATTN_WIDTH = 512
HEAD_DIM = 64
N_Q_HEADS = 8
N_KV_HEADS = 2
KV_WIDTH = 128
AXIS_DIM = 32
ROPE_THETA = 10000.0
HG_WIDTH = 512
HG_EXPAND = 128
HG_HEADS = 4
HG_CHUNK = 64
RMS_EPS = 1e-6
LN_EPS = 1e-5
DEPTH = 1
DEEPNORM_ALPHA = (2 * DEPTH) ** 0.25
IN_WIDTH = 3840
LOG2E = 1.4426950408889634
Q_SCALE = HEAD_DIM ** -0.5 * LOG2E

LANES = 128
BF16_SUBLANES = 16
VMEM_LIMIT = 56 * 1024 * 1024

PROJ_TM = 512
ATTN_TQ = 128
ATTN_LAG = 2
ATTN_TK = 256
HG_BLOCK = 32
HG_BLOCK_OUT = 64
OUT_TM = 1024
OUT_RING = 3


def _exact_zero(x):
    bits = lax.bitcast_convert_type(x, jnp.uint32)
    return ((bits >> 16) >> 16).astype(F32)


def _sigmoid(x):
    return 1.0 / (1.0 + jnp.exp(-x))


def _proj_kernel(x_ref, w_ref, cos_ref, sin_ref, qw_ref, kw_ref, lbl_ref, seg_ref,
                 qt_ref, k_ref, vt_ref, ag_ref, hq_ref, hi_ref, lf_ref, hg_ref):
    xb = x_ref[...].astype(BF16)
    tm = xb.shape[0]

    def proj(a, b):
        return jnp.dot(xb, w_ref[:, a:b], preferred_element_type=F32)

    cosv = cos_ref[...]
    sinv = sin_ref[...]
    lane = lax.broadcasted_iota(jnp.int32, (tm, LANES), 1)
    first_half = (lane % AXIS_DIM) < (AXIS_DIM // 2)

    def norm_rope(a, w_row, seg, scale):
        sq = (a * a).astype(BF16)
        blk = min(a.shape[1], 2 * LANES)
        ms = jnp.concatenate(
            [jnp.dot(sq[:, j:j + blk], seg[j:j + blk, j:j + blk], preferred_element_type=F32)
             for j in range(0, a.shape[1], blk)], axis=1)
        y = a * lax.rsqrt(ms + RMS_EPS) * w_row
        outs = []
        for c in range(a.shape[1] // LANES):
            yc = y[:, c * LANES:(c + 1) * LANES]
            partner = jnp.where(first_half,
                                pltpu.roll(yc, LANES - AXIS_DIM // 2, 1),
                                pltpu.roll(yc, AXIS_DIM // 2, 1))
            outs.append((yc * cosv + partner * sinv) * scale)
        return outs

    aq = proj(0, ATTN_WIDTH)
    qs = norm_rope(aq, qw_ref[...], seg_ref[...], Q_SCALE)
    for c, qc in enumerate(qs):
        qt_ref[c * LANES:(c + 1) * LANES, :] = qc.T.astype(BF16)
    akv = proj(ATTN_WIDTH, ATTN_WIDTH + 2 * KV_WIDTH)
    kk = norm_rope(akv[:, :KV_WIDTH], kw_ref[...], seg_ref[:KV_WIDTH, :KV_WIDTH], 1.0)[0]
    k_ref[...] = kk.astype(BF16)
    vt_ref[...] = akv[:, KV_WIDTH:].T.astype(BF16)

    base = ATTN_WIDTH + 2 * KV_WIDTH
    ag = proj(base, base + ATTN_WIDTH)
    ag_ref[...] = (ag * _sigmoid(ag)).astype(BF16)
    base += ATTN_WIDTH
    hq = proj(base, base + HG_WIDTH)
    hq_ref[...] = (hq * _sigmoid(hq) * (HG_EXPAND ** -0.5)).astype(BF16)
    base += HG_WIDTH
    hi_ref[...] = proj(base, base + HG_WIDTH).astype(BF16)
    base += HG_WIDTH
    lbl = lbl_ref[...]
    for d in range(2):
        l0 = lbl[2 * d:2 * d + 1, :]
        l1 = lbl[2 * d + 1:2 * d + 2, :]
        mx = jnp.maximum(l0, l1)
        e0 = jnp.exp(l0 - mx)
        e1 = jnp.exp(l1 - mx)
        lb = e0 / (e0 + e1)
        z = proj(base, base + HG_WIDTH)
        lf_ref[:, d * HG_WIDTH:(d + 1) * HG_WIDTH] = jnp.log(lb + (1.0 - lb) * _sigmoid(z)) * LOG2E
        base += HG_WIDTH
    hg = proj(base, base + HG_WIDTH)
    hg_ref[...] = (hg * _sigmoid(hg)).astype(BF16)


def _proj_call(x, w_in_b, cos_t, sin_t, qw, kw, lbl, seg):
    B, S, D = x.shape
    tm = PROJ_TM
    nt = S // tm
    row = lambda b, i: (b, i, 0)
    const2 = lambda b, i: (0, 0)
    out_shape = (
        jax.ShapeDtypeStruct((B, ATTN_WIDTH, S), BF16),
        jax.ShapeDtypeStruct((B, S, KV_WIDTH), BF16),
        jax.ShapeDtypeStruct((B, KV_WIDTH, S), BF16),
        jax.ShapeDtypeStruct((B, S, ATTN_WIDTH), BF16),
        jax.ShapeDtypeStruct((B, S, HG_WIDTH), BF16),
        jax.ShapeDtypeStruct((B, S, HG_WIDTH), BF16),
        jax.ShapeDtypeStruct((B, S, 2 * HG_WIDTH), F32),
        jax.ShapeDtypeStruct((B, S, HG_WIDTH), BF16),
    )
    in_specs = [
        pl.BlockSpec((None, tm, D), row),
        pl.BlockSpec((D, IN_WIDTH), const2),
        pl.BlockSpec((tm, LANES), lambda b, i: (i, 0)),
        pl.BlockSpec((tm, LANES), lambda b, i: (i, 0)),
        pl.BlockSpec((1, ATTN_WIDTH), const2),
        pl.BlockSpec((1, KV_WIDTH), const2),
        pl.BlockSpec((4, HG_WIDTH), const2),
        pl.BlockSpec((ATTN_WIDTH, ATTN_WIDTH), const2),
    ]
    out_specs = (
        pl.BlockSpec((None, ATTN_WIDTH, tm), lambda b, i: (b, 0, i)),
        pl.BlockSpec((None, tm, KV_WIDTH), row),
        pl.BlockSpec((None, KV_WIDTH, tm), lambda b, i: (b, 0, i)),
        pl.BlockSpec((None, tm, ATTN_WIDTH), row),
        pl.BlockSpec((None, tm, HG_WIDTH), row),
        pl.BlockSpec((None, tm, HG_WIDTH), row),
        pl.BlockSpec((None, tm, 2 * HG_WIDTH), row),
        pl.BlockSpec((None, tm, HG_WIDTH), row),
    )
    return pl.pallas_call(
        _proj_kernel,
        grid=(B, nt),
        in_specs=in_specs,
        out_specs=out_specs,
        out_shape=out_shape,
        compiler_params=pltpu.CompilerParams(
            dimension_semantics=("arbitrary", "arbitrary"),
            vmem_limit_bytes=VMEM_LIMIT),
        name="in_proj",
    )(x, w_in_b, cos_t, sin_t, qw, kw, lbl, seg)


def _attn_kernel(qt_ref, k_ref, vt_ref, o_ref, s_ref, p_ref, vx_ref):
    S = k_ref.shape[0]
    G = N_Q_HEADS // N_KV_HEADS
    tq, tk = ATTN_TQ, ATTN_TK
    cols = G * tq
    n_units = N_KV_HEADS * (S // tq)
    n_c = S // tk
    vrows = vx_ref.shape[1]

    for g in range(N_KV_HEADS):
        vx_ref[g, 0:HEAD_DIM, :] = vt_ref[g * HEAD_DIM:(g + 1) * HEAD_DIM, :]
        vx_ref[g, HEAD_DIM:vrows, :] = jnp.ones((vrows - HEAD_DIM, S), BF16)

    row_group = lax.broadcasted_iota(jnp.int32, (N_KV_HEADS * HEAD_DIM, cols), 0) // HEAD_DIM

    def q_weights(u):
        i, g = u >> 1, u & 1
        c0 = pl.multiple_of(i * tq, tq)
        qs = jnp.concatenate(
            [qt_ref[pl.ds(pl.multiple_of((G * g + h) * HEAD_DIM, HEAD_DIM), HEAD_DIM),
                    pl.ds(c0, tq)] for h in range(G)], axis=1)
        return jnp.where(row_group == g, jnp.concatenate([qs, qs], axis=0), 0.0).astype(BF16)

    def stage(u, m_prev, do_scores=True, do_probs=True, do_pv=True):
        if do_scores:
            w = q_weights(u)
            mx = jnp.full((8, cols), -jnp.inf, F32)
        if do_pv:
            g_pv = (u - 2) & 1
            acc = jnp.zeros((vrows, cols), F32)
        acc_after = []
        for c in range(n_c + ATTN_LAG):
            if do_pv and c < n_c:
                rows = pl.ds(c * tk, tk)
                acc = acc + jnp.dot(vx_ref[g_pv, :, rows], p_ref[rows, :],
                                    preferred_element_type=F32)
                acc_after.append(acc)
            if c < ATTN_LAG:
                continue
            rows = pl.ds((c - ATTN_LAG) * tk, tk)
            m_c = m_prev
            if do_pv and do_probs:
                done = acc_after[c - ATTN_LAG]
                guard = sum(_exact_zero(done[0:8, t * 2 * LANES:(t * 2 + 1) * LANES])
                            for t in range(cols // (2 * LANES)))
                m_c = m_prev + jnp.tile(guard[0:1, :], (1, cols // LANES))
            if do_probs:
                p_ref[rows, :] = jnp.exp2(s_ref[rows, :] - m_c).astype(BF16)
            if do_scores:
                s = jnp.dot(k_ref[rows, :], w, preferred_element_type=F32)
                s_ref[rows, :] = s
                mx = jnp.maximum(mx, jnp.max(s.reshape(tk // 8, 8, cols), axis=0))
        if do_pv:
            o_t = acc[0:HEAD_DIM, :] / acc[HEAD_DIM:HEAD_DIM + 1, :]
            o = o_t.T
            blk = jnp.concatenate([o[h * tq:(h + 1) * tq, :] for h in range(G)], axis=1)
            r0 = pl.multiple_of(((u - 2) >> 1) * tq, tq)
            o_ref[g_pv, pl.ds(r0, tq), :] = blk.astype(o_ref.dtype)
        if do_scores:
            return jnp.max(mx, axis=0, keepdims=True)
        return m_prev

    zero = jnp.int32(0)
    m = stage(zero, None, do_probs=False, do_pv=False)
    m = stage(zero + 1, m, do_pv=False)
    m = lax.fori_loop(2, n_units, stage, m)
    m = stage(zero + n_units, m, do_scores=False)
    stage(zero + n_units + 1, m, do_scores=False, do_probs=False)


def _attn_call(qt, k, vt):
    B, _, S = qt.shape
    G = N_Q_HEADS // N_KV_HEADS
    cols = G * ATTN_TQ
    gw = G * HEAD_DIM
    whole = lambda b: (b, 0, 0)
    return pl.pallas_call(
        _attn_kernel,
        grid=(B,),
        in_specs=[
            pl.BlockSpec((None, ATTN_WIDTH, S), whole),
            pl.BlockSpec((None, S, KV_WIDTH), whole),
            pl.BlockSpec((None, KV_WIDTH, S), whole),
        ],
        out_specs=pl.BlockSpec((None, N_KV_HEADS, S, gw), lambda b: (b, 0, 0, 0)),
        out_shape=jax.ShapeDtypeStruct((B, N_KV_HEADS, S, gw), BF16),
        scratch_shapes=[
            pltpu.VMEM((S, cols), F32),
            pltpu.VMEM((S, cols), BF16),
            pltpu.VMEM((N_KV_HEADS, HEAD_DIM + BF16_SUBLANES, S), BF16),
        ],
        compiler_params=pltpu.CompilerParams(
            dimension_semantics=("arbitrary",),
            vmem_limit_bytes=VMEM_LIMIT),
        name="attention",
    )(qt, k, vt)


def _hgrn_kernel(q_ref, i_ref, lff_ref, lfb_ref, g_ref, nw_ref, o_ref,
                 qm_ref, km_ref, qb_ref, kd_ref, dec_ref, ut_ref, st_ref):
    S = q_ref.shape[0]
    C = HG_CHUNK
    K = HG_EXPAND
    n_chunks = S // C
    nb = HG_BLOCK
    rows_b = nb * C
    r = lax.broadcasted_iota(jnp.int32, (C, C), 0)
    c = lax.broadcasted_iota(jnp.int32, (C, C), 1)
    lower = r >= c
    upper = r <= c
    tri2 = [jnp.concatenate([t, t], axis=1) for t in
            (jnp.where(lower, 1.0, 0.0).astype(BF16), jnp.where(upper, 1.0, 0.0).astype(BF16))]
    mid = (C // 2, C - 1 - C // 2)
    last = (C - 1, 0)

    def block_rows(t):
        return t * rows_b if isinstance(t, int) else pl.multiple_of(t * rows_b, rows_b)

    def prefix_sums(t):
        rows = pl.ds(block_rows(t), rows_b)
        lfs = (lff_ref[rows, :], lfb_ref[rows, :])
        b3s = []
        for d, lf in enumerate(lfs):
            hi = lf.astype(BF16)
            lo = (lf - hi.astype(F32)).astype(BF16)
            b3s.append(jnp.stack([
                jnp.dot(tri2[d],
                        jnp.concatenate([hi[j * C:(j + 1) * C], lo[j * C:(j + 1) * C]], axis=0),
                        preferred_element_type=F32)
                for j in range(nb)]))
        return lfs, b3s

    def state_updates(t):
        r0 = block_rows(t)
        for j in range(nb):
            cr = pl.ds(r0 + j * C, C)
            ivt = i_ref[cr, :].astype(F32).T.astype(BF16)
            ut_ref[t * nb + j] = jnp.dot(ivt, kd_ref[cr, :], preferred_element_type=F32)

    def scaled_operands(t, lfs, b3s):
        rows = pl.ds(block_rows(t), rows_b)
        q3 = q_ref[rows, :].astype(F32).reshape(nb, C, K)
        for d, (lf, b3) in enumerate(zip(lfs, b3s)):
            k3 = (1.0 - jnp.exp2(lf)).reshape(nb, C, K)
            bm = b3[:, mid[d]:mid[d] + 1, :]
            bl = b3[:, last[d]:last[d] + 1, :]
            e = jnp.exp2(b3 - bm)
            einv = 1.0 / e
            qm = q3 * e
            km = k3 * einv
            qm_ref[d, rows, :] = qm.reshape(rows_b, K).astype(BF16)
            km_ref[d, rows, :] = km.reshape(rows_b, K).astype(BF16)
            qb_ref[rows, d * K:(d + 1) * K] = (qm * jnp.exp2(bm)).reshape(rows_b, K).astype(BF16)
            kd_ref[rows, d * K:(d + 1) * K] = (km * jnp.exp2(bl - bm)).reshape(rows_b, K).astype(BF16)
            c0 = t * nb if isinstance(t, int) else pl.multiple_of(t * nb, nb)
            dec_ref[d, pl.ds(c0, nb), :] = jnp.exp2(bl).reshape(nb, K)

    n_blocks = n_chunks // nb
    scaled_operands(0, *prefix_sums(0))

    def prep(t, _):
        sums = prefix_sums(t)
        state_updates(t - 1)
        scaled_operands(t, *sums)
        return 0

    lax.fori_loop(1, n_blocks, prep, 0)
    state_updates(n_blocks - 1)

    def chain(n, states):
        sf, sb = states
        m = n_chunks - 1 - n
        st_ref[n, :, 0:K] = sf.astype(BF16)
        st_ref[m, :, K:2 * K] = sb.astype(BF16)
        return (dec_ref[0, pl.ds(n, 1), :] * sf + ut_ref[n, :, 0:K],
                dec_ref[1, pl.ds(m, 1), :] * sb + ut_ref[m, :, K:2 * K])

    zero_state = jnp.zeros((K, K), F32)
    lax.fori_loop(0, n_chunks, chain, (zero_state, zero_state))

    nw = nw_ref[...]
    nt_dims = (((1,), (1,)), ((), ()))

    no = HG_BLOCK_OUT

    def outp(t, _):
        r0 = pl.multiple_of(t * (no * C), no * C)
        crs = [pl.ds(r0 + j * C, C) for j in range(no)]
        a_f = [lax.dot_general(qm_ref[0, cr, :], km_ref[0, cr, :], nt_dims,
                               preferred_element_type=F32) for cr in crs]
        a_b = [lax.dot_general(qm_ref[1, cr, :], km_ref[1, cr, :], nt_dims,
                               preferred_element_type=F32) for cr in crs]
        a = [(jnp.where(lower, f, 0.0) + jnp.where(upper, b, 0.0)).astype(BF16)
             for f, b in zip(a_f, a_b)]
        outs = [jnp.dot(a[j], i_ref[cr, :], preferred_element_type=F32)
                + lax.dot_general(qb_ref[cr, :], st_ref[t * no + j], nt_dims,
                                  preferred_element_type=F32)
                for j, cr in enumerate(crs)]
        for cr, o in zip(crs, outs):
            ms = jnp.mean(o * o, axis=-1, keepdims=True)
            y = o * lax.rsqrt(ms + RMS_EPS) * nw
            o_ref[cr, :] = (y * g_ref[cr, :].astype(F32)).astype(o_ref.dtype)
        return 0

    lax.fori_loop(0, n_chunks // no, outp, 0)


def _hgrn_call(hq, hi, lf, hg, nw):
    B, S, _ = hq.shape
    n_chunks = S // HG_CHUNK
    K = HG_EXPAND
    blk = lambda b, h: (b, 0, h)
    return pl.pallas_call(
        _hgrn_kernel,
        grid=(B, HG_HEADS),
        in_specs=[
            pl.BlockSpec((None, S, K), blk),
            pl.BlockSpec((None, S, K), blk),
            pl.BlockSpec((None, S, K), blk),
            pl.BlockSpec((None, S, K), lambda b, h: (b, 0, HG_HEADS + h)),
            pl.BlockSpec((None, S, K), blk),
            pl.BlockSpec((1, K), lambda b, h: (0, 0)),
        ],
        out_specs=pl.BlockSpec((None, S, K), blk),
        out_shape=jax.ShapeDtypeStruct((B, S, HG_WIDTH), BF16),
        scratch_shapes=[
            pltpu.VMEM((2, S, K), BF16),
            pltpu.VMEM((2, S, K), BF16),
            pltpu.VMEM((S, 2 * K), BF16),
            pltpu.VMEM((S, 2 * K), BF16),
            pltpu.VMEM((2, n_chunks, K), F32),
            pltpu.VMEM((n_chunks, K, 2 * K), F32),
            pltpu.VMEM((n_chunks, K, 2 * K), BF16),
        ],
        compiler_params=pltpu.CompilerParams(
            dimension_semantics=("arbitrary", "arbitrary"),
            vmem_limit_bytes=VMEM_LIMIT),
        name="hgrn2",
    )(hq, hi, lf, lf, hg, nw)


def _out_kernel(x_hbm, attn_ref, ag_ref, hgb_ref, anw_ref, w_ref, lnw_ref, lnb_ref, o_ref,
                xbuf, sem):
    nt = pl.num_programs(1)
    n_steps = pl.num_programs(0) * nt
    step = pl.program_id(0) * nt + pl.program_id(1)
    tm = xbuf.shape[1]

    def x_copy(st):
        return pltpu.make_async_copy(
            x_hbm.at[st // nt, pl.ds(pl.multiple_of((st % nt) * tm, tm), tm)],
            xbuf.at[st % OUT_RING], sem.at[st % OUT_RING])

    @pl.when(step == 0)
    def _():
        for st in range(OUT_RING - 1):
            x_copy(step + st).start()

    @pl.when(step + (OUT_RING - 1) < n_steps)
    def _():
        x_copy(step + (OUT_RING - 1)).start()

    x_copy(step).wait()
    x = xbuf[step % OUT_RING]

    a = jnp.concatenate([attn_ref[g] for g in range(N_KV_HEADS)], axis=1).astype(F32)
    ms = jnp.mean(a * a, axis=-1, keepdims=True)
    ab = a * lax.rsqrt(ms + RMS_EPS) * anw_ref[...] * ag_ref[...].astype(F32)
    y = (jnp.dot(ab.astype(BF16), w_ref[:ATTN_WIDTH, :], preferred_element_type=F32)
         + jnp.dot(hgb_ref[...], w_ref[ATTN_WIDTH:, :], preferred_element_type=F32))
    z = DEEPNORM_ALPHA * x + y
    mu = jnp.mean(z, axis=-1, keepdims=True)
    zc = z - mu
    var = jnp.mean(zc * zc, axis=-1, keepdims=True)
    o_ref[...] = zc * lax.rsqrt(var + LN_EPS) * lnw_ref[...] + lnb_ref[...]


def _out_call(x, attn, ag, hgb, anw, w_out_b, lnw, lnb):
    B, S, D = x.shape
    tm = OUT_TM
    row = lambda b, i: (b, i, 0)
    const2 = lambda b, i: (0, 0)
    return pl.pallas_call(
        _out_kernel,
        grid=(B, S // tm),
        in_specs=[
            pl.BlockSpec(memory_space=pl.ANY),
            pl.BlockSpec((None, N_KV_HEADS, tm, ATTN_WIDTH // N_KV_HEADS), lambda b, i: (b, 0, i, 0)),
            pl.BlockSpec((None, tm, ATTN_WIDTH), row),
            pl.BlockSpec((None, tm, HG_WIDTH), row),
            pl.BlockSpec((1, ATTN_WIDTH), const2),
            pl.BlockSpec((ATTN_WIDTH + HG_WIDTH, D), const2),
            pl.BlockSpec((1, D), const2),
            pl.BlockSpec((1, D), const2),
        ],
        out_specs=pl.BlockSpec((None, tm, D), row),
        out_shape=jax.ShapeDtypeStruct((B, S, D), x.dtype),
        scratch_shapes=[pltpu.VMEM((OUT_RING, tm, D), x.dtype),
                        pltpu.SemaphoreType.DMA((OUT_RING,))],
        compiler_params=pltpu.CompilerParams(
            dimension_semantics=("arbitrary", "arbitrary"),
            vmem_limit_bytes=VMEM_LIMIT),
        name="out_proj_ln",
    )(x, attn, ag, hgb, anw, w_out_b, lnw, lnb)


def _rope_tables(seq_len):
    n_rows = seq_len // GRID_W
    inv = ROPE_THETA ** (-jnp.arange(0, AXIS_DIM, 2, dtype=F32) / AXIS_DIM)
    lane = jnp.arange(LANES, dtype=jnp.int32)
    d = lane % HEAD_DIM
    use_col = (d // AXIS_DIM) == 1
    dd = d % AXIS_DIM
    first = dd < (AXIS_DIM // 2)
    freq = inv[dd % (AXIS_DIM // 2)]
    idx = jnp.arange(max(n_rows, GRID_W), dtype=jnp.int32).astype(F32)
    ang = idx[:, None] * freq[None, :]
    cos_i = jnp.cos(ang)
    sin_i = jnp.where(first[None, :], -jnp.sin(ang), jnp.sin(ang))
    cos_i, sin_i = lax.optimization_barrier((cos_i, sin_i))

    def expand(tab):
        by_row = jnp.repeat(tab[:n_rows], GRID_W, axis=0)
        by_col = jnp.tile(tab[:GRID_W], (n_rows, 1))
        return jnp.where(use_col[None, :], by_col, by_row)

    return expand(cos_i), expand(sin_i)


def kernel(x, w_in, q_norm_w, k_norm_w, attn_norm_w, hg_lb_logits, hg_norm_w, w_out, ln_w, ln_b):
    B, S, D = x.shape
    assert w_in.shape == (DEPTH, D, IN_WIDTH) and DEPTH == 1
    cos_t, sin_t = _rope_tables(S)
    qw = jnp.tile(q_norm_w[0].astype(F32), N_Q_HEADS).reshape(1, ATTN_WIDTH)
    kw = jnp.tile(k_norm_w[0].astype(F32), N_KV_HEADS).reshape(1, KV_WIDTH)
    lbl = hg_lb_logits.astype(F32).reshape(2 * (DEPTH + 1), HG_WIDTH)
    hid = jnp.arange(ATTN_WIDTH, dtype=jnp.int32) // HEAD_DIM
    seg = jnp.where(hid[:, None] == hid[None, :], 1.0 / HEAD_DIM, 0.0).astype(BF16)

    qt, k, vt, ag, hq, hi, lf, hg = _proj_call(
        x, w_in[0].astype(BF16), cos_t, sin_t, qw, kw, lbl, seg)
    attn = _attn_call(qt, k, vt)
    hgb = _hgrn_call(hq, hi, lf, hg, hg_norm_w[0].astype(F32).reshape(1, HG_EXPAND))
    return _out_call(
        x, attn, ag, hgb,
        attn_norm_w[0].astype(F32).reshape(1, ATTN_WIDTH),
        w_out[0].astype(BF16),
        ln_w[0].astype(F32).reshape(1, D),
        ln_b[0].astype(F32).reshape(1, D))
```
